```python
import functools
import jax, jax.numpy as jnp
from jax import lax
import numpy as np

D_MODEL = 2048
BATCH = 1
SEQ = 8192
DEPTH = 4

GRID_W = 64
CTX_LEN = 256
EPS = 1e-6
N_MOD = 6

RET_DK = 128
RET_DV = 128
RET_WIDTH = D_MODEL // 2
RET_HEADS = RET_WIDTH // RET_DV
RET_CHUNK = 128
K_SCALE = RET_DK ** -0.5
ROPE_THETA = 10000.0

POOL_WIDTH = D_MODEL // 4
POOL_WINDOWS = (2, 4, 8, 16)
POOL_GROUPS = len(POOL_WINDOWS)
POOL_GROUP_DIM = POOL_WIDTH // POOL_GROUPS

SG_WIDTH = D_MODEL // 4
SG_GROUPS = 4
SG_GROUP_DIM = SG_WIDTH // SG_GROUPS
SG_CHUNK = 128

N_BRANCH = 3
D_FF = 4 * D_MODEL
N_IN = 4 * RET_WIDTH + POOL_WIDTH + 2 * SG_WIDTH + N_BRANCH * D_MODEL
SPLIT_IDX = (RET_WIDTH, 2 * RET_WIDTH, 3 * RET_WIDTH, 4 * RET_WIDTH,
             4 * RET_WIDTH + POOL_WIDTH, 4 * RET_WIDTH + POOL_WIDTH + SG_WIDTH,
             4 * RET_WIDTH + POOL_WIDTH + 2 * SG_WIDTH)

kernel_name = "hybrid_retention_pool_sgmlp_diffusion_trunk"

F32 = jnp.float32


def rms_norm(x, g):
    xf = x.astype(F32)
    y = xf * lax.rsqrt(jnp.mean(xf * xf, axis=-1, keepdims=True) + EPS)
    return (y * g.astype(F32)).astype(x.dtype)


def modulate(h, shift, scale):
    return h * (1 + scale) + shift


def to_heads(t):
    b, tl, _ = t.shape
    return t.astype(F32).reshape(b, tl, RET_HEADS, -1).transpose(0, 2, 1, 3)


def axial_rope(t, rows, cols):
    half = RET_DK // 2
    nf = half // 2
    inv = ROPE_THETA ** (-jnp.arange(nf, dtype=F32) / nf)

    def rot(tp, pos):
        ang = pos[:, None] * inv[None]
        cos, sin = jnp.cos(ang), jnp.sin(ang)
        t1, t2 = tp[..., :nf], tp[..., nf:]
        return jnp.concatenate([t1 * cos - t2 * sin, t1 * sin + t2 * cos], axis=-1)

    return jnp.concatenate([rot(t[..., :half], rows), rot(t[..., half:], cols)], axis=-1)


def retention_chunkwise(q, k, v, log_gamma, state0):
    b, h, t, dk = q.shape
    n = t // RET_CHUNK
    qc = q.reshape(b, h, n, RET_CHUNK, dk)
    kc = k.reshape(b, h, n, RET_CHUNK, dk)
    vc = v.reshape(b, h, n, RET_CHUNK, -1)
    idx = jnp.arange(RET_CHUNK, dtype=F32)
    dist = idx[:, None] - idx[None, :]
    decay = jnp.where(dist >= 0, jnp.exp(log_gamma[:, None, None] * jnp.maximum(dist, 0.0)), 0.0)
    scores = jnp.einsum('bhncd,bhnsd->bhncs', qc, kc) * decay[None, :, None]
    o_intra = jnp.einsum('bhncs,bhnse->bhnce', scores, vc)
    q_decay = jnp.exp(log_gamma[:, None] * (idx + 1.0)[None])[None, :, None, :, None]
    k_decay = jnp.exp(log_gamma[:, None] * (RET_CHUNK - 1.0 - idx)[None])[None, :, None, :, None]
    kv = jnp.einsum('bhnsd,bhnse->bhnde', kc * k_decay, vc)
    chunk_decay = jnp.exp(log_gamma * RET_CHUNK)[None, :, None, None]

    def step(state, kv_n):
        return state * chunk_decay + kv_n, state

    final, states = lax.scan(step, state0, jnp.moveaxis(kv, 2, 0))
    states = jnp.moveaxis(states, 0, 2)
    o_cross = jnp.einsum('bhncd,bhnde->bhnce', qc * q_decay, states)
    return (o_intra + o_cross).reshape(b, h, t, -1), final


def retention_final_state(k, v, log_gamma):
    t = k.shape[2]
    w = jnp.exp(log_gamma[:, None] * (t - 1.0 - jnp.arange(t, dtype=F32))[None])
    return jnp.einsum('bhtd,bhte,ht->bhde', k, v, w)


def bidir_retention(q, k, v, lg_f, lg_b, sf0, sb0):
    of, sf = retention_chunkwise(q, k, v, lg_f, sf0)
    ob, sb = retention_chunkwise(jnp.flip(q, 2), jnp.flip(k, 2), jnp.flip(v, 2), lg_b, sb0)
    return of + jnp.flip(ob, 2), sf, sb


def retention_out(o, gate, norm_g):
    o = o * lax.rsqrt(jnp.mean(o * o, axis=-1, keepdims=True) + EPS)
    b, h, t, dv = o.shape
    o = o.transpose(0, 2, 1, 3).reshape(b, t, h * dv) * norm_g.astype(F32)
    return (o * jax.nn.silu(gate.astype(F32))).astype(gate.dtype)


def multiscale_pool(p, seg_len, pool_w, pool_scale):
    b, t, _ = p.shape
    ns = t // seg_len
    pf = p.astype(F32).reshape(b, ns, seg_len, POOL_GROUPS, POOL_GROUP_DIM)
    cs = jnp.pad(jnp.cumsum(pf, axis=2), ((0, 0), (0, 0), (1, 0), (0, 0), (0, 0)))
    pos = jnp.arange(seg_len)
    outs = []
    for gi, w in enumerate(POOL_WINDOWS):
        lo = jnp.maximum(pos - w // 2, 0)
        hi = jnp.minimum(pos + w // 2 - 1, seg_len - 1)
        cnt = (hi - lo + 1).astype(F32)
        cg = cs[:, :, :, gi, :]
        mean = (cg[:, :, hi + 1] - cg[:, :, lo]) / cnt[:, None]
        outs.append(mean - pf[:, :, :, gi, :])
    y = jnp.stack(outs, axis=3)
    y = jnp.einsum('bnsgc,gcd->bnsgd', y, pool_w)
    return (y.reshape(b, t, POOL_WIDTH) * pool_scale).astype(p.dtype)


def chunk_spatial_gating(u, v, v_norm_g, w_s, b_s):
    b, t, _ = u.shape
    u = jax.nn.gelu(u)
    v = rms_norm(jax.nn.gelu(v), v_norm_g)
    vc = v.reshape(b, t // SG_CHUNK, SG_CHUNK, SG_GROUPS, SG_GROUP_DIM)
    mixed = jnp.einsum('gij,bnjgc->bnigc', w_s, vc) + b_s.T[None, None, :, :, None]
    return u * mixed.reshape(b, t, SG_WIDTH).astype(u.dtype)


def token_mixer(z, rope_pos, seg_len, sf0, sb0, lg_f, lg_b, ret_norm_g, pool_w, pool_scale,
                sg_norm_g, sg_w, sg_b, w_br, w_bp, w_bs, w_out):
    zq, zk, zv, zg, zp, zu, zsv, zgate = jnp.split(z, SPLIT_IDX, axis=-1)
    q = to_heads(zq)
    k = to_heads(zk) * K_SCALE
    v = to_heads(zv)
    if rope_pos is not None:
        q = axial_rope(q, *rope_pos)
        k = axial_rope(k, *rope_pos)
    o, sf, sb = bidir_retention(q, k, v, lg_f, lg_b, sf0, sb0)
    ret = retention_out(o, zg, ret_norm_g)
    pool = multiscale_pool(zp, seg_len, pool_w, pool_scale)
    sg = chunk_spatial_gating(zu, zsv, sg_norm_g, sg_w, sg_b)
    g_r, g_p, g_s = jnp.split(jax.nn.sigmoid(zgate), N_BRANCH, axis=-1)
    y = g_r * (ret @ w_br) + g_p * (pool @ w_bp) + g_s * (sg @ w_bs)
    return y @ w_out, sf, sb


def sq_relu_mlp(h, w1, w2):
    a = jax.nn.relu(h @ w1)
    return (a * a) @ w2


def setup_inputs(seed: int = 0) -> dict:
    key = jax.random.key(seed)
    ks = jax.random.split(key, 24)

    def nrm(k, shape, scale):
        return jax.random.normal(k, shape, F32) * scale

    gamma0 = 1.0 - 2.0 ** (-5.0 - np.arange(RET_HEADS))
    logit0 = jnp.asarray(np.log(gamma0 / (1.0 - gamma0)).astype(np.float32))
    return {
        "x": nrm(ks[0], (BATCH, SEQ, D_MODEL), 1.0),
        "c": nrm(ks[1], (BATCH, D_MODEL), 1.0),
        "ctx": nrm(ks[2], (BATCH, CTX_LEN, D_MODEL), 1.0),
        "c_ctx": nrm(ks[3], (D_MODEL,), 1.0),
        "w_ada": nrm(ks[4], (DEPTH, D_MODEL, N_MOD * D_MODEL), D_MODEL ** -0.5),
        "b_ada": nrm(ks[5], (DEPTH, N_MOD * D_MODEL), 0.02),
        "norm1_g": 1.0 + nrm(ks[6], (DEPTH, D_MODEL), 0.02),
        "w_in": nrm(ks[7], (DEPTH, D_MODEL, N_IN), D_MODEL ** -0.5),
        "ret_decay_logit": logit0[None, None, :] + nrm(ks[8], (DEPTH, 2, RET_HEADS), 0.1),
        "ret_norm_g": 1.0 + nrm(ks[9], (DEPTH, RET_WIDTH), 0.02),
        "pool_w": nrm(ks[10], (DEPTH, POOL_GROUPS, POOL_GROUP_DIM, POOL_GROUP_DIM), POOL_GROUP_DIM ** -0.5),
        "pool_scale": 1.0 + nrm(ks[11], (DEPTH, POOL_WIDTH), 0.1),
        "sg_norm_g": 1.0 + nrm(ks[12], (DEPTH, SG_WIDTH), 0.02),
        "sg_w": nrm(ks[13], (DEPTH, SG_GROUPS, SG_CHUNK, SG_CHUNK), SG_CHUNK ** -0.5),
        "sg_b": 1.0 + nrm(ks[14], (DEPTH, SG_GROUPS, SG_CHUNK), 0.02),
        "w_br": nrm(ks[15], (DEPTH, RET_WIDTH, D_MODEL), RET_WIDTH ** -0.5),
        "w_bp": nrm(ks[16], (DEPTH, POOL_WIDTH, D_MODEL), POOL_WIDTH ** -0.5),
        "w_bs": nrm(ks[17], (DEPTH, SG_WIDTH, D_MODEL), SG_WIDTH ** -0.5),
        "w_out": nrm(ks[18], (DEPTH, D_MODEL, D_MODEL), D_MODEL ** -0.5),
        "norm2_g": 1.0 + nrm(ks[19], (DEPTH, D_MODEL), 0.02),
        "w1": nrm(ks[20], (DEPTH, D_MODEL, D_FF), D_MODEL ** -0.5),
        "w2": nrm(ks[21], (DEPTH, D_FF, D_MODEL), D_FF ** -0.5),
        "final_norm_g": 1.0 + nrm(ks[22], (D_MODEL,), 0.02),
    }


def reference(x, c, ctx, c_ctx, w_ada, b_ada, norm1_g, w_in, ret_decay_logit, ret_norm_g, pool_w,
              pool_scale, sg_norm_g, sg_w, sg_b, w_br, w_bp, w_bs, w_out, norm2_g, w1, w2, final_norm_g):
    t = x.shape[1]
    rows_n = t // GRID_W
    tok = jnp.arange(rows_n * GRID_W)
    rope_pos = ((tok // GRID_W).astype(F32), (tok % GRID_W).astype(F32))
    silu_c = jax.nn.silu(c)
    silu_cc = jax.nn.silu(c_ctx)[None]
    xc = ctx
    for l in range(DEPTH):
        last = l == DEPTH - 1
        mod_x = jnp.split((silu_c @ w_ada[l] + b_ada[l])[:, None, :], N_MOD, axis=-1)
        mod_c = jnp.split((silu_cc @ w_ada[l] + b_ada[l])[:, None, :], N_MOD, axis=-1)
        lg = jax.nn.log_sigmoid(ret_decay_logit[l].astype(F32))
        mixer = functools.partial(
            token_mixer, lg_f=lg[0], lg_b=lg[1], ret_norm_g=ret_norm_g[l], pool_w=pool_w[l],
            pool_scale=pool_scale[l], sg_norm_g=sg_norm_g[l], sg_w=sg_w[l], sg_b=sg_b[l],
            w_br=w_br[l], w_bp=w_bp[l], w_bs=w_bs[l], w_out=w_out[l])

        hc = modulate(rms_norm(xc, norm1_g[l]), mod_c[0], mod_c[1])
        if last:
            zk, zv = jnp.split(hc @ w_in[l][:, RET_WIDTH:3 * RET_WIDTH], 2, axis=-1)
            kh = to_heads(zk) * K_SCALE
            vh = to_heads(zv)
            sf = retention_final_state(kh, vh, lg[0])
            sb = retention_final_state(jnp.flip(kh, 2), jnp.flip(vh, 2), lg[1])
        else:
            zero = jnp.zeros((xc.shape[0], RET_HEADS, RET_DK, RET_DV), F32)
            out_c, sf, sb = mixer(hc @ w_in[l], None, CTX_LEN, zero, zero)
            xc = xc + mod_c[2] * out_c
            hc2 = modulate(rms_norm(xc, norm2_g[l]), mod_c[3], mod_c[4])
            xc = xc + mod_c[5] * sq_relu_mlp(hc2, w1[l], w2[l])

        hx = modulate(rms_norm(x, norm1_g[l]), mod_x[0], mod_x[1])
        out_x, _, _ = mixer(hx @ w_in[l], rope_pos, GRID_W, sf, sb)
        x = x + mod_x[2] * out_x
        hx2 = modulate(rms_norm(x, norm2_g[l]), mod_x[3], mod_x[4])
        x = x + mod_x[5] * sq_relu_mlp(hx2, w1[l], w2[l])
    return rms_norm(x, final_norm_g)
```

```python
import functools

import numpy as np
import jax
import jax.numpy as jnp
from jax import lax
from jax.experimental import pallas as pl
from jax.experimental.pallas import tpu as pltpu

F32 = jnp.float32
BF16 = jnp.bfloat16

D_MODEL = 2048
DEPTH = 4
GRID_W = 64
EPS = 1e-6
N_MOD = 6

HEAD_DIM = 128
RET_WIDTH = D_MODEL // 2
RET_HEADS = RET_WIDTH // HEAD_DIM
CHUNK = 128
K_SCALE = HEAD_DIM ** -0.5
ROPE_THETA = 10000.0

POOL_WIDTH = D_MODEL // 4
POOL_WINDOWS = (2, 4, 8, 16)
POOL_GROUPS = len(POOL_WINDOWS)
GROUP_DIM = POOL_WIDTH // POOL_GROUPS

SG_WIDTH = D_MODEL // 4
SG_GROUPS = 4

D_FF = 4 * D_MODEL
N_IN = 4 * RET_WIDTH + POOL_WIDTH + 2 * SG_WIDTH + 3 * D_MODEL

SUBLANES = 8
VMEM_LIMIT_BYTES = 56 * 1024 * 1024

IN_TILE = 512
N_IN_TILES = N_IN // IN_TILE
_QKVG_TILES = 4 * RET_WIDTH // IN_TILE
_SMALL_TILES = (POOL_WIDTH + 2 * SG_WIDTH) // IN_TILE
_GATE_TILES = 3 * D_MODEL // IN_TILE


def _w_in_tile(j):
    return jnp.where(j < _QKVG_TILES, j,
                     jnp.where(j < _QKVG_TILES + _GATE_TILES, j + _SMALL_TILES, j - _GATE_TILES))


def _const_spec(shape):
    nd = len(shape)
    return pl.BlockSpec(shape, lambda *_: (0,) * nd, pipeline_mode=pl.Buffered(1))


def _rms(x):
    return x * lax.rsqrt(jnp.mean(x * x, axis=-1, keepdims=True) + EPS)


ADA_TILE = 1024
ADA_ROWS = 8


def _ada_kernel(cond_ref, w_ref, b_ref, o_ref):
    tn = w_ref.shape[2]

    def body(g, accs):
        r0 = pl.multiple_of(g * SUBLANES, SUBLANES)
        w = w_ref[0, pl.ds(r0, SUBLANES), :]
        out = []
        for r in range(2):
            a = cond_ref[r, pl.ds(r0, SUBLANES), :]
            a = a * jax.nn.sigmoid(a)
            out.append(accs[r] + w * a)
        return tuple(out)

    zero = jnp.zeros((SUBLANES, tn), F32)
    accs = lax.fori_loop(0, D_MODEL // SUBLANES, body, (zero, zero), unroll=8)
    o_ref[0] = jnp.zeros((ADA_ROWS, tn), F32)
    for r in range(2):
        o_ref[0, r:r + 1, :] = jnp.sum(accs[r], axis=0, keepdims=True) + b_ref[0]


def _ada(cond, w_ada, b_ada):
    n = N_MOD * D_MODEL
    return pl.pallas_call(
        _ada_kernel,
        grid=(DEPTH, n // ADA_TILE),
        in_specs=[
            pl.BlockSpec((2, D_MODEL, 1), lambda l, j: (0, 0, 0)),
            pl.BlockSpec((1, D_MODEL, ADA_TILE), lambda l, j: (l, 0, j)),
            pl.BlockSpec((1, 1, ADA_TILE), lambda l, j: (l, 0, j)),
        ],
        out_specs=pl.BlockSpec((1, ADA_ROWS, ADA_TILE), lambda l, j: (l, 0, j)),
        out_shape=jax.ShapeDtypeStruct((DEPTH, ADA_ROWS, n), F32),
        compiler_params=pltpu.CompilerParams(
            dimension_semantics=("arbitrary", "arbitrary"), vmem_limit_bytes=VMEM_LIMIT_BYTES),
        name="ada",
    )(cond, w_ada, b_ada.reshape(DEPTH, 1, n))


NORM_ROWS = 32


def _inproj_kernel(x_ref, g_ref, shift_ref, scale_ref, w_ref, z_ref, h_scr):
    tm = x_ref.shape[0]

    @pl.when(pl.program_id(1) == 0)
    def _():
        gain = g_ref[...]
        mul = 1.0 + scale_ref[...]
        shift = shift_ref[...]

        def body(r, carry):
            r0 = pl.multiple_of(r * NORM_ROWS, NORM_ROWS)
            x = x_ref[pl.ds(r0, NORM_ROWS), :]
            h_scr[pl.ds(r0, NORM_ROWS), :] = (_rms(x) * gain * mul + shift).astype(BF16)
            return carry

        lax.fori_loop(0, tm // NORM_ROWS, body, 0)

    z_ref[...] = jnp.dot(h_scr[...], w_ref[...], preferred_element_type=F32)


def _in_proj(x, gain, shift, scale, w_in, tm):
    t = x.shape[0]
    vec = pl.BlockSpec((1, D_MODEL), lambda i, j: (0, 0))
    return pl.pallas_call(
        _inproj_kernel,
        grid=(t // tm, N_IN_TILES),
        in_specs=[
            pl.BlockSpec((tm, D_MODEL), lambda i, j: (i, 0)),
            vec, vec, vec,
            pl.BlockSpec((D_MODEL, IN_TILE), lambda i, j: (0, _w_in_tile(j))),
        ],
        out_specs=pl.BlockSpec((tm, IN_TILE), lambda i, j: (i, j)),
        out_shape=jax.ShapeDtypeStruct((t, N_IN), F32),
        scratch_shapes=[pltpu.VMEM((tm, D_MODEL), BF16)],
        compiler_params=pltpu.CompilerParams(
            dimension_semantics=("arbitrary", "arbitrary"), vmem_limit_bytes=VMEM_LIMIT_BYTES),
        name="in_proj",
    )(x, gain, shift, scale, w_in)


def _rope(t, cos, sin_lo, sin_hi):
    up = pltpu.roll(t, HEAD_DIM - 32, axis=1)
    down = pltpu.roll(t, 32, axis=1)
    return t * cos + up * sin_lo + down * sin_hi


def _state_kernel(*refs, use_rope):
    if use_rope:
        (kf_ref, vf_ref, kb_ref, vb_ref, cf_ref, slf_ref, shf_ref, cb_ref, slb_ref, shb_ref,
         kdf_ref, kdb_ref, cdf_ref, cdb_ref, sf0_ref, sb0_ref,
         sf_all_ref, sb_all_ref, sf_fin_ref, sb_fin_ref, sf_scr, sb_scr) = refs
    else:
        (kf_ref, vf_ref, kb_ref, vb_ref,
         kdf_ref, kdb_ref, cdf_ref, cdb_ref, sf0_ref, sb0_ref,
         sf_all_ref, sb_all_ref, sf_fin_ref, sb_fin_ref, sf_scr, sb_scr) = refs
    t = pl.program_id(0)

    @pl.when(t == 0)
    def _():
        sf_scr[...] = sf0_ref[...]
        sb_scr[...] = sb0_ref[...]

    sf_all_ref[0] = sf_scr[...].astype(BF16)
    sb_all_ref[0] = sb_scr[...].astype(BF16)

    contract_rows = (((0,), (0,)), ((), ()))
    for h in range(RET_HEADS):
        hs = slice(h * HEAD_DIM, (h + 1) * HEAD_DIM)
        kf = kf_ref[:, hs] * K_SCALE
        kb = kb_ref[:, hs] * K_SCALE
        if use_rope:
            kf = _rope(kf, cf_ref[...], slf_ref[...], shf_ref[...])
            kb = _rope(kb, cb_ref[...], slb_ref[...], shb_ref[...])
        kvf = lax.dot_general((kf * kdf_ref[:, hs]).astype(BF16), vf_ref[:, hs].astype(BF16),
                              contract_rows, preferred_element_type=F32)
        kvb = lax.dot_general((kb * kdb_ref[:, hs]).astype(BF16), vb_ref[:, hs].astype(BF16),
                              contract_rows, preferred_element_type=F32)
        sf_scr[h] = sf_scr[h] * cdf_ref[h] + kvf
        sb_scr[h] = sb_scr[h] * cdb_ref[h] + kvb

    @pl.when(t == pl.num_programs(0) - 1)
    def _():
        sf_fin_ref[...] = sf_scr[...]
        sb_fin_ref[...] = sb_scr[...]


def _states(z, rope, kdf, kdb, cdf, cdb, sf0, sb0):
    t = z.shape[0]
    n = t // CHUNK
    use_rope = rope is not None
    fwd = lambda c: pl.BlockSpec((CHUNK, RET_WIDTH), lambda i: (i, c))
    bwd = lambda c: pl.BlockSpec((CHUNK, RET_WIDTH), lambda i: (n - 1 - i, c))
    in_specs = [fwd(1), fwd(2), bwd(1), bwd(2)]
    args = [z, z, z, z]
    if use_rope:
        in_specs += [pl.BlockSpec((CHUNK, HEAD_DIM), lambda i: (i, 0))] * 3
        in_specs += [pl.BlockSpec((CHUNK, HEAD_DIM), lambda i: (n - 1 - i, 0))] * 3
        args += list(rope) + list(rope)
    state_shape = (RET_HEADS, HEAD_DIM, HEAD_DIM)
    in_specs += [_const_spec((CHUNK, RET_WIDTH))] * 2
    in_specs += [_const_spec((RET_HEADS, 1, HEAD_DIM))] * 2
    in_specs += [_const_spec(state_shape)] * 2
    args += [kdf, kdb, cdf, cdb, sf0, sb0]
    all_shape = jax.ShapeDtypeStruct((n,) + state_shape, BF16)
    fin_shape = jax.ShapeDtypeStruct(state_shape, F32)
    return pl.pallas_call(
        functools.partial(_state_kernel, use_rope=use_rope),
        grid=(n,),
        in_specs=in_specs,
        out_specs=[
            pl.BlockSpec((1,) + state_shape, lambda i: (i, 0, 0, 0)),
            pl.BlockSpec((1,) + state_shape, lambda i: (n - 1 - i, 0, 0, 0)),
            pl.BlockSpec(state_shape, lambda i: (0, 0, 0)),
            pl.BlockSpec(state_shape, lambda i: (0, 0, 0)),
        ],
        out_shape=[all_shape, all_shape, fin_shape, fin_shape],
        scratch_shapes=[pltpu.VMEM(state_shape, F32), pltpu.VMEM(state_shape, F32)],
        compiler_params=pltpu.CompilerParams(
            dimension_semantics=("arbitrary",), vmem_limit_bytes=VMEM_LIMIT_BYTES),
        name="ret_states",
    )(*args)


def _mixer_kernel(*refs, use_rope):
    refs = list(refs)
    (zq_ref, zk_ref, zv_ref, zg_ref, zgr_ref, zgp_ref, zgs_ref, zp_ref, zu_ref, zsv_ref,
     x_ref, sf_ref, sb_ref) = refs[:13]
    refs = refs[13:]
    if use_rope:
        cos_ref, slo_ref, shi_ref = refs[:3]
        refs = refs[3:]
    (dmat_ref, qdf_ref, qdb_ref, retg_ref, pmask_ref, pinv_ref, poolw_ref, pscale_ref,
     sgng_ref, sgw_ref, btab_ref, wbr_ref, wbp_ref, wbs_ref, wout_ref,
     gate_ref, n2g_ref, shift2_ref, scale2_ref,
     x1_ref, h2_ref, ret_scr, pool_scr, sg_scr) = refs
    nck = x_ref.shape[0] // CHUNK
    contract_last = (((1,), (1,)), ((), ()))

    for c in range(nck):
        rows = slice(c * CHUNK, (c + 1) * CHUNK)
        for h in range(RET_HEADS):
            hs = slice(h * HEAD_DIM, (h + 1) * HEAD_DIM)
            q = zq_ref[rows, hs]
            k = zk_ref[rows, hs] * K_SCALE
            if use_rope:
                cos, slo, shi = cos_ref[rows, :], slo_ref[rows, :], shi_ref[rows, :]
                q = _rope(q, cos, slo, shi)
                k = _rope(k, cos, slo, shi)
            qb = q.astype(BF16)
            kb = k.astype(BF16)
            vb = zv_ref[rows, hs].astype(BF16)
            scores = lax.dot_general(qb, kb, contract_last, preferred_element_type=F32)
            scores = scores * dmat_ref[h]
            o = jnp.dot(scores.astype(BF16), vb, preferred_element_type=F32)
            o = o + qdf_ref[:, hs] * jnp.dot(qb, sf_ref[c, h], preferred_element_type=F32)
            o = o + qdb_ref[:, hs] * jnp.dot(qb, sb_ref[c, h], preferred_element_type=F32)
            g = zg_ref[rows, hs]
            ret = _rms(o) * retg_ref[:, hs] * (g * jax.nn.sigmoid(g))
            ret_scr[rows, hs] = ret.astype(BF16)

    for gi in range(POOL_GROUPS):
        gs = slice(gi * GROUP_DIM, (gi + 1) * GROUP_DIM)
        p = zp_ref[:, gs]
        hi = p.astype(BF16)
        lo = (p - hi.astype(F32)).astype(BF16)
        win = (jnp.dot(pmask_ref[gi], hi, preferred_element_type=F32)
               + jnp.dot(pmask_ref[gi], lo, preferred_element_type=F32))
        y = win * pinv_ref[:, gs] - p
        y = jnp.dot(y.astype(BF16), poolw_ref[gi], preferred_element_type=F32)
        pool_scr[:, gs] = (y * pscale_ref[:, gs]).astype(BF16)

    for c in range(nck):
        rows = slice(c * CHUNK, (c + 1) * CHUNK)
        sv = _rms(jax.nn.gelu(zsv_ref[rows, :])) * sgng_ref[...]
        for gi in range(SG_GROUPS):
            gs = slice(gi * GROUP_DIM, (gi + 1) * GROUP_DIM)
            mixed = jnp.dot(sgw_ref[gi], sv[:, gs].astype(BF16), preferred_element_type=F32)
            mixed = mixed + btab_ref[:, gs]
            sg_scr[rows, gs] = (jax.nn.gelu(zu_ref[rows, gs]) * mixed).astype(BF16)

    y = jax.nn.sigmoid(zgr_ref[...]) * jnp.dot(ret_scr[...], wbr_ref[...],
                                                preferred_element_type=F32)
    y = y + jax.nn.sigmoid(zgp_ref[...]) * jnp.dot(pool_scr[...], wbp_ref[...],
                                                    preferred_element_type=F32)
    y = y + jax.nn.sigmoid(zgs_ref[...]) * jnp.dot(sg_scr[...], wbs_ref[...],
                                                    preferred_element_type=F32)
    out = jnp.dot(y.astype(BF16), wout_ref[...], preferred_element_type=F32)
    x1 = x_ref[...] + gate_ref[...] * out
    x1_ref[...] = x1
    h2 = _rms(x1) * n2g_ref[...] * (1.0 + scale2_ref[...]) + shift2_ref[...]
    h2_ref[...] = h2.astype(BF16)


def _mixer(z, x, sf_all, sb_all, rope, tabs, lw, mods, tb):
    t = x.shape[0]
    use_rope = rope is not None
    nck = tb // CHUNK

    def zspec(width, idx):
        return pl.BlockSpec((tb, width), lambda i: (i, idx))

    gate0 = 4 * RET_WIDTH // D_MODEL
    small0 = (4 * RET_WIDTH + 3 * D_MODEL) // POOL_WIDTH
    state_spec = pl.BlockSpec((nck, RET_HEADS, HEAD_DIM, HEAD_DIM), lambda i: (i, 0, 0, 0))
    in_specs = [zspec(RET_WIDTH, 0), zspec(RET_WIDTH, 1), zspec(RET_WIDTH, 2), zspec(RET_WIDTH, 3),
                zspec(D_MODEL, gate0), zspec(D_MODEL, gate0 + 1), zspec(D_MODEL, gate0 + 2),
                zspec(POOL_WIDTH, small0), zspec(SG_WIDTH, small0 + 1), zspec(SG_WIDTH, small0 + 2),
                pl.BlockSpec((tb, D_MODEL), lambda i: (i, 0)), state_spec, state_spec]
    args = [z] * 10 + [x, sf_all, sb_all]
    if use_rope:
        in_specs += [pl.BlockSpec((tb, HEAD_DIM), lambda i: (i, 0))] * 3
        args += list(rope)
    consts = [tabs["dmat"], tabs["qdf"], tabs["qdb"], lw["ret_norm_g"], tabs["pmask"], tabs["pinv"],
              lw["pool_w"], lw["pool_scale"], lw["sg_norm_g"], lw["sg_w"], tabs["btab"],
              lw["w_br"], lw["w_bp"], lw["w_bs"], lw["w_out"],
              mods[2], lw["norm2_g"], mods[3], mods[4]]
    in_specs += [_const_spec(a.shape) for a in consts]
    args += consts
    return pl.pallas_call(
        functools.partial(_mixer_kernel, use_rope=use_rope),
        grid=(t // tb,),
        in_specs=in_specs,
        out_specs=[pl.BlockSpec((tb, D_MODEL), lambda i: (i, 0)),
                   pl.BlockSpec((tb, D_MODEL), lambda i: (i, 0))],
        out_shape=[jax.ShapeDtypeStruct((t, D_MODEL), F32),
                   jax.ShapeDtypeStruct((t, D_MODEL), BF16)],
        scratch_shapes=[pltpu.VMEM((tb, RET_WIDTH), BF16),
                        pltpu.VMEM((tb, POOL_WIDTH), BF16),
                        pltpu.VMEM((tb, SG_WIDTH), BF16)],
        compiler_params=pltpu.CompilerParams(
            dimension_semantics=("arbitrary",), vmem_limit_bytes=VMEM_LIMIT_BYTES),
        name="mixer",
    )(*args)


def _mlp_kernel(*refs, final):
    if final:
        h2_ref, x1_ref, w1_ref, w2_ref, gate_ref, fng_ref, o_ref = refs
    else:
        h2_ref, x1_ref, w1_ref, w2_ref, gate_ref, o_ref = refs
    f = pl.program_id(1)
    a = jnp.maximum(jnp.dot(h2_ref[...], w1_ref[...], preferred_element_type=F32), 0.0)
    part = jnp.dot((a * a).astype(BF16), w2_ref[...], preferred_element_type=F32)

    @pl.when(f == 0)
    def _():
        o_ref[...] = part

    @pl.when(f > 0)
    def _():
        o_ref[...] += part

    @pl.when(f == pl.num_programs(1) - 1)
    def _():
        x2 = x1_ref[...] + gate_ref[...] * o_ref[...]
        if final:
            x2 = _rms(x2) * fng_ref[...]
        o_ref[...] = x2


def _mlp(h2, x1, w1, w2, gate, final_gain, tm, tf):
    t = x1.shape[0]
    final = final_gain is not None
    vec = pl.BlockSpec((1, D_MODEL), lambda i, f: (0, 0))
    in_specs = [pl.BlockSpec((tm, D_MODEL), lambda i, f: (i, 0)),
                pl.BlockSpec((tm, D_MODEL), lambda i, f: (i, 0)),
                pl.BlockSpec((D_MODEL, tf), lambda i, f: (0, f)),
                pl.BlockSpec((tf, D_MODEL), lambda i, f: (f, 0)),
                vec]
    args = [h2, x1, w1, w2, gate]
    if final:
        in_specs.append(vec)
        args.append(final_gain)
    return pl.pallas_call(
        functools.partial(_mlp_kernel, final=final),
        grid=(t // tm, D_FF // tf),
        in_specs=in_specs,
        out_specs=pl.BlockSpec((tm, D_MODEL), lambda i, f: (i, 0)),
        out_shape=jax.ShapeDtypeStruct((t, D_MODEL), F32),
        compiler_params=pltpu.CompilerParams(
            dimension_semantics=("arbitrary", "arbitrary"), vmem_limit_bytes=VMEM_LIMIT_BYTES),
        name="mlp",
    )(*args)


def _rope_tables(t):
    half = HEAD_DIM // 2
    nf = half // 2
    tok = np.arange(t)
    inv = ROPE_THETA ** (-np.arange(nf, dtype=np.float64) / nf)
    lane = np.arange(HEAD_DIM)
    pos = np.where(lane[None, :] < half, (tok // GRID_W)[:, None], (tok % GRID_W)[:, None])
    ang = pos * inv[lane % nf][None, :]
    lower = (lane % half) < nf
    sin = np.sin(ang)
    cos = np.cos(ang).astype(np.float32)
    sin_lo = np.where(lower[None, :], -sin, 0.0).astype(np.float32)
    sin_hi = np.where(lower[None, :], 0.0, sin).astype(np.float32)
    return jnp.asarray(cos), jnp.asarray(sin_lo), jnp.asarray(sin_hi)


def _pool_tables(tb, seg_len):
    pos = np.arange(tb) % seg_len
    base = np.arange(tb) - pos
    col = np.arange(tb)[None, :]
    masks, invs = [], []
    for w in POOL_WINDOWS:
        lo = np.maximum(pos - w // 2, 0)
        hi = np.minimum(pos + w // 2 - 1, seg_len - 1)
        masks.append((col >= (base + lo)[:, None]) & (col <= (base + hi)[:, None]))
        invs.append(np.repeat((1.0 / (hi - lo + 1))[:, None], GROUP_DIM, axis=1))
    pmask = jnp.asarray(np.stack(masks).astype(np.float32), dtype=BF16)
    pinv = jnp.asarray(np.concatenate(invs, axis=1).astype(np.float32))
    return pmask, pinv


def _decay_tables(logit):
    lg = jax.nn.log_sigmoid(logit.astype(F32))
    lgf, lgb = lg[0], lg[1]
    idx = jnp.arange(CHUNK, dtype=F32)
    dist = idx[:, None] - idx[None, :]
    fwd = jnp.exp(lgf[:, None, None] * jnp.maximum(dist, 0.0))
    bwd = jnp.exp(lgb[:, None, None] * jnp.maximum(-dist, 0.0))
    dmat = jnp.where(dist > 0, fwd, jnp.where(dist < 0, bwd, 2.0))

    def lanes(tab):
        return jnp.repeat(tab.T, HEAD_DIM, axis=1)

    return dict(
        dmat=dmat,
        qdf=lanes(jnp.exp(lgf[:, None] * (idx + 1.0)[None])),
        qdb=lanes(jnp.exp(lgb[:, None] * (CHUNK - idx)[None])),
        kdf=lanes(jnp.exp(lgf[:, None] * (CHUNK - 1.0 - idx)[None])),
        kdb=lanes(jnp.exp(lgb[:, None] * idx[None])),
        cdf=jnp.broadcast_to(jnp.exp(lgf * CHUNK)[:, None, None], (RET_HEADS, 1, HEAD_DIM)),
        cdb=jnp.broadcast_to(jnp.exp(lgb * CHUNK)[:, None, None], (RET_HEADS, 1, HEAD_DIM)),
    )


def kernel(x, c, ctx, c_ctx, w_ada, b_ada, norm1_g, w_in, ret_decay_logit, ret_norm_g, pool_w,
           pool_scale, sg_norm_g, sg_w, sg_b, w_br, w_bp, w_bs, w_out, norm2_g, w1, w2, final_norm_g):
    assert x.shape[0] == 1 and ctx.shape[0] == 1
    t = x.shape[1]
    tc = ctx.shape[1]
    xs = x[0]
    xc = ctx[0]

    cond = jnp.stack([c[0], c_ctx])[:, :, None]
    mods_all = _ada(cond, w_ada, b_ada)

    rope = _rope_tables(t)
    lat_tb = CHUNK
    pool_lat = _pool_tables(lat_tb, GRID_W)
    pool_ctx = _pool_tables(tc, tc)
    zero_state = jnp.zeros((RET_HEADS, HEAD_DIM, HEAD_DIM), F32)
    row = lambda v: v.reshape(1, -1)

    for l in range(DEPTH):
        last = l == DEPTH - 1
        mods_x = [mods_all[l, 0:1, k * D_MODEL:(k + 1) * D_MODEL] for k in range(N_MOD)]
        mods_c = [mods_all[l, 1:2, k * D_MODEL:(k + 1) * D_MODEL] for k in range(N_MOD)]
        tabs = _decay_tables(ret_decay_logit[l])
        tabs["btab"] = jnp.repeat(sg_b[l].T, GROUP_DIM, axis=1)
        lw = dict(
            ret_norm_g=row(ret_norm_g[l]), pool_w=pool_w[l].astype(BF16), pool_scale=row(pool_scale[l]),
            sg_norm_g=row(sg_norm_g[l]), sg_w=sg_w[l].astype(BF16),
            w_br=w_br[l].astype(BF16), w_bp=w_bp[l].astype(BF16), w_bs=w_bs[l].astype(BF16),
            w_out=w_out[l].astype(BF16), norm2_g=row(norm2_g[l]))
        w_in_l = w_in[l].astype(BF16)
        w1_l = w1[l].astype(BF16)
        w2_l = w2[l].astype(BF16)
        n1g = row(norm1_g[l])

        zc = _in_proj(xc, n1g, mods_c[0], mods_c[1], w_in_l, tc)
        sfc, sbc, sf, sb = _states(zc, None, tabs["kdf"], tabs["kdb"], tabs["cdf"], tabs["cdb"],
                                   zero_state, zero_state)
        if not last:
            tabs_c = dict(tabs, pmask=pool_ctx[0], pinv=pool_ctx[1])
            xc1, hc2 = _mixer(zc, xc, sfc, sbc, None, tabs_c, lw, mods_c, tc)
            xc = _mlp(hc2, xc1, w1_l, w2_l, mods_c[5], None, tc, 1024)

        zx = _in_proj(xs, n1g, mods_x[0], mods_x[1], w_in_l, 1024)
        sfx, sbx, _, _ = _states(zx, rope, tabs["kdf"], tabs["kdb"], tabs["cdf"], tabs["cdb"], sf, sb)
        tabs_x = dict(tabs, pmask=pool_lat[0], pinv=pool_lat[1])
        x1, h2 = _mixer(zx, xs, sfx, sbx, rope, tabs_x, lw, mods_x, lat_tb)
        xs = _mlp(h2, x1, w1_l, w2_l, mods_x[5], row(final_norm_g) if last else None, 512, 1024)

    return xs[None]
```

```python
import functools

import numpy as np
import jax
import jax.numpy as jnp
from jax import lax
from jax.experimental import pallas as pl
from jax.experimental.pallas import tpu as pltpu

F32 = jnp.float32
BF16 = jnp.bfloat16

D_MODEL = 2048
DEPTH = 4
GRID_W = 64
EPS = 1e-6
N_MOD = 6

HEAD_DIM = 128
RET_WIDTH = D_MODEL // 2
RET_HEADS = RET_WIDTH // HEAD_DIM
CHUNK = 128
K_SCALE = HEAD_DIM ** -0.5
ROPE_THETA = 10000.0

POOL_WIDTH = D_MODEL // 4
POOL_WINDOWS = (2, 4, 8, 16)
POOL_GROUPS = len(POOL_WINDOWS)
GROUP_DIM = POOL_WIDTH // POOL_GROUPS

SG_WIDTH = D_MODEL // 4
SG_GROUPS = 4

D_FF = 4 * D_MODEL
N_IN = 4 * RET_WIDTH + POOL_WIDTH + 2 * SG_WIDTH + 3 * D_MODEL

SUBLANES = 8
VMEM_LIMIT_BYTES = 56 * 1024 * 1024

IN_TILE = 512
N_IN_TILES = N_IN // IN_TILE
_QK_TILES = 2 * RET_WIDTH // IN_TILE
_QKVG_TILES = 4 * RET_WIDTH // IN_TILE
_SMALL_TILES = (POOL_WIDTH + 2 * SG_WIDTH) // IN_TILE
_GATE_TILES = 3 * D_MODEL // IN_TILE
_HEADS_PER_TILE = IN_TILE // HEAD_DIM


def _w_in_tile(j):
    return jnp.where(j < _QKVG_TILES, j,
                     jnp.where(j < _QKVG_TILES + _GATE_TILES, j + _SMALL_TILES, j - _GATE_TILES))


def _layer_spec(arr, l, grid_rank):
    nd = arr.ndim - 1
    return pl.BlockSpec((None,) + arr.shape[1:], lambda *_: (l,) + (0,) * nd,
                        pipeline_mode=pl.Buffered(1))


def _const_spec(arr):
    nd = arr.ndim
    return pl.BlockSpec(arr.shape, lambda *_: (0,) * nd, pipeline_mode=pl.Buffered(1))


def _mod_spec(l, k):
    return pl.BlockSpec((None, ADA_ROWS, D_MODEL), lambda *_: (l, 0, k))


def _rms(x):
    return x * lax.rsqrt(jnp.mean(x * x, axis=-1, keepdims=True) + EPS)


ADA_TILE = 1024
ADA_ROWS = 8
ROW_LATENT, ROW_CONTEXT = 0, 1


def _ada_kernel(cond_ref, w_ref, b_ref, o_ref):
    tn = w_ref.shape[1]

    def body(g, accs):
        r0 = pl.multiple_of(g * SUBLANES, SUBLANES)
        w = w_ref[pl.ds(r0, SUBLANES), :]
        out = []
        for r in range(2):
            a = cond_ref[r, pl.ds(r0, SUBLANES), :]
            a = a * jax.nn.sigmoid(a)
            out.append(accs[r] + w * a)
        return tuple(out)

    zero = jnp.zeros((SUBLANES, tn), F32)
    accs = lax.fori_loop(0, D_MODEL // SUBLANES, body, (zero, zero), unroll=8)
    o_ref[...] = jnp.zeros((ADA_ROWS, tn), F32)
    for r in range(2):
        o_ref[r:r + 1, :] = jnp.sum(accs[r], axis=0, keepdims=True) + b_ref[...]


def _ada(cond, w_ada, b_ada):
    n = N_MOD * D_MODEL
    return pl.pallas_call(
        _ada_kernel,
        grid=(DEPTH, n // ADA_TILE),
        in_specs=[
            pl.BlockSpec((2, D_MODEL, 1), lambda l, j: (0, 0, 0)),
            pl.BlockSpec((None, D_MODEL, ADA_TILE), lambda l, j: (l, 0, j)),
            pl.BlockSpec((None, 1, ADA_TILE), lambda l, j: (l, 0, j)),
        ],
        out_specs=pl.BlockSpec((None, ADA_ROWS, ADA_TILE), lambda l, j: (l, 0, j)),
        out_shape=jax.ShapeDtypeStruct((DEPTH, ADA_ROWS, n), F32),
        compiler_params=pltpu.CompilerParams(
            dimension_semantics=("arbitrary", "arbitrary"), vmem_limit_bytes=VMEM_LIMIT_BYTES),
        name="ada",
    )(cond, w_ada, b_ada.reshape(DEPTH, 1, n))


def _rope(t, cos, sin_lo, sin_hi):
    up = pltpu.roll(t, HEAD_DIM - 32, axis=1)
    down = pltpu.roll(t, 32, axis=1)
    return t * cos + up * sin_lo + down * sin_hi


NORM_ROWS = 32


def _inproj_kernel(*refs, mod_row, use_rope):
    if use_rope:
        x_ref, g_ref, shift_ref, scale_ref, w_ref, cos_ref, slo_ref, shi_ref, z_ref, h_scr = refs
    else:
        x_ref, g_ref, shift_ref, scale_ref, w_ref, z_ref, h_scr = refs
    tm = x_ref.shape[0]
    j = pl.program_id(1)

    @pl.when(j == 0)
    def _():
        gain = g_ref[...]
        mul = 1.0 + scale_ref[mod_row:mod_row + 1, :]
        shift = shift_ref[mod_row:mod_row + 1, :]

        def body(r, carry):
            r0 = pl.multiple_of(r * NORM_ROWS, NORM_ROWS)
            x = x_ref[pl.ds(r0, NORM_ROWS), :]
            h_scr[pl.ds(r0, NORM_ROWS), :] = (_rms(x) * gain * mul + shift).astype(BF16)
            return carry

        lax.fori_loop(0, tm // NORM_ROWS, body, 0)

    is_k = jnp.logical_and(j >= _QK_TILES // 2, j < _QK_TILES)
    col_scale = jnp.where(is_k, K_SCALE, 1.0).astype(F32)

    def project():
        return jnp.dot(h_scr[...], w_ref[...], preferred_element_type=F32) * col_scale

    if use_rope:
        @pl.when(j < _QK_TILES)
        def _():
            acc = project()
            cos, slo, shi = cos_ref[...], slo_ref[...], shi_ref[...]
            for hh in range(_HEADS_PER_TILE):
                hs = slice(hh * HEAD_DIM, (hh + 1) * HEAD_DIM)
                z_ref[:, hs] = _rope(acc[:, hs], cos, slo, shi).astype(BF16)

        @pl.when(j >= _QK_TILES)
        def _():
            z_ref[...] = project().astype(BF16)
    else:
        z_ref[...] = project().astype(BF16)


def _in_proj(x, l, norm1_g, mods_all, mod_row, w_in, rope, tm):
    t = x.shape[0]
    use_rope = rope is not None
    in_specs = [
        pl.BlockSpec((tm, D_MODEL), lambda i, j: (i, 0)),
        pl.BlockSpec((None, 1, D_MODEL), lambda i, j: (l, 0, 0)),
        _mod_spec(l, 0), _mod_spec(l, 1),
        pl.BlockSpec((None, D_MODEL, IN_TILE), lambda i, j: (l, 0, _w_in_tile(j))),
    ]
    args = [x, norm1_g, mods_all, mods_all, w_in]
    if use_rope:
        in_specs += [pl.BlockSpec((tm, HEAD_DIM), lambda i, j: (i, 0))] * 3
        args += list(rope)
    return pl.pallas_call(
        functools.partial(_inproj_kernel, mod_row=mod_row, use_rope=use_rope),
        grid=(t // tm, N_IN_TILES),
        in_specs=in_specs,
        out_specs=pl.BlockSpec((tm, IN_TILE), lambda i, j: (i, j)),
        out_shape=jax.ShapeDtypeStruct((t, N_IN), BF16),
        scratch_shapes=[pltpu.VMEM((tm, D_MODEL), BF16)],
        compiler_params=pltpu.CompilerParams(
            dimension_semantics=("arbitrary", "arbitrary"), vmem_limit_bytes=VMEM_LIMIT_BYTES),
        name="in_proj",
    )(*args)


def _state_kernel(kf_ref, vf_ref, kb_ref, vb_ref, kdf_ref, kdb_ref, cdf_ref, cdb_ref, sf0_ref, sb0_ref,
                  sf_all_ref, sb_all_ref, sf_fin_ref, sb_fin_ref, sf_scr, sb_scr):
    t = pl.program_id(0)
    cps = kf_ref.shape[0] // CHUNK

    @pl.when(t == 0)
    def _():
        sf_scr[...] = sf0_ref[...]
        sb_scr[...] = sb0_ref[...]

    contract_rows = (((0,), (0,)), ((), ()))

    def advance(c, k_ref, v_ref, kd_ref, cd_ref, all_ref, scr):
        rows = slice(c * CHUNK, (c + 1) * CHUNK)
        all_ref[c] = scr[...].astype(BF16)
        for h in range(RET_HEADS):
            hs = slice(h * HEAD_DIM, (h + 1) * HEAD_DIM)
            kd = (k_ref[rows, hs].astype(F32) * kd_ref[:, hs]).astype(BF16)
            kv = lax.dot_general(kd, v_ref[rows, hs], contract_rows, preferred_element_type=F32)
            scr[h] = scr[h] * cd_ref[h] + kv

    for c in range(cps):
        advance(c, kf_ref, vf_ref, kdf_ref, cdf_ref, sf_all_ref, sf_scr)
        advance(cps - 1 - c, kb_ref, vb_ref, kdb_ref, cdb_ref, sb_all_ref, sb_scr)

    @pl.when(t == pl.num_programs(0) - 1)
    def _():
        sf_fin_ref[...] = sf_scr[...]
        sb_fin_ref[...] = sb_scr[...]


def _states(z, l, tabs, sf0, sb0, cps):
    t = z.shape[0]
    nb = t // (cps * CHUNK)
    fwd = lambda c: pl.BlockSpec((cps * CHUNK, RET_WIDTH), lambda i: (i, c))
    bwd = lambda c: pl.BlockSpec((cps * CHUNK, RET_WIDTH), lambda i: (nb - 1 - i, c))
    state_shape = (RET_HEADS, HEAD_DIM, HEAD_DIM)
    consts = [tabs["kdf"], tabs["kdb"], tabs["cdf"], tabs["cdb"]]
    in_specs = [fwd(1), fwd(2), bwd(1), bwd(2)] + [_layer_spec(a, l, 1) for a in consts]
    in_specs += [_const_spec(sf0), _const_spec(sb0)]
    all_shape = jax.ShapeDtypeStruct((t // CHUNK,) + state_shape, BF16)
    fin_shape = jax.ShapeDtypeStruct(state_shape, F32)
    return pl.pallas_call(
        _state_kernel,
        grid=(nb,),
        in_specs=in_specs,
        out_specs=[
            pl.BlockSpec((cps,) + state_shape, lambda i: (i, 0, 0, 0)),
            pl.BlockSpec((cps,) + state_shape, lambda i: (nb - 1 - i, 0, 0, 0)),
            pl.BlockSpec(state_shape, lambda i: (0, 0, 0)),
            pl.BlockSpec(state_shape, lambda i: (0, 0, 0)),
        ],
        out_shape=[all_shape, all_shape, fin_shape, fin_shape],
        scratch_shapes=[pltpu.VMEM(state_shape, F32), pltpu.VMEM(state_shape, F32)],
        compiler_params=pltpu.CompilerParams(
            dimension_semantics=("arbitrary",), vmem_limit_bytes=VMEM_LIMIT_BYTES),
        name="ret_states",
    )(z, z, z, z, *consts, sf0, sb0)


def _mixer_kernel(zq_ref, zk_ref, zv_ref, zg_ref, zgr_ref, zgp_ref, zgs_ref, zp_ref, zu_ref, zsv_ref,
                  x_ref, sf_ref, sb_ref,
                  dmat_ref, qdf_ref, qdb_ref, retg_ref, pmask_ref, pinv_ref, poolw_ref, pscale_ref,
                  sgng_ref, sgw_ref, btab_ref, wbr_ref, wbp_ref, wbs_ref, wout_ref,
                  gate_ref, n2g_ref, shift2_ref, scale2_ref,
                  x1_ref, h2_ref, ret_scr, pool_scr, sg_scr, *, mod_row):
    nck = x_ref.shape[0] // CHUNK
    contract_last = (((1,), (1,)), ((), ()))
    mr = slice(mod_row, mod_row + 1)

    for c in range(nck):
        rows = slice(c * CHUNK, (c + 1) * CHUNK)
        for h in range(RET_HEADS):
            hs = slice(h * HEAD_DIM, (h + 1) * HEAD_DIM)
            qb = zq_ref[rows, hs]
            scores = lax.dot_general(qb, zk_ref[rows, hs], contract_last, preferred_element_type=F32)
            scores = scores * dmat_ref[h]
            o = jnp.dot(scores.astype(BF16), zv_ref[rows, hs], preferred_element_type=F32)
            o = o + qdf_ref[:, hs] * jnp.dot(qb, sf_ref[c, h], preferred_element_type=F32)
            o = o + qdb_ref[:, hs] * jnp.dot(qb, sb_ref[c, h], preferred_element_type=F32)
            g = zg_ref[rows, hs].astype(F32)
            ret = _rms(o) * retg_ref[:, hs] * (g * jax.nn.sigmoid(g))
            ret_scr[rows, hs] = ret.astype(BF16)

    for gi in range(POOL_GROUPS):
        gs = slice(gi * GROUP_DIM, (gi + 1) * GROUP_DIM)
        pb = zp_ref[:, gs]
        win = jnp.dot(pmask_ref[gi], pb, preferred_element_type=F32)
        y = win * pinv_ref[:, gs] - pb.astype(F32)
        y = jnp.dot(y.astype(BF16), poolw_ref[gi], preferred_element_type=F32)
        pool_scr[:, gs] = (y * pscale_ref[:, gs]).astype(BF16)

    for c in range(nck):
        rows = slice(c * CHUNK, (c + 1) * CHUNK)
        sv = _rms(jax.nn.gelu(zsv_ref[rows, :].astype(F32))) * sgng_ref[...]
        for gi in range(SG_GROUPS):
            gs = slice(gi * GROUP_DIM, (gi + 1) * GROUP_DIM)
            mixed = jnp.dot(sgw_ref[gi], sv[:, gs].astype(BF16), preferred_element_type=F32)
            mixed = mixed + btab_ref[:, gs]
            u = jax.nn.gelu(zu_ref[rows, gs].astype(F32))
            sg_scr[rows, gs] = (u * mixed).astype(BF16)

    y = jax.nn.sigmoid(zgr_ref[...].astype(F32)) * jnp.dot(ret_scr[...], wbr_ref[...],
                                                            preferred_element_type=F32)
    y = y + jax.nn.sigmoid(zgp_ref[...].astype(F32)) * jnp.dot(pool_scr[...], wbp_ref[...],
                                                                preferred_element_type=F32)
    y = y + jax.nn.sigmoid(zgs_ref[...].astype(F32)) * jnp.dot(sg_scr[...], wbs_ref[...],
                                                                preferred_element_type=F32)
    out = jnp.dot(y.astype(BF16), wout_ref[...], preferred_element_type=F32)
    x1 = x_ref[...] + gate_ref[mr, :] * out
    x1_ref[...] = x1
    h2 = _rms(x1) * n2g_ref[...] * (1.0 + scale2_ref[mr, :]) + shift2_ref[mr, :]
    h2_ref[...] = h2.astype(BF16)


def _mixer(z, x, sf_all, sb_all, l, tabs, pool_tabs, p, mods_all, mod_row, tb):
    t = x.shape[0]
    nck = tb // CHUNK

    def zspec(width, idx):
        return pl.BlockSpec((tb, width), lambda i: (i, idx))

    gate0 = 4 * RET_WIDTH // D_MODEL
    small0 = (4 * RET_WIDTH + 3 * D_MODEL) // POOL_WIDTH
    state_spec = pl.BlockSpec((nck, RET_HEADS, HEAD_DIM, HEAD_DIM), lambda i: (i, 0, 0, 0))
    in_specs = [zspec(RET_WIDTH, 0), zspec(RET_WIDTH, 1), zspec(RET_WIDTH, 2), zspec(RET_WIDTH, 3),
                zspec(D_MODEL, gate0), zspec(D_MODEL, gate0 + 1), zspec(D_MODEL, gate0 + 2),
                zspec(POOL_WIDTH, small0), zspec(SG_WIDTH, small0 + 1), zspec(SG_WIDTH, small0 + 2),
                pl.BlockSpec((tb, D_MODEL), lambda i: (i, 0)), state_spec, state_spec]
    args = [z] * 10 + [x, sf_all, sb_all]
    pmask, pinv = pool_tabs
    layer_consts = [tabs["dmat"], tabs["qdf"], tabs["qdb"], p["ret_norm_g"]]
    in_specs += [_layer_spec(a, l, 1) for a in layer_consts] + [_const_spec(pmask), _const_spec(pinv)]
    args += layer_consts + [pmask, pinv]
    layer_consts = [p["pool_w"], p["pool_scale"], p["sg_norm_g"], p["sg_w"], tabs["btab"],
                    p["w_br"], p["w_bp"], p["w_bs"], p["w_out"]]
    in_specs += [_layer_spec(a, l, 1) for a in layer_consts]
    args += layer_consts
    in_specs += [_mod_spec(l, 2), _layer_spec(p["norm2_g"], l, 1), _mod_spec(l, 3), _mod_spec(l, 4)]
    args += [mods_all, p["norm2_g"], mods_all, mods_all]
    return pl.pallas_call(
        functools.partial(_mixer_kernel, mod_row=mod_row),
        grid=(t // tb,),
        in_specs=in_specs,
        out_specs=[pl.BlockSpec((tb, D_MODEL), lambda i: (i, 0)),
                   pl.BlockSpec((tb, D_MODEL), lambda i: (i, 0))],
        out_shape=[jax.ShapeDtypeStruct((t, D_MODEL), F32),
                   jax.ShapeDtypeStruct((t, D_MODEL), BF16)],
        scratch_shapes=[pltpu.VMEM((tb, RET_WIDTH), BF16),
                        pltpu.VMEM((tb, POOL_WIDTH), BF16),
                        pltpu.VMEM((tb, SG_WIDTH), BF16)],
        compiler_params=pltpu.CompilerParams(
            dimension_semantics=("arbitrary",), vmem_limit_bytes=VMEM_LIMIT_BYTES),
        name="mixer",
    )(*args)


def _mlp_kernel(*refs, mod_row, final):
    if final:
        h2_ref, x1_ref, w1_ref, w2_ref, gate_ref, fng_ref, o_ref = refs
    else:
        h2_ref, x1_ref, w1_ref, w2_ref, gate_ref, o_ref = refs
    f = pl.program_id(1)
    a = jnp.maximum(jnp.dot(h2_ref[...], w1_ref[...], preferred_element_type=F32), 0.0)
    part = jnp.dot((a * a).astype(BF16), w2_ref[...], preferred_element_type=F32)

    @pl.when(f == 0)
    def _():
        o_ref[...] = part

    @pl.when(f > 0)
    def _():
        o_ref[...] += part

    @pl.when(f == pl.num_programs(1) - 1)
    def _():
        x2 = x1_ref[...] + gate_ref[mod_row:mod_row + 1, :] * o_ref[...]
        if final:
            x2 = _rms(x2) * fng_ref[...]
        o_ref[...] = x2


def _mlp(h2, x1, l, w1, w2, mods_all, mod_row, final_gain, tm, tf):
    t = x1.shape[0]
    final = final_gain is not None
    in_specs = [pl.BlockSpec((tm, D_MODEL), lambda i, f: (i, 0)),
                pl.BlockSpec((tm, D_MODEL), lambda i, f: (i, 0)),
                pl.BlockSpec((None, D_MODEL, tf), lambda i, f: (l, 0, f)),
                pl.BlockSpec((None, tf, D_MODEL), lambda i, f: (l, f, 0)),
                _mod_spec(l, 5)]
    args = [h2, x1, w1, w2, mods_all]
    if final:
        in_specs.append(pl.BlockSpec((1, D_MODEL), lambda i, f: (0, 0)))
        args.append(final_gain)
    return pl.pallas_call(
        functools.partial(_mlp_kernel, mod_row=mod_row, final=final),
        grid=(t // tm, D_FF // tf),
        in_specs=in_specs,
        out_specs=pl.BlockSpec((tm, D_MODEL), lambda i, f: (i, 0)),
        out_shape=jax.ShapeDtypeStruct((t, D_MODEL), F32),
        compiler_params=pltpu.CompilerParams(
            dimension_semantics=("arbitrary", "arbitrary"), vmem_limit_bytes=VMEM_LIMIT_BYTES),
        name="mlp",
    )(*args)


def _rope_tables(t):
    half = HEAD_DIM // 2
    nf = half // 2
    tok = np.arange(t)
    inv = ROPE_THETA ** (-np.arange(nf, dtype=np.float64) / nf)
    lane = np.arange(HEAD_DIM)
    pos = np.where(lane[None, :] < half, (tok // GRID_W)[:, None], (tok % GRID_W)[:, None])
    ang = pos * inv[lane % nf][None, :]
    lower = (lane % half) < nf
    sin = np.sin(ang)
    cos = np.cos(ang).astype(np.float32)
    sin_lo = np.where(lower[None, :], -sin, 0.0).astype(np.float32)
    sin_hi = np.where(lower[None, :], 0.0, sin).astype(np.float32)
    return jnp.asarray(cos), jnp.asarray(sin_lo), jnp.asarray(sin_hi)


def _pool_tables(tb, seg_len):
    pos = np.arange(tb) % seg_len
    base = np.arange(tb) - pos
    col = np.arange(tb)[None, :]
    masks, invs = [], []
    for w in POOL_WINDOWS:
        lo = np.maximum(pos - w // 2, 0)
        hi = np.minimum(pos + w // 2 - 1, seg_len - 1)
        masks.append((col >= (base + lo)[:, None]) & (col <= (base + hi)[:, None]))
        invs.append(np.repeat((1.0 / (hi - lo + 1))[:, None], GROUP_DIM, axis=1))
    pmask = jnp.asarray(np.stack(masks).astype(np.float32), dtype=BF16)
    pinv = jnp.asarray(np.concatenate(invs, axis=1).astype(np.float32))
    return pmask, pinv


def _decay_tables(logit):
    lg = jax.nn.log_sigmoid(logit.astype(F32))
    lgf, lgb = lg[0], lg[1]
    idx = jnp.arange(CHUNK, dtype=F32)
    dist = idx[:, None] - idx[None, :]
    fwd = jnp.exp(lgf[:, None, None] * jnp.maximum(dist, 0.0))
    bwd = jnp.exp(lgb[:, None, None] * jnp.maximum(-dist, 0.0))
    dmat = jnp.where(dist > 0, fwd, jnp.where(dist < 0, bwd, 2.0))

    def lanes(tab):
        return jnp.repeat(tab.T, HEAD_DIM, axis=1)

    return dict(
        dmat=dmat,
        qdf=lanes(jnp.exp(lgf[:, None] * (idx + 1.0)[None])),
        qdb=lanes(jnp.exp(lgb[:, None] * (CHUNK - idx)[None])),
        kdf=lanes(jnp.exp(lgf[:, None] * (CHUNK - 1.0 - idx)[None])),
        kdb=lanes(jnp.exp(lgb[:, None] * idx[None])),
        cdf=jnp.broadcast_to(jnp.exp(lgf * CHUNK)[:, None, None], (RET_HEADS, 1, HEAD_DIM)),
        cdb=jnp.broadcast_to(jnp.exp(lgb * CHUNK)[:, None, None], (RET_HEADS, 1, HEAD_DIM)),
    )


LATENT_ROWS_IN = 1024
LATENT_ROWS_MIX = 256
LATENT_ROWS_MLP = 512
LATENT_STATE_CHUNKS = 4
FF_TILE = 1024


def kernel(x, c, ctx, c_ctx, w_ada, b_ada, norm1_g, w_in, ret_decay_logit, ret_norm_g, pool_w,
           pool_scale, sg_norm_g, sg_w, sg_b, w_br, w_bp, w_bs, w_out, norm2_g, w1, w2, final_norm_g):
    assert x.shape[0] == 1 and ctx.shape[0] == 1
    t = x.shape[1]
    tc = ctx.shape[1]
    xs = x[0]
    xc = ctx[0]

    cond = jnp.stack([c[0], c_ctx])[:, :, None]
    mods_all = _ada(cond, w_ada, b_ada)

    rope = _rope_tables(t)
    pool_lat = _pool_tables(LATENT_ROWS_MIX, GRID_W)
    pool_ctx = _pool_tables(tc, tc)
    zero_state = jnp.zeros((RET_HEADS, HEAD_DIM, HEAD_DIM), F32)

    tabs = jax.vmap(_decay_tables)(ret_decay_logit)
    tabs["btab"] = jnp.repeat(jnp.swapaxes(sg_b, 1, 2), GROUP_DIM, axis=2)
    vec = lambda v: v.reshape(DEPTH, 1, -1)
    p = dict(ret_norm_g=vec(ret_norm_g), pool_w=pool_w.astype(BF16), pool_scale=vec(pool_scale),
             sg_norm_g=vec(sg_norm_g), sg_w=sg_w.astype(BF16),
             w_br=w_br.astype(BF16), w_bp=w_bp.astype(BF16), w_bs=w_bs.astype(BF16),
             w_out=w_out.astype(BF16), norm2_g=vec(norm2_g))
    w_in_b = w_in.astype(BF16)
    w1_b = w1.astype(BF16)
    w2_b = w2.astype(BF16)
    n1g = vec(norm1_g)
    fng = final_norm_g.reshape(1, -1)

    for l in range(DEPTH):
        last = l == DEPTH - 1

        zc = _in_proj(xc, l, n1g, mods_all, ROW_CONTEXT, w_in_b, None, tc)
        sfc, sbc, sf, sb = _states(zc, l, tabs, zero_state, zero_state, tc // CHUNK)
        if not last:
            xc1, hc2 = _mixer(zc, xc, sfc, sbc, l, tabs, pool_ctx, p, mods_all, ROW_CONTEXT, tc)
            xc = _mlp(hc2, xc1, l, w1_b, w2_b, mods_all, ROW_CONTEXT, None, tc, FF_TILE)

        zx = _in_proj(xs, l, n1g, mods_all, ROW_LATENT, w_in_b, rope, LATENT_ROWS_IN)
        sfx, sbx, _, _ = _states(zx, l, tabs, sf, sb, LATENT_STATE_CHUNKS)
        x1, h2 = _mixer(zx, xs, sfx, sbx, l, tabs, pool_lat, p, mods_all, ROW_LATENT, LATENT_ROWS_MIX)
        xs = _mlp(h2, x1, l, w1_b, w2_b, mods_all, ROW_LATENT, fng if last else None,
                  LATENT_ROWS_MLP, FF_TILE)

    return xs[None]
```

```python
import functools

import numpy as np
import jax
import jax.numpy as jnp
from jax import lax
from jax.experimental import pallas as pl
from jax.experimental.pallas import tpu as pltpu

F32 = jnp.float32
BF16 = jnp.bfloat16

D_MODEL = 2048
DEPTH = 4
GRID_W = 64
EPS = 1e-6
N_MOD = 6

HEAD_DIM = 128
RET_WIDTH = D_MODEL // 2
RET_HEADS = RET_WIDTH // HEAD_DIM
CHUNK = 128
K_SCALE = HEAD_DIM ** -0.5
ROPE_THETA = 10000.0

POOL_WIDTH = D_MODEL // 4
POOL_WINDOWS = (2, 4, 8, 16)
POOL_GROUPS = len(POOL_WINDOWS)
GROUP_DIM = POOL_WIDTH // POOL_GROUPS

SG_WIDTH = D_MODEL // 4
SG_GROUPS = 4

D_FF = 4 * D_MODEL
N_IN = 4 * RET_WIDTH + POOL_WIDTH + 2 * SG_WIDTH + 3 * D_MODEL

SUBLANES = 8
VMEM_LIMIT_BYTES = 56 * 1024 * 1024

IN_TILE = N_IN // 4
N_IN_TILES = N_IN // IN_TILE
QK_COLS = 2 * RET_WIDTH
SUB_TILE = 512
_HEADS_PER_SUB = SUB_TILE // HEAD_DIM
_GATE_COL0 = 4 * RET_WIDTH + POOL_WIDTH + 2 * SG_WIDTH

ADA_TILE = 1024
ADA_ROWS = 8
ROW_LATENT, ROW_CONTEXT = 0, 1


def _layer_spec(arr, l):
    nd = arr.ndim - 1
    return pl.BlockSpec((None,) + arr.shape[1:], lambda *_: (l,) + (0,) * nd,
                        pipeline_mode=pl.Buffered(1))


def _const_spec(arr):
    nd = arr.ndim
    return pl.BlockSpec(arr.shape, lambda *_: (0,) * nd, pipeline_mode=pl.Buffered(1))


def _mod_spec(l, k):
    return pl.BlockSpec((None, ADA_ROWS, D_MODEL), lambda *_: (l, 0, k))


def _gain_spec(l):
    return pl.BlockSpec((None, 1, D_MODEL), lambda *_: (l, 0, 0))


def _rms(x):
    return x * lax.rsqrt(jnp.mean(x * x, axis=-1, keepdims=True) + EPS)


def _norm_modulate(x, gain, shift, scale):
    return _rms(x) * gain * (1.0 + scale) + shift


def _ada_kernel(cond_ref, w_ref, b_ref, o_ref):
    tn = w_ref.shape[1]

    def body(g, accs):
        r0 = pl.multiple_of(g * SUBLANES, SUBLANES)
        w = w_ref[pl.ds(r0, SUBLANES), :]
        out = []
        for r in range(2):
            a = cond_ref[r, pl.ds(r0, SUBLANES), :]
            a = a * jax.nn.sigmoid(a)
            out.append(accs[r] + w * a)
        return tuple(out)

    zero = jnp.zeros((SUBLANES, tn), F32)
    accs = lax.fori_loop(0, D_MODEL // SUBLANES, body, (zero, zero), unroll=8)
    o_ref[...] = jnp.zeros((ADA_ROWS, tn), F32)
    for r in range(2):
        o_ref[r:r + 1, :] = jnp.sum(accs[r], axis=0, keepdims=True) + b_ref[...]


def _ada(cond, w_ada, b_ada):
    n = N_MOD * D_MODEL
    return pl.pallas_call(
        _ada_kernel,
        grid=(DEPTH, n // ADA_TILE),
        in_specs=[
            pl.BlockSpec((2, D_MODEL, 1), lambda l, j: (0, 0, 0)),
            pl.BlockSpec((None, D_MODEL, ADA_TILE), lambda l, j: (l, 0, j)),
            pl.BlockSpec((None, 1, ADA_TILE), lambda l, j: (l, 0, j)),
        ],
        out_specs=pl.BlockSpec((None, ADA_ROWS, ADA_TILE), lambda l, j: (l, 0, j)),
        out_shape=jax.ShapeDtypeStruct((DEPTH, ADA_ROWS, n), F32),
        compiler_params=pltpu.CompilerParams(
            dimension_semantics=("arbitrary", "arbitrary"), vmem_limit_bytes=VMEM_LIMIT_BYTES),
        name="ada",
    )(cond, w_ada, b_ada.reshape(DEPTH, 1, n))


def _norm_kernel(x_ref, g_ref, shift_ref, scale_ref, h_ref, *, mod_row):
    mr = slice(mod_row, mod_row + 1)
    h_ref[...] = _norm_modulate(x_ref[...], g_ref[...], shift_ref[mr, :], scale_ref[mr, :]).astype(BF16)


def _norm_mod(x, l, norm1_g, mods_all, mod_row, tm):
    t = x.shape[0]
    return pl.pallas_call(
        functools.partial(_norm_kernel, mod_row=mod_row),
        grid=(t // tm,),
        in_specs=[pl.BlockSpec((tm, D_MODEL), lambda i: (i, 0)),
                  _gain_spec(l), _mod_spec(l, 0), _mod_spec(l, 1)],
        out_specs=pl.BlockSpec((tm, D_MODEL), lambda i: (i, 0)),
        out_shape=jax.ShapeDtypeStruct((t, D_MODEL), BF16),
        compiler_params=pltpu.CompilerParams(
            dimension_semantics=("arbitrary",), vmem_limit_bytes=VMEM_LIMIT_BYTES),
        name="norm_mod",
    )(x, norm1_g, mods_all, mods_all)


def _rope(t, cos, sin_lo, sin_hi):
    up = pltpu.roll(t, HEAD_DIM - 32, axis=1)
    down = pltpu.roll(t, 32, axis=1)
    return t * cos + up * sin_lo + down * sin_hi


def _inproj_kernel(*refs, use_rope):
    if use_rope:
        h_ref, w_ref, cos_ref, slo_ref, shi_ref, z_ref = refs
    else:
        h_ref, w_ref, z_ref = refs
    j = pl.program_id(1)

    @pl.when(j == 0)
    def _():
        for blk in range(QK_COLS // SUB_TILE):
            c0 = blk * SUB_TILE
            acc = jnp.dot(h_ref[...], w_ref[:, c0:c0 + SUB_TILE], preferred_element_type=F32)
            if c0 >= RET_WIDTH:
                acc = acc * K_SCALE
            if use_rope:
                cos, slo, shi = cos_ref[...], slo_ref[...], shi_ref[...]
                for hh in range(_HEADS_PER_SUB):
                    hs = slice(hh * HEAD_DIM, (hh + 1) * HEAD_DIM)
                    z_ref[:, c0 + hh * HEAD_DIM:c0 + (hh + 1) * HEAD_DIM] = (
                        _rope(acc[:, hs], cos, slo, shi).astype(BF16))
            else:
                z_ref[:, c0:c0 + SUB_TILE] = acc.astype(BF16)
        z_ref[:, QK_COLS:] = jnp.dot(h_ref[...], w_ref[:, QK_COLS:],
                                     preferred_element_type=F32).astype(BF16)

    @pl.when(j > 0)
    def _():
        z_ref[...] = jnp.dot(h_ref[...], w_ref[...], preferred_element_type=F32).astype(BF16)


def _in_proj(h, l, w_in, rope, tm):
    t = h.shape[0]
    use_rope = rope is not None
    in_specs = [
        pl.BlockSpec((tm, D_MODEL), lambda i, j: (i, 0)),
        pl.BlockSpec((None, D_MODEL, IN_TILE), lambda i, j: (l, 0, j)),
    ]
    args = [h, w_in]
    if use_rope:
        in_specs += [pl.BlockSpec((tm, HEAD_DIM), lambda i, j: (i, 0))] * 3
        args += list(rope)
    return pl.pallas_call(
        functools.partial(_inproj_kernel, use_rope=use_rope),
        grid=(t // tm, N_IN_TILES),
        in_specs=in_specs,
        out_specs=pl.BlockSpec((tm, IN_TILE), lambda i, j: (i, j)),
        out_shape=jax.ShapeDtypeStruct((t, N_IN), BF16),
        compiler_params=pltpu.CompilerParams(
            dimension_semantics=("arbitrary", "arbitrary"), vmem_limit_bytes=VMEM_LIMIT_BYTES),
        name="in_proj",
    )(*args)


def _state_kernel(kf_ref, vf_ref, kb_ref, vb_ref, kdf_ref, kdb_ref, cdf_ref, cdb_ref, sf0_ref, sb0_ref,
                  sf_all_ref, sb_all_ref, sf_fin_ref, sb_fin_ref, sf_scr, sb_scr):
    t = pl.program_id(0)
    cps = kf_ref.shape[0] // CHUNK

    @pl.when(t == 0)
    def _():
        sf_scr[...] = sf0_ref[...]
        sb_scr[...] = sb0_ref[...]

    contract_rows = (((0,), (0,)), ((), ()))

    def advance(c, k_ref, v_ref, kd_ref, cd_ref, all_ref, scr):
        rows = slice(c * CHUNK, (c + 1) * CHUNK)
        all_ref[c] = scr[...].astype(BF16)
        for h in range(RET_HEADS):
            hs = slice(h * HEAD_DIM, (h + 1) * HEAD_DIM)
            kd = (k_ref[rows, hs].astype(F32) * kd_ref[:, hs]).astype(BF16)
            kv = lax.dot_general(kd, v_ref[rows, hs], contract_rows, preferred_element_type=F32)
            scr[h] = scr[h] * cd_ref[h] + kv

    for c in range(cps):
        advance(c, kf_ref, vf_ref, kdf_ref, cdf_ref, sf_all_ref, sf_scr)
        advance(cps - 1 - c, kb_ref, vb_ref, kdb_ref, cdb_ref, sb_all_ref, sb_scr)

    @pl.when(t == pl.num_programs(0) - 1)
    def _():
        sf_fin_ref[...] = sf_scr[...]
        sb_fin_ref[...] = sb_scr[...]


def _states(z, l, tabs, sf0, sb0, cps):
    t = z.shape[0]
    nb = t // (cps * CHUNK)
    fwd = lambda c: pl.BlockSpec((cps * CHUNK, RET_WIDTH), lambda i: (i, c))
    bwd = lambda c: pl.BlockSpec((cps * CHUNK, RET_WIDTH), lambda i: (nb - 1 - i, c))
    state_shape = (RET_HEADS, HEAD_DIM, HEAD_DIM)
    consts = [tabs["kdf"], tabs["kdb"], tabs["cdf"], tabs["cdb"]]
    in_specs = [fwd(1), fwd(2), bwd(1), bwd(2)] + [_layer_spec(a, l) for a in consts]
    in_specs += [_const_spec(sf0), _const_spec(sb0)]
    all_shape = jax.ShapeDtypeStruct((t // CHUNK,) + state_shape, BF16)
    fin_shape = jax.ShapeDtypeStruct(state_shape, F32)
    return pl.pallas_call(
        _state_kernel,
        grid=(nb,),
        in_specs=in_specs,
        out_specs=[
            pl.BlockSpec((cps,) + state_shape, lambda i: (i, 0, 0, 0)),
            pl.BlockSpec((cps,) + state_shape, lambda i: (nb - 1 - i, 0, 0, 0)),
            pl.BlockSpec(state_shape, lambda i: (0, 0, 0)),
            pl.BlockSpec(state_shape, lambda i: (0, 0, 0)),
        ],
        out_shape=[all_shape, all_shape, fin_shape, fin_shape],
        scratch_shapes=[pltpu.VMEM(state_shape, F32), pltpu.VMEM(state_shape, F32)],
        compiler_params=pltpu.CompilerParams(
            dimension_semantics=("arbitrary",), vmem_limit_bytes=VMEM_LIMIT_BYTES),
        name="ret_states",
    )(z, z, z, z, *consts, sf0, sb0)


MERGE_TILE = 512
ZERO_ROWS = 16


def _mixer_kernel(zq_ref, zk_ref, zv_ref, zg_ref, zp_ref, zu_ref, zsv_ref, sf_ref, sb_ref,
                  zgr_ref, zgp_ref, zgs_ref, x_ref,
                  dmat_ref, qdf_ref, qdb_ref, retg_ref, pmask_ref, pinv_ref, poolw_ref, pscale_ref,
                  sgng_ref, sgw_ref, btab_ref, wbr_ref, wbp_ref, wbs_ref, wout_ref,
                  gate_ref, n2g_ref, shift2_ref, scale2_ref,
                  x1_ref, h2_ref,
                  y_scr, ret_cur, pool_cur, sg_cur, ret_prev, pool_prev, sg_prev, *, mod_row):
    tb = x_ref.shape[0]
    nck = tb // CHUNK
    contract_last = (((1,), (1,)), ((), ()))
    mr = slice(mod_row, mod_row + 1)

    @pl.when(pl.program_id(0) == 0)
    def _():
        def zero_rows(r, carry):
            rows = pl.ds(pl.multiple_of(r * ZERO_ROWS, ZERO_ROWS), ZERO_ROWS)
            ret_prev[rows, :] = jnp.zeros((ZERO_ROWS, RET_WIDTH), BF16)
            pool_prev[rows, :] = jnp.zeros((ZERO_ROWS, POOL_WIDTH), BF16)
            sg_prev[rows, :] = jnp.zeros((ZERO_ROWS, SG_WIDTH), BF16)
            return carry

        lax.fori_loop(0, tb // ZERO_ROWS, zero_rows, 0)

    def merge_piece(i):
        cs = slice(i * MERGE_TILE, (i + 1) * MERGE_TILE)
        y = jax.nn.sigmoid(zgr_ref[:, cs].astype(F32)) * jnp.dot(
            ret_prev[...], wbr_ref[:, cs], preferred_element_type=F32)
        y = y + jax.nn.sigmoid(zgp_ref[:, cs].astype(F32)) * jnp.dot(
            pool_prev[...], wbp_ref[:, cs], preferred_element_type=F32)
        y = y + jax.nn.sigmoid(zgs_ref[:, cs].astype(F32)) * jnp.dot(
            sg_prev[...], wbs_ref[:, cs], preferred_element_type=F32)
        y_scr[:, cs] = y.astype(BF16)

    def out_piece(i):
        cs = slice(i * MERGE_TILE, (i + 1) * MERGE_TILE)
        out = jnp.dot(y_scr[...], wout_ref[:, cs], preferred_element_type=F32)
        x1_ref[:, cs] = x_ref[:, cs] + gate_ref[mr, cs] * out

    def ret_unit(c, h):
        rows = slice(c * CHUNK, (c + 1) * CHUNK)
        hs = slice(h * HEAD_DIM, (h + 1) * HEAD_DIM)
        qb = zq_ref[rows, hs]
        scores = lax.dot_general(qb, zk_ref[rows, hs], contract_last, preferred_element_type=F32)
        scores = (scores * dmat_ref[h]).astype(BF16)
        cross = qdf_ref[:, hs] * jnp.dot(qb, sf_ref[c, h], preferred_element_type=F32)
        cross = cross + qdb_ref[:, hs] * jnp.dot(qb, sb_ref[c, h], preferred_element_type=F32)

        def second():
            o = jnp.dot(scores, zv_ref[rows, hs], preferred_element_type=F32) + cross
            g = zg_ref[rows, hs].astype(F32)
            ret = _rms(o) * retg_ref[:, hs] * (g * jax.nn.sigmoid(g))
            ret_cur[rows, hs] = ret.astype(BF16)

        return second

    def pool_unit(gi):
        gs = slice(gi * GROUP_DIM, (gi + 1) * GROUP_DIM)
        pb = zp_ref[:, gs]
        win = jnp.dot(pmask_ref[gi], pb, preferred_element_type=F32)
        pooled = (win * pinv_ref[:, gs] - pb.astype(F32)).astype(BF16)

        def second():
            mapped = jnp.dot(pooled, poolw_ref[gi], preferred_element_type=F32)
            pool_cur[:, gs] = (mapped * pscale_ref[:, gs]).astype(BF16)

        return second

    def sg_unit(c):
        rows = slice(c * CHUNK, (c + 1) * CHUNK)
        sv = (_rms(jax.nn.gelu(zsv_ref[rows, :].astype(F32))) * sgng_ref[...]).astype(BF16)

        def second():
            for gi in range(SG_GROUPS):
                gs = slice(gi * GROUP_DIM, (gi + 1) * GROUP_DIM)
                mixed = jnp.dot(sgw_ref[gi], sv[:, gs], preferred_element_type=F32)
                mixed = mixed + btab_ref[:, gs]
                u = jax.nn.gelu(zu_ref[rows, gs].astype(F32))
                sg_cur[rows, gs] = (u * mixed).astype(BF16)

        return second

    mix_units = [functools.partial(ret_unit, c, h) for c in range(nck) for h in range(RET_HEADS)]
    mix_units += [functools.partial(pool_unit, gi) for gi in range(POOL_GROUPS)]
    mix_units += [functools.partial(sg_unit, c) for c in range(nck)]
    n_pieces = D_MODEL // MERGE_TILE
    pieces = ([functools.partial(merge_piece, i) for i in range(n_pieces)]
              + [functools.partial(out_piece, i) for i in range(n_pieces)])
    per_piece = -(-len(mix_units) // len(pieces))
    pending = [unit() for unit in mix_units[:per_piece]]
    for i, piece in enumerate(pieces):
        piece()
        for second in pending:
            second()
        pending = [unit() for unit in mix_units[(i + 1) * per_piece:(i + 2) * per_piece]]
    assert not pending
    h2_ref[...] = _norm_modulate(x1_ref[...], n2g_ref[...], shift2_ref[mr, :],
                                 scale2_ref[mr, :]).astype(BF16)

    ret_prev[...] = ret_cur[...]
    pool_prev[...] = pool_cur[...]
    sg_prev[...] = sg_cur[...]


def _mixer(z, x, sf_all, sb_all, l, tabs, pool_tabs, p, mods_all, mod_row, tb):
    t = x.shape[0]
    nb = t // tb
    nck = tb // CHUNK
    cur = lambda s: jnp.minimum(s, nb - 1)
    prev = lambda s: jnp.maximum(s - 1, 0)

    def zspec(width, idx, blk):
        return pl.BlockSpec((tb, width), lambda s: (blk(s), idx))

    gate0 = 4 * RET_WIDTH // D_MODEL
    small0 = (4 * RET_WIDTH + 3 * D_MODEL) // POOL_WIDTH
    state_spec = pl.BlockSpec((nck, RET_HEADS, HEAD_DIM, HEAD_DIM), lambda s: (cur(s), 0, 0, 0))
    row_spec = pl.BlockSpec((tb, D_MODEL), lambda s: (prev(s), 0))
    in_specs = [zspec(RET_WIDTH, 0, cur), zspec(RET_WIDTH, 1, cur), zspec(RET_WIDTH, 2, cur),
                zspec(RET_WIDTH, 3, cur), zspec(POOL_WIDTH, small0, cur),
                zspec(SG_WIDTH, small0 + 1, cur), zspec(SG_WIDTH, small0 + 2, cur),
                state_spec, state_spec,
                zspec(D_MODEL, gate0, prev), zspec(D_MODEL, gate0 + 1, prev),
                zspec(D_MODEL, gate0 + 2, prev), row_spec]
    args = [z] * 7 + [sf_all, sb_all] + [z] * 3 + [x]
    pmask, pinv = pool_tabs
    layer_consts = [tabs["dmat"], tabs["qdf"], tabs["qdb"], p["ret_norm_g"]]
    in_specs += [_layer_spec(a, l) for a in layer_consts] + [_const_spec(pmask), _const_spec(pinv)]
    args += layer_consts + [pmask, pinv]
    layer_consts = [p["pool_w"], p["pool_scale"], p["sg_norm_g"], p["sg_w"], tabs["btab"],
                    p["w_br"], p["w_bp"], p["w_bs"], p["w_out"]]
    in_specs += [_layer_spec(a, l) for a in layer_consts]
    args += layer_consts
    in_specs += [_mod_spec(l, 2), _gain_spec(l), _mod_spec(l, 3), _mod_spec(l, 4)]
    args += [mods_all, p["norm2_g"], mods_all, mods_all]
    mix_scratch = [pltpu.VMEM((tb, RET_WIDTH), BF16), pltpu.VMEM((tb, POOL_WIDTH), BF16),
                   pltpu.VMEM((tb, SG_WIDTH), BF16)]
    return pl.pallas_call(
        functools.partial(_mixer_kernel, mod_row=mod_row),
        grid=(nb + 1,),
        in_specs=in_specs,
        out_specs=[row_spec, row_spec],
        out_shape=[jax.ShapeDtypeStruct((t, D_MODEL), F32),
                   jax.ShapeDtypeStruct((t, D_MODEL), BF16)],
        scratch_shapes=[pltpu.VMEM((tb, D_MODEL), BF16)] + mix_scratch + mix_scratch,
        compiler_params=pltpu.CompilerParams(
            dimension_semantics=("arbitrary",), vmem_limit_bytes=VMEM_LIMIT_BYTES),
        name="mixer",
    )(*args)


def _mlp_kernel(*refs, mod_row, final):
    if final:
        h2_ref, x1_ref, w1_ref, w2_ref, gate_ref, fng_ref, o_ref = refs
    else:
        (h2_ref, x1_ref, w1_ref, w2_ref, gate_ref, ng_ref, nshift_ref, nscale_ref,
         o_ref, hn_ref) = refs
    f = pl.program_id(1)
    mr = slice(mod_row, mod_row + 1)

    @pl.when(f == 0)
    def _():
        o_ref[...] = jnp.zeros_like(o_ref)

    a = jnp.maximum(jnp.dot(h2_ref[...], w1_ref[...], preferred_element_type=F32), 0.0)
    o_ref[...] += jnp.dot((a * a).astype(BF16), w2_ref[...], preferred_element_type=F32)

    @pl.when(f == pl.num_programs(1) - 1)
    def _():
        x2 = x1_ref[...] + gate_ref[mr, :] * o_ref[...]
        if final:
            o_ref[...] = _rms(x2) * fng_ref[...]
        else:
            o_ref[...] = x2
            hn_ref[...] = _norm_modulate(x2, ng_ref[...], nshift_ref[mr, :],
                                         nscale_ref[mr, :]).astype(BF16)


def _mlp(h2, x1, l, w1, w2, mods_all, mod_row, norm1_g, final_gain, tm, tf):
    t = x1.shape[0]
    final = final_gain is not None
    rows = pl.BlockSpec((tm, D_MODEL), lambda i, f: (i, 0))
    in_specs = [rows, rows,
                pl.BlockSpec((None, D_MODEL, tf), lambda i, f: (l, 0, f)),
                pl.BlockSpec((None, tf, D_MODEL), lambda i, f: (l, f, 0)),
                _mod_spec(l, 5)]
    args = [h2, x1, w1, w2, mods_all]
    x_shape = jax.ShapeDtypeStruct((t, D_MODEL), F32)
    if final:
        in_specs.append(pl.BlockSpec((1, D_MODEL), lambda i, f: (0, 0)))
        args.append(final_gain)
        out_specs, out_shape = rows, x_shape
    else:
        in_specs += [_gain_spec(l + 1), _mod_spec(l + 1, 0), _mod_spec(l + 1, 1)]
        args += [norm1_g, mods_all, mods_all]
        out_specs = [rows, rows]
        out_shape = [x_shape, jax.ShapeDtypeStruct((t, D_MODEL), BF16)]
    return pl.pallas_call(
        functools.partial(_mlp_kernel, mod_row=mod_row, final=final),
        grid=(t // tm, D_FF // tf),
        in_specs=in_specs,
        out_specs=out_specs,
        out_shape=out_shape,
        compiler_params=pltpu.CompilerParams(
            dimension_semantics=("arbitrary", "arbitrary"), vmem_limit_bytes=VMEM_LIMIT_BYTES),
        name="mlp",
    )(*args)


def _rope_tables(t):
    half = HEAD_DIM // 2
    nf = half // 2
    tok = np.arange(t)
    inv = ROPE_THETA ** (-np.arange(nf, dtype=np.float64) / nf)
    lane = np.arange(HEAD_DIM)
    pos = np.where(lane[None, :] < half, (tok // GRID_W)[:, None], (tok % GRID_W)[:, None])
    ang = pos * inv[lane % nf][None, :]
    lower = (lane % half) < nf
    sin = np.sin(ang)
    cos = np.cos(ang).astype(np.float32)
    sin_lo = np.where(lower[None, :], -sin, 0.0).astype(np.float32)
    sin_hi = np.where(lower[None, :], 0.0, sin).astype(np.float32)
    return jnp.asarray(cos), jnp.asarray(sin_lo), jnp.asarray(sin_hi)


def _pool_tables(tb, seg_len):
    pos = np.arange(tb) % seg_len
    base = np.arange(tb) - pos
    col = np.arange(tb)[None, :]
    masks, invs = [], []
    for w in POOL_WINDOWS:
        lo = np.maximum(pos - w // 2, 0)
        hi = np.minimum(pos + w // 2 - 1, seg_len - 1)
        masks.append((col >= (base + lo)[:, None]) & (col <= (base + hi)[:, None]))
        invs.append(np.repeat((1.0 / (hi - lo + 1))[:, None], GROUP_DIM, axis=1))
    pmask = jnp.asarray(np.stack(masks).astype(np.float32), dtype=BF16)
    pinv = jnp.asarray(np.concatenate(invs, axis=1).astype(np.float32))
    return pmask, pinv


def _decay_tables(logit):
    lg = jax.nn.log_sigmoid(logit.astype(F32))
    lgf, lgb = lg[0], lg[1]
    idx = jnp.arange(CHUNK, dtype=F32)
    dist = idx[:, None] - idx[None, :]
    fwd = jnp.exp(lgf[:, None, None] * jnp.maximum(dist, 0.0))
    bwd = jnp.exp(lgb[:, None, None] * jnp.maximum(-dist, 0.0))
    dmat = jnp.where(dist > 0, fwd, jnp.where(dist < 0, bwd, 2.0))

    def lanes(tab):
        return jnp.repeat(tab.T, HEAD_DIM, axis=1)

    return dict(
        dmat=dmat,
        qdf=lanes(jnp.exp(lgf[:, None] * (idx + 1.0)[None])),
        qdb=lanes(jnp.exp(lgb[:, None] * (CHUNK - idx)[None])),
        kdf=lanes(jnp.exp(lgf[:, None] * (CHUNK - 1.0 - idx)[None])),
        kdb=lanes(jnp.exp(lgb[:, None] * idx[None])),
        cdf=jnp.broadcast_to(jnp.exp(lgf * CHUNK)[:, None, None], (RET_HEADS, 1, HEAD_DIM)),
        cdb=jnp.broadcast_to(jnp.exp(lgb * CHUNK)[:, None, None], (RET_HEADS, 1, HEAD_DIM)),
    )


LATENT_ROWS_IN = 1024
LATENT_ROWS_MIX = 256
LATENT_ROWS_MLP = 512
LATENT_STATE_CHUNKS = 4
FF_TILE = 1024


def kernel(x, c, ctx, c_ctx, w_ada, b_ada, norm1_g, w_in, ret_decay_logit, ret_norm_g, pool_w,
           pool_scale, sg_norm_g, sg_w, sg_b, w_br, w_bp, w_bs, w_out, norm2_g, w1, w2, final_norm_g):
    assert x.shape[0] == 1 and ctx.shape[0] == 1
    t = x.shape[1]
    tc = ctx.shape[1]
    xs = x[0]
    xc = ctx[0]

    cond = jnp.stack([c[0], c_ctx])[:, :, None]
    mods_all = _ada(cond, w_ada, b_ada)

    rope = _rope_tables(t)
    pool_lat = _pool_tables(LATENT_ROWS_MIX, GRID_W)
    pool_ctx = _pool_tables(tc, tc)
    zero_state = jnp.zeros((RET_HEADS, HEAD_DIM, HEAD_DIM), F32)

    tabs = jax.vmap(_decay_tables)(ret_decay_logit)
    tabs["btab"] = jnp.repeat(jnp.swapaxes(sg_b, 1, 2), GROUP_DIM, axis=2)
    vec = lambda v: v.reshape(DEPTH, 1, -1)
    p = dict(ret_norm_g=vec(ret_norm_g), pool_w=pool_w.astype(BF16), pool_scale=vec(pool_scale),
             sg_norm_g=vec(sg_norm_g), sg_w=sg_w.astype(BF16),
             w_br=w_br.astype(BF16), w_bp=w_bp.astype(BF16), w_bs=w_bs.astype(BF16),
             w_out=w_out.astype(BF16), norm2_g=vec(norm2_g))
    small0 = 4 * RET_WIDTH
    w_in_b = jnp.concatenate([w_in[:, :, :small0], w_in[:, :, _GATE_COL0:],
                              w_in[:, :, small0:_GATE_COL0]], axis=2).astype(BF16)
    w1_b = w1.astype(BF16)
    w2_b = w2.astype(BF16)
    n1g = vec(norm1_g)
    fng = final_norm_g.reshape(1, -1)

    hc = _norm_mod(xc, 0, n1g, mods_all, ROW_CONTEXT, tc)
    hx = _norm_mod(xs, 0, n1g, mods_all, ROW_LATENT, LATENT_ROWS_MLP)
    for l in range(DEPTH):
        last = l == DEPTH - 1

        zc = _in_proj(hc, l, w_in_b, None, tc)
        sfc, sbc, sf, sb = _states(zc, l, tabs, zero_state, zero_state, tc // CHUNK)
        if not last:
            xc1, hc2 = _mixer(zc, xc, sfc, sbc, l, tabs, pool_ctx, p, mods_all, ROW_CONTEXT, tc)
            xc, hc = _mlp(hc2, xc1, l, w1_b, w2_b, mods_all, ROW_CONTEXT, n1g, None, tc, FF_TILE)

        zx = _in_proj(hx, l, w_in_b, rope, LATENT_ROWS_IN)
        sfx, sbx, _, _ = _states(zx, l, tabs, sf, sb, LATENT_STATE_CHUNKS)
        x1, h2 = _mixer(zx, xs, sfx, sbx, l, tabs, pool_lat, p, mods_all, ROW_LATENT, LATENT_ROWS_MIX)
        if last:
            xs = _mlp(h2, x1, l, w1_b, w2_b, mods_all, ROW_LATENT, n1g, fng, LATENT_ROWS_MLP, FF_TILE)
        else:
            xs, hx = _mlp(h2, x1, l, w1_b, w2_b, mods_all, ROW_LATENT, n1g, None,
                          LATENT_ROWS_MLP, FF_TILE)

    return xs[None]
```

```python
import functools

import numpy as np
import jax
import jax.numpy as jnp
from jax import lax
from jax.experimental import pallas as pl
from jax.experimental.pallas import tpu as pltpu

F32 = jnp.float32
BF16 = jnp.bfloat16

D_MODEL = 2048
DEPTH = 4
GRID_W = 64
EPS = 1e-6
N_MOD = 6

HEAD_DIM = 128
RET_WIDTH = D_MODEL // 2
RET_HEADS = RET_WIDTH // HEAD_DIM
CHUNK = 128
K_SCALE = HEAD_DIM ** -0.5
ROPE_THETA = 10000.0

POOL_WIDTH = D_MODEL // 4
POOL_WINDOWS = (2, 4, 8, 16)
POOL_GROUPS = len(POOL_WINDOWS)
GROUP_DIM = POOL_WIDTH // POOL_GROUPS

SG_WIDTH = D_MODEL // 4
SG_GROUPS = 4

D_FF = 4 * D_MODEL
N_IN = 4 * RET_WIDTH + POOL_WIDTH + 2 * SG_WIDTH + 3 * D_MODEL

SUBLANES = 8
LANES = 128
VMEM_LIMIT_BYTES = 56 * 1024 * 1024

IN_TILE = N_IN // 4
N_IN_TILES = N_IN // IN_TILE
QK_COLS = 2 * RET_WIDTH
SUB_TILE = 512
_HEADS_PER_SUB = SUB_TILE // HEAD_DIM
MERGE_TILE = 512
N_MERGE = D_MODEL // MERGE_TILE
_POOL_BLOCK = 4 * RET_WIDTH // POOL_WIDTH
_GATE_BLOCK = (4 * RET_WIDTH + POOL_WIDTH + 2 * SG_WIDTH) // MERGE_TILE

ADA_TILE = 1024
ADA_ROWS = 8
ROW_LATENT, ROW_CONTEXT = 0, 1


def _layer_spec(arr, l):
    nd = arr.ndim - 1
    return pl.BlockSpec((None,) + arr.shape[1:], lambda *_: (l,) + (0,) * nd,
                        pipeline_mode=pl.Buffered(1))


def _const_spec(arr):
    nd = arr.ndim
    return pl.BlockSpec(arr.shape, lambda *_: (0,) * nd, pipeline_mode=pl.Buffered(1))


def _mod_spec(l, k):
    return pl.BlockSpec((None, ADA_ROWS, D_MODEL), lambda *_: (l, 0, k))


def _gain_spec(l):
    return pl.BlockSpec((None, 1, D_MODEL), lambda *_: (l, 0, 0))


def _rms(x):
    return x * lax.rsqrt(jnp.mean(x * x, axis=-1, keepdims=True) + EPS)


def _norm_modulate(x, gain, shift, scale):
    return _rms(x) * gain * (1.0 + scale) + shift


def _ada_kernel(cond_ref, w_ref, b_ref, o_ref, act_scr):
    tn = w_ref.shape[1]

    @pl.when(jnp.logical_and(pl.program_id(0) == 0, pl.program_id(1) == 0))
    def _():
        for r in range(2):
            a = cond_ref[r]
            act_scr[r] = jnp.broadcast_to(a * jax.nn.sigmoid(a), (D_MODEL, LANES))

    def body(g, accs):
        r0 = pl.multiple_of(g * SUBLANES, SUBLANES)
        w = w_ref[pl.ds(r0, SUBLANES), :]
        out = []
        for r in range(2):
            a = pltpu.repeat(act_scr[r, pl.ds(r0, SUBLANES), :], tn // LANES, axis=1)
            out.append(accs[r] + w * a)
        return tuple(out)

    zero = jnp.zeros((SUBLANES, tn), F32)
    accs = lax.fori_loop(0, D_MODEL // SUBLANES, body, (zero, zero), unroll=8)
    o_ref[...] = jnp.zeros((ADA_ROWS, tn), F32)
    for r in range(2):
        o_ref[r:r + 1, :] = jnp.sum(accs[r], axis=0, keepdims=True) + b_ref[...]


def _ada(cond, w_ada, b_ada):
    n = N_MOD * D_MODEL
    return pl.pallas_call(
        _ada_kernel,
        grid=(DEPTH, n // ADA_TILE),
        in_specs=[
            pl.BlockSpec((2, D_MODEL, 1), lambda l, j: (0, 0, 0)),
            pl.BlockSpec((None, D_MODEL, ADA_TILE), lambda l, j: (l, 0, j)),
            pl.BlockSpec((None, 1, ADA_TILE), lambda l, j: (l, 0, j)),
        ],
        out_specs=pl.BlockSpec((None, ADA_ROWS, ADA_TILE), lambda l, j: (l, 0, j)),
        out_shape=jax.ShapeDtypeStruct((DEPTH, ADA_ROWS, n), F32),
        scratch_shapes=[pltpu.VMEM((2, D_MODEL, LANES), F32)],
        compiler_params=pltpu.CompilerParams(
            dimension_semantics=("arbitrary", "arbitrary"), vmem_limit_bytes=VMEM_LIMIT_BYTES),
        name="ada",
    )(cond, w_ada, b_ada.reshape(DEPTH, 1, n))


def _norm_kernel(x_ref, g_ref, shift_ref, scale_ref, h_ref, *, mod_row):
    mr = slice(mod_row, mod_row + 1)
    h_ref[...] = _norm_modulate(x_ref[...], g_ref[...], shift_ref[mr, :], scale_ref[mr, :]).astype(BF16)


def _norm_mod(x, l, norm1_g, mods_all, mod_row, tm):
    t = x.shape[0]
    return pl.pallas_call(
        functools.partial(_norm_kernel, mod_row=mod_row),
        grid=(t // tm,),
        in_specs=[pl.BlockSpec((tm, D_MODEL), lambda i: (i, 0)),
                  _gain_spec(l), _mod_spec(l, 0), _mod_spec(l, 1)],
        out_specs=pl.BlockSpec((tm, D_MODEL), lambda i: (i, 0)),
        out_shape=jax.ShapeDtypeStruct((t, D_MODEL), BF16),
        compiler_params=pltpu.CompilerParams(
            dimension_semantics=("arbitrary",), vmem_limit_bytes=VMEM_LIMIT_BYTES),
        name="norm_mod",
    )(x, norm1_g, mods_all, mods_all)


def _rope(t, cos, sin_lo, sin_hi):
    up = pltpu.roll(t, HEAD_DIM - 32, axis=1)
    down = pltpu.roll(t, 32, axis=1)
    return t * cos + up * sin_lo + down * sin_hi


def _inproj_kernel(*refs, use_rope):
    if use_rope:
        h_ref, w_ref, cos_ref, slo_ref, shi_ref, z_ref = refs
    else:
        h_ref, w_ref, z_ref = refs
    j = pl.program_id(1)

    @pl.when(j == 0)
    def _():
        for blk in range(QK_COLS // SUB_TILE):
            c0 = blk * SUB_TILE
            acc = jnp.dot(h_ref[...], w_ref[:, c0:c0 + SUB_TILE], preferred_element_type=F32)
            if c0 >= RET_WIDTH:
                acc = acc * K_SCALE
            if use_rope:
                cos, slo, shi = cos_ref[...], slo_ref[...], shi_ref[...]
                for hh in range(_HEADS_PER_SUB):
                    hs = slice(hh * HEAD_DIM, (hh + 1) * HEAD_DIM)
                    z_ref[:, c0 + hh * HEAD_DIM:c0 + (hh + 1) * HEAD_DIM] = (
                        _rope(acc[:, hs], cos, slo, shi).astype(BF16))
            else:
                z_ref[:, c0:c0 + SUB_TILE] = acc.astype(BF16)
        z_ref[:, QK_COLS:] = jnp.dot(h_ref[...], w_ref[:, QK_COLS:],
                                     preferred_element_type=F32).astype(BF16)

    @pl.when(j > 0)
    def _():
        z_ref[...] = jnp.dot(h_ref[...], w_ref[...], preferred_element_type=F32).astype(BF16)


def _in_proj(h, w_in, rope, tm):
    t = h.shape[0]
    use_rope = rope is not None
    in_specs = [
        pl.BlockSpec((tm, D_MODEL), lambda i, j: (i, 0)),
        pl.BlockSpec((D_MODEL, IN_TILE), lambda i, j: (0, j)),
    ]
    args = [h, w_in]
    if use_rope:
        in_specs += [pl.BlockSpec((tm, HEAD_DIM), lambda i, j: (i, 0))] * 3
        args += list(rope)
    return pl.pallas_call(
        functools.partial(_inproj_kernel, use_rope=use_rope),
        grid=(t // tm, N_IN_TILES),
        in_specs=in_specs,
        out_specs=pl.BlockSpec((tm, IN_TILE), lambda i, j: (i, j)),
        out_shape=jax.ShapeDtypeStruct((t, N_IN), BF16),
        compiler_params=pltpu.CompilerParams(
            dimension_semantics=("arbitrary", "arbitrary"), vmem_limit_bytes=VMEM_LIMIT_BYTES),
        name="in_proj",
    )(*args)


def _state_kernel(kf_ref, vf_ref, kb_ref, vb_ref, kdf_ref, kdb_ref, cdf_ref, cdb_ref, sf0_ref, sb0_ref,
                  sf_all_ref, sb_all_ref, sf_fin_ref, sb_fin_ref, sf_scr, sb_scr):
    t = pl.program_id(0)
    cps = kf_ref.shape[0] // CHUNK

    @pl.when(t == 0)
    def _():
        sf_scr[...] = sf0_ref[...]
        sb_scr[...] = sb0_ref[...]

    contract_rows = (((0,), (0,)), ((), ()))

    def advance(c, k_ref, v_ref, kd_ref, cd_ref, all_ref, scr):
        rows = slice(c * CHUNK, (c + 1) * CHUNK)
        all_ref[c] = scr[...].astype(BF16)
        for h in range(RET_HEADS):
            hs = slice(h * HEAD_DIM, (h + 1) * HEAD_DIM)
            kd = (k_ref[rows, hs].astype(F32) * kd_ref[:, hs]).astype(BF16)
            kv = lax.dot_general(kd, v_ref[rows, hs], contract_rows, preferred_element_type=F32)
            scr[h] = scr[h] * cd_ref[h] + kv

    for c in range(cps):
        advance(c, kf_ref, vf_ref, kdf_ref, cdf_ref, sf_all_ref, sf_scr)
        advance(cps - 1 - c, kb_ref, vb_ref, kdb_ref, cdb_ref, sb_all_ref, sb_scr)

    @pl.when(t == pl.num_programs(0) - 1)
    def _():
        sf_fin_ref[...] = sf_scr[...]
        sb_fin_ref[...] = sb_scr[...]


def _states(z, l, tabs, sf0, sb0, cps):
    t = z.shape[0]
    nb = t // (cps * CHUNK)
    fwd = lambda c: pl.BlockSpec((cps * CHUNK, RET_WIDTH), lambda i: (i, c))
    bwd = lambda c: pl.BlockSpec((cps * CHUNK, RET_WIDTH), lambda i: (nb - 1 - i, c))
    state_shape = (RET_HEADS, HEAD_DIM, HEAD_DIM)
    consts = [tabs["kdf"], tabs["kdb"], tabs["cdf"], tabs["cdb"]]
    in_specs = [fwd(1), fwd(2), bwd(1), bwd(2)] + [_layer_spec(a, l) for a in consts]
    in_specs += [_const_spec(sf0), _const_spec(sb0)]
    all_shape = jax.ShapeDtypeStruct((t // CHUNK,) + state_shape, BF16)
    fin_shape = jax.ShapeDtypeStruct(state_shape, F32)
    return pl.pallas_call(
        _state_kernel,
        grid=(nb,),
        in_specs=in_specs,
        out_specs=[
            pl.BlockSpec((cps,) + state_shape, lambda i: (i, 0, 0, 0)),
            pl.BlockSpec((cps,) + state_shape, lambda i: (nb - 1 - i, 0, 0, 0)),
            pl.BlockSpec(state_shape, lambda i: (0, 0, 0)),
            pl.BlockSpec(state_shape, lambda i: (0, 0, 0)),
        ],
        out_shape=[all_shape, all_shape, fin_shape, fin_shape],
        scratch_shapes=[pltpu.VMEM(state_shape, F32), pltpu.VMEM(state_shape, F32)],
        compiler_params=pltpu.CompilerParams(
            dimension_semantics=("arbitrary",), vmem_limit_bytes=VMEM_LIMIT_BYTES),
        name="ret_states",
    )(z, z, z, z, *consts, sf0, sb0)


ZERO_ROWS = 16


def _mixer_kernel(*refs, mod_row):
    zq_ref, zk_ref, zv_ref, zg_ref, zp_ref, zu_ref, zsv_ref, sf_ref, sb_ref = refs[:9]
    zgr_refs, zgp_refs, zgs_refs = (refs[9 + k * N_MERGE:9 + (k + 1) * N_MERGE] for k in range(3))
    (x_ref,
     dmat_ref, qdf_ref, qdb_ref, retg_ref, pmask_ref, pinv_ref, poolw_ref, pscale_ref,
     sgng_ref, sgw_ref, btab_ref, wbr_ref, wbp_ref, wbs_ref, wout_ref,
     gate_ref, n2g_ref, shift2_ref, scale2_ref,
     x1_ref, h2_ref,
     y_scr, ret_cur, pool_cur, sg_cur, ret_prev, pool_prev, sg_prev) = refs[9 + 3 * N_MERGE:]
    tb = x_ref.shape[0]
    nck = tb // CHUNK
    contract_last = (((1,), (1,)), ((), ()))
    mr = slice(mod_row, mod_row + 1)

    @pl.when(pl.program_id(0) == 0)
    def _():
        def zero_rows(r, carry):
            rows = pl.ds(pl.multiple_of(r * ZERO_ROWS, ZERO_ROWS), ZERO_ROWS)
            ret_prev[rows, :] = jnp.zeros((ZERO_ROWS, RET_WIDTH), BF16)
            pool_prev[rows, :] = jnp.zeros((ZERO_ROWS, POOL_WIDTH), BF16)
            sg_prev[rows, :] = jnp.zeros((ZERO_ROWS, SG_WIDTH), BF16)
            return carry

        lax.fori_loop(0, tb // ZERO_ROWS, zero_rows, 0)

    def merge_piece(i):
        cs = slice(i * MERGE_TILE, (i + 1) * MERGE_TILE)
        y = jax.nn.sigmoid(zgr_refs[i][...].astype(F32)) * jnp.dot(
            ret_prev[...], wbr_ref[:, cs], preferred_element_type=F32)
        y = y + jax.nn.sigmoid(zgp_refs[i][...].astype(F32)) * jnp.dot(
            pool_prev[...], wbp_ref[:, cs], preferred_element_type=F32)
        y = y + jax.nn.sigmoid(zgs_refs[i][...].astype(F32)) * jnp.dot(
            sg_prev[...], wbs_ref[:, cs], preferred_element_type=F32)
        y_scr[:, cs] = y.astype(BF16)

    def out_piece(i):
        cs = slice(i * MERGE_TILE, (i + 1) * MERGE_TILE)
        out = jnp.dot(y_scr[...], wout_ref[:, cs], preferred_element_type=F32)
        x1_ref[:, cs] = x_ref[:, cs] + gate_ref[mr, cs] * out

    def ret_unit(c, h):
        rows = slice(c * CHUNK, (c + 1) * CHUNK)
        hs = slice(h * HEAD_DIM, (h + 1) * HEAD_DIM)
        qb = zq_ref[rows, hs]
        scores = lax.dot_general(qb, zk_ref[rows, hs], contract_last, preferred_element_type=F32)
        scores = (scores * dmat_ref[h]).astype(BF16)
        cross = qdf_ref[:, hs] * jnp.dot(qb, sf_ref[c, h], preferred_element_type=F32)
        cross = cross + qdb_ref[:, hs] * jnp.dot(qb, sb_ref[c, h], preferred_element_type=F32)

        def second():
            o = jnp.dot(scores, zv_ref[rows, hs], preferred_element_type=F32) + cross
            g = zg_ref[rows, hs].astype(F32)
            ret = _rms(o) * retg_ref[:, hs] * (g * jax.nn.sigmoid(g))
            ret_cur[rows, hs] = ret.astype(BF16)

        return second

    def pool_unit(gi):
        gs = slice(gi * GROUP_DIM, (gi + 1) * GROUP_DIM)
        pb = zp_ref[:, gs]
        win = jnp.dot(pmask_ref[gi], pb, preferred_element_type=F32)
        pooled = (win * pinv_ref[:, gs] - pb.astype(F32)).astype(BF16)

        def second():
            mapped = jnp.dot(pooled, poolw_ref[gi], preferred_element_type=F32)
            pool_cur[:, gs] = (mapped * pscale_ref[:, gs]).astype(BF16)

        return second

    def sg_unit(c):
        rows = slice(c * CHUNK, (c + 1) * CHUNK)
        sv = (_rms(jax.nn.gelu(zsv_ref[rows, :].astype(F32))) * sgng_ref[...]).astype(BF16)

        def second():
            for gi in range(SG_GROUPS):
                gs = slice(gi * GROUP_DIM, (gi + 1) * GROUP_DIM)
                mixed = jnp.dot(sgw_ref[gi], sv[:, gs], preferred_element_type=F32)
                mixed = mixed + btab_ref[:, gs]
                u = jax.nn.gelu(zu_ref[rows, gs].astype(F32))
                sg_cur[rows, gs] = (u * mixed).astype(BF16)

        return second

    mix_units = [functools.partial(ret_unit, c, h) for c in range(nck) for h in range(RET_HEADS)]
    mix_units += [functools.partial(pool_unit, gi) for gi in range(POOL_GROUPS)]
    mix_units += [functools.partial(sg_unit, c) for c in range(nck)]
    pieces = ([functools.partial(merge_piece, i) for i in range(N_MERGE)]
              + [functools.partial(out_piece, i) for i in range(N_MERGE)])
    per_piece = -(-len(mix_units) // len(pieces))
    pending = [unit() for unit in mix_units[:per_piece]]
    for i, piece in enumerate(pieces):
        piece()
        for second in pending:
            second()
        pending = [unit() for unit in mix_units[(i + 1) * per_piece:(i + 2) * per_piece]]
    assert not pending
    h2_ref[...] = _norm_modulate(x1_ref[...], n2g_ref[...], shift2_ref[mr, :],
                                 scale2_ref[mr, :]).astype(BF16)

    ret_prev[...] = ret_cur[...]
    pool_prev[...] = pool_cur[...]
    sg_prev[...] = sg_cur[...]


def _mixer(z, x, sf_all, sb_all, l, tabs, pool_tabs, p, mods_all, mod_row, tb):
    t = x.shape[0]
    nb = t // tb
    nck = tb // CHUNK
    cur = lambda s: jnp.minimum(s, nb - 1)
    prev = lambda s: jnp.maximum(s - 1, 0)

    def zspec(width, idx, blk):
        return pl.BlockSpec((tb, width), lambda s: (blk(s), idx))

    state_spec = pl.BlockSpec((nck, RET_HEADS, HEAD_DIM, HEAD_DIM), lambda s: (cur(s), 0, 0, 0))
    row_spec = pl.BlockSpec((tb, D_MODEL), lambda s: (prev(s), 0))
    in_specs = [zspec(RET_WIDTH, 0, cur), zspec(RET_WIDTH, 1, cur), zspec(RET_WIDTH, 2, cur),
                zspec(RET_WIDTH, 3, cur), zspec(POOL_WIDTH, _POOL_BLOCK, cur),
                zspec(SG_WIDTH, _POOL_BLOCK + 1, cur), zspec(SG_WIDTH, _POOL_BLOCK + 2, cur),
                state_spec, state_spec]
    in_specs += [zspec(MERGE_TILE, _GATE_BLOCK + k, prev) for k in range(3 * N_MERGE)]
    in_specs.append(row_spec)
    args = [z] * 7 + [sf_all, sb_all] + [z] * (3 * N_MERGE) + [x]
    pmask, pinv = pool_tabs
    layer_consts = [tabs["dmat"], tabs["qdf"], tabs["qdb"], p["ret_norm_g"]]
    in_specs += [_layer_spec(a, l) for a in layer_consts] + [_const_spec(pmask), _const_spec(pinv)]
    args += layer_consts + [pmask, pinv]
    layer_consts = [p["pool_w"], p["pool_scale"], p["sg_norm_g"], p["sg_w"], tabs["btab"],
                    p["w_br"], p["w_bp"], p["w_bs"], p["w_out"]]
    in_specs += [_layer_spec(a, l) for a in layer_consts]
    args += layer_consts
    in_specs += [_mod_spec(l, 2), _gain_spec(l), _mod_spec(l, 3), _mod_spec(l, 4)]
    args += [mods_all, p["norm2_g"], mods_all, mods_all]
    mix_scratch = [pltpu.VMEM((tb, RET_WIDTH), BF16), pltpu.VMEM((tb, POOL_WIDTH), BF16),
                   pltpu.VMEM((tb, SG_WIDTH), BF16)]
    return pl.pallas_call(
        functools.partial(_mixer_kernel, mod_row=mod_row),
        grid=(nb + 1,),
        in_specs=in_specs,
        out_specs=[row_spec, row_spec],
        out_shape=[jax.ShapeDtypeStruct((t, D_MODEL), F32),
                   jax.ShapeDtypeStruct((t, D_MODEL), BF16)],
        scratch_shapes=[pltpu.VMEM((tb, D_MODEL), BF16)] + mix_scratch + mix_scratch,
        compiler_params=pltpu.CompilerParams(
            dimension_semantics=("arbitrary",), vmem_limit_bytes=VMEM_LIMIT_BYTES),
        name="mixer",
    )(*args)


def _mlp_kernel(*refs, mod_row, final, n_cast):
    refs = list(refs)
    if n_cast:
        cast_dst = refs[-n_cast:]
        refs = refs[:-n_cast]
        n_out = 1 if final else 2
        cast_src = refs[-n_out - n_cast:-n_out]
        refs = refs[:-n_out - n_cast] + refs[-n_out:]
        for src, dst in zip(cast_src, cast_dst):
            dst[...] = src[...].astype(BF16)
    if final:
        h2_ref, x1_ref, w1_ref, w2_ref, gate_ref, fng_ref, o_ref = refs
    else:
        (h2_ref, x1_ref, w1_ref, w2_ref, gate_ref, ng_ref, nshift_ref, nscale_ref,
         o_ref, hn_ref) = refs
    f = pl.program_id(1)
    mr = slice(mod_row, mod_row + 1)

    @pl.when(f == 0)
    def _():
        o_ref[...] = jnp.zeros_like(o_ref)

    a = jnp.maximum(jnp.dot(h2_ref[...], w1_ref[...], preferred_element_type=F32), 0.0)
    o_ref[...] += jnp.dot((a * a).astype(BF16), w2_ref[...], preferred_element_type=F32)

    @pl.when(f == pl.num_programs(1) - 1)
    def _():
        x2 = x1_ref[...] + gate_ref[mr, :] * o_ref[...]
        if final:
            o_ref[...] = _rms(x2) * fng_ref[...]
        else:
            o_ref[...] = x2
            hn_ref[...] = _norm_modulate(x2, ng_ref[...], nshift_ref[mr, :],
                                         nscale_ref[mr, :]).astype(BF16)


def _mlp(h2, x1, l, w1, w2, mods_all, mod_row, norm1_g, final_gain, tm, tf, cast_next=()):
    t = x1.shape[0]
    final = final_gain is not None
    n_f = D_FF // tf
    grid = (t // tm, n_f)
    rows = pl.BlockSpec((tm, D_MODEL), lambda i, f: (i, 0))
    in_specs = [rows, rows,
                pl.BlockSpec((D_MODEL, tf), lambda i, f: (0, f)),
                pl.BlockSpec((tf, D_MODEL), lambda i, f: (f, 0)),
                _mod_spec(l, 5)]
    args = [h2, x1, w1, w2, mods_all]
    x_shape = jax.ShapeDtypeStruct((t, D_MODEL), F32)
    if final:
        in_specs.append(pl.BlockSpec((1, D_MODEL), lambda i, f: (0, 0)))
        args.append(final_gain)
        out_specs, out_shape = [rows], [x_shape]
    else:
        in_specs += [_gain_spec(l + 1), _mod_spec(l + 1, 0), _mod_spec(l + 1, 1)]
        args += [norm1_g, mods_all, mods_all]
        out_specs = [rows, rows]
        out_shape = [x_shape, jax.ShapeDtypeStruct((t, D_MODEL), BF16)]
    for w in cast_next:
        slab, cols = w.shape[1] // (grid[0] * grid[1]), w.shape[2]
        assert slab * grid[0] * grid[1] == w.shape[1] and slab % (2 * SUBLANES) == 0
        in_specs.append(pl.BlockSpec((None, slab, cols), lambda i, f: (l + 1, i * n_f + f, 0)))
        args.append(w)
        out_specs.append(pl.BlockSpec((slab, cols), lambda i, f: (i * n_f + f, 0)))
        out_shape.append(jax.ShapeDtypeStruct(w.shape[1:], BF16))
    return pl.pallas_call(
        functools.partial(_mlp_kernel, mod_row=mod_row, final=final, n_cast=len(cast_next)),
        grid=grid,
        in_specs=in_specs,
        out_specs=out_specs,
        out_shape=out_shape,
        compiler_params=pltpu.CompilerParams(
            dimension_semantics=("arbitrary", "arbitrary"), vmem_limit_bytes=VMEM_LIMIT_BYTES),
        name="mlp",
    )(*args)


def _rope_tables(t):
    half = HEAD_DIM // 2
    nf = half // 2
    tok = np.arange(t)
    inv = ROPE_THETA ** (-np.arange(nf, dtype=np.float64) / nf)
    lane = np.arange(HEAD_DIM)
    pos = np.where(lane[None, :] < half, (tok // GRID_W)[:, None], (tok % GRID_W)[:, None])
    ang = pos * inv[lane % nf][None, :]
    lower = (lane % half) < nf
    sin = np.sin(ang)
    cos = np.cos(ang).astype(np.float32)
    sin_lo = np.where(lower[None, :], -sin, 0.0).astype(np.float32)
    sin_hi = np.where(lower[None, :], 0.0, sin).astype(np.float32)
    return jnp.asarray(cos), jnp.asarray(sin_lo), jnp.asarray(sin_hi)


def _pool_tables(tb, seg_len):
    pos = np.arange(tb) % seg_len
    base = np.arange(tb) - pos
    col = np.arange(tb)[None, :]
    masks, invs = [], []
    for w in POOL_WINDOWS:
        lo = np.maximum(pos - w // 2, 0)
        hi = np.minimum(pos + w // 2 - 1, seg_len - 1)
        masks.append((col >= (base + lo)[:, None]) & (col <= (base + hi)[:, None]))
        invs.append(np.repeat((1.0 / (hi - lo + 1))[:, None], GROUP_DIM, axis=1))
    pmask = jnp.asarray(np.stack(masks).astype(np.float32), dtype=BF16)
    pinv = jnp.asarray(np.concatenate(invs, axis=1).astype(np.float32))
    return pmask, pinv


def _decay_tables(logit):
    lg = jax.nn.log_sigmoid(logit.astype(F32))
    lgf, lgb = lg[0], lg[1]
    idx = jnp.arange(CHUNK, dtype=F32)
    dist = idx[:, None] - idx[None, :]
    fwd = jnp.exp(lgf[:, None, None] * jnp.maximum(dist, 0.0))
    bwd = jnp.exp(lgb[:, None, None] * jnp.maximum(-dist, 0.0))
    dmat = jnp.where(dist > 0, fwd, jnp.where(dist < 0, bwd, 2.0))

    def lanes(tab):
        return jnp.repeat(tab.T, HEAD_DIM, axis=1)

    return dict(
        dmat=dmat,
        qdf=lanes(jnp.exp(lgf[:, None] * (idx + 1.0)[None])),
        qdb=lanes(jnp.exp(lgb[:, None] * (CHUNK - idx)[None])),
        kdf=lanes(jnp.exp(lgf[:, None] * (CHUNK - 1.0 - idx)[None])),
        kdb=lanes(jnp.exp(lgb[:, None] * idx[None])),
        cdf=jnp.broadcast_to(jnp.exp(lgf * CHUNK)[:, None, None], (RET_HEADS, 1, HEAD_DIM)),
        cdb=jnp.broadcast_to(jnp.exp(lgb * CHUNK)[:, None, None], (RET_HEADS, 1, HEAD_DIM)),
    )


LATENT_ROWS_IN = 1024
LATENT_ROWS_MIX = 256
LATENT_ROWS_MLP = 512
LATENT_STATE_CHUNKS = 4
FF_TILE = 1024


def kernel(x, c, ctx, c_ctx, w_ada, b_ada, norm1_g, w_in, ret_decay_logit, ret_norm_g, pool_w,
           pool_scale, sg_norm_g, sg_w, sg_b, w_br, w_bp, w_bs, w_out, norm2_g, w1, w2, final_norm_g):
    assert x.shape[0] == 1 and ctx.shape[0] == 1
    t = x.shape[1]
    tc = ctx.shape[1]
    xs = x[0]
    xc = ctx[0]

    cond = jnp.stack([c[0], c_ctx])[:, :, None]
    mods_all = _ada(cond, w_ada, b_ada)

    rope = _rope_tables(t)
    pool_lat = _pool_tables(LATENT_ROWS_MIX, GRID_W)
    pool_ctx = _pool_tables(tc, tc)
    zero_state = jnp.zeros((RET_HEADS, HEAD_DIM, HEAD_DIM), F32)

    tabs = jax.vmap(_decay_tables)(ret_decay_logit)
    tabs["btab"] = jnp.repeat(jnp.swapaxes(sg_b, 1, 2), GROUP_DIM, axis=2)
    vec = lambda v: v.reshape(DEPTH, 1, -1)
    p = dict(ret_norm_g=vec(ret_norm_g), pool_w=pool_w.astype(BF16), pool_scale=vec(pool_scale),
             sg_norm_g=vec(sg_norm_g), sg_w=sg_w.astype(BF16),
             w_br=w_br.astype(BF16), w_bp=w_bp.astype(BF16), w_bs=w_bs.astype(BF16),
             w_out=w_out.astype(BF16), norm2_g=vec(norm2_g))
    n1g = vec(norm1_g)
    fng = final_norm_g.reshape(1, -1)
    streamed = (w_in, w1, w2)
    w_in_b, w1_b, w2_b = (w[0].astype(BF16) for w in streamed)

    hc = _norm_mod(xc, 0, n1g, mods_all, ROW_CONTEXT, tc)
    hx = _norm_mod(xs, 0, n1g, mods_all, ROW_LATENT, LATENT_ROWS_MLP)
    for l in range(DEPTH):
        last = l == DEPTH - 1

        zc = _in_proj(hc, w_in_b, None, tc)
        sfc, sbc, sf, sb = _states(zc, l, tabs, zero_state, zero_state, tc // CHUNK)
        if not last:
            xc1, hc2 = _mixer(zc, xc, sfc, sbc, l, tabs, pool_ctx, p, mods_all, ROW_CONTEXT, tc)
            xc, hc = _mlp(hc2, xc1, l, w1_b, w2_b, mods_all, ROW_CONTEXT, n1g, None, tc, FF_TILE)

        zx = _in_proj(hx, w_in_b, rope, LATENT_ROWS_IN)
        sfx, sbx, _, _ = _states(zx, l, tabs, sf, sb, LATENT_STATE_CHUNKS)
        x1, h2 = _mixer(zx, xs, sfx, sbx, l, tabs, pool_lat, p, mods_all, ROW_LATENT, LATENT_ROWS_MIX)
        if last:
            xs, = _mlp(h2, x1, l, w1_b, w2_b, mods_all, ROW_LATENT, n1g, fng, LATENT_ROWS_MLP, FF_TILE)
        else:
            xs, hx, w_in_b, w1_b, w2_b = _mlp(h2, x1, l, w1_b, w2_b, mods_all, ROW_LATENT, n1g, None,
                                              LATENT_ROWS_MLP, FF_TILE, cast_next=streamed)

    return xs[None]
```

```python
import functools

import numpy as np
import jax
import jax.numpy as jnp
from jax import lax
from jax.experimental import pallas as pl
from jax.experimental.pallas import tpu as pltpu

F32 = jnp.float32
BF16 = jnp.bfloat16

D_MODEL = 2048
DEPTH = 4
GRID_W = 64
EPS = 1e-6
N_MOD = 6

HEAD_DIM = 128
RET_WIDTH = D_MODEL // 2
RET_HEADS = RET_WIDTH // HEAD_DIM
CHUNK = 128
K_SCALE = HEAD_DIM ** -0.5
ROPE_THETA = 10000.0

POOL_WIDTH = D_MODEL // 4
POOL_WINDOWS = (2, 4, 8, 16)
POOL_GROUPS = len(POOL_WINDOWS)
GROUP_DIM = POOL_WIDTH // POOL_GROUPS

SG_WIDTH = D_MODEL // 4
SG_GROUPS = 4

D_FF = 4 * D_MODEL
N_IN = 4 * RET_WIDTH + POOL_WIDTH + 2 * SG_WIDTH + 3 * D_MODEL

SUBLANES = 8
LANES = 128
VMEM_LIMIT_BYTES = 56 * 1024 * 1024

IN_TILE = N_IN // 4
N_IN_TILES = N_IN // IN_TILE
QK_COLS = 2 * RET_WIDTH
SUB_TILE = 512
_HEADS_PER_SUB = SUB_TILE // HEAD_DIM
MERGE_TILE = 512
N_MERGE = D_MODEL // MERGE_TILE
_POOL_BLOCK = 4 * RET_WIDTH // POOL_WIDTH
_GATE_BLOCK = (4 * RET_WIDTH + POOL_WIDTH + 2 * SG_WIDTH) // MERGE_TILE

ADA_TILE = 1024
ADA_ROWS = 8
ROW_LATENT, ROW_CONTEXT = 0, 1


def _layer_spec(arr, l):
    nd = arr.ndim - 1
    return pl.BlockSpec((None,) + arr.shape[1:], lambda *_: (l,) + (0,) * nd,
                        pipeline_mode=pl.Buffered(1))


def _const_spec(arr):
    nd = arr.ndim
    return pl.BlockSpec(arr.shape, lambda *_: (0,) * nd, pipeline_mode=pl.Buffered(1))


def _mod_spec(l, k):
    return pl.BlockSpec((None, ADA_ROWS, D_MODEL), lambda *_: (l, 0, k))


def _gain_spec(l):
    return pl.BlockSpec((None, 1, D_MODEL), lambda *_: (l, 0, 0))


def _rms(x):
    return x * lax.rsqrt(jnp.mean(x * x, axis=-1, keepdims=True) + EPS)


def _norm_modulate(x, gain, shift, scale):
    return _rms(x) * gain * (1.0 + scale) + shift


def _ada_kernel(cond_ref, w_ref, b_ref, o_ref, act_scr):
    tn = w_ref.shape[1]

    @pl.when(jnp.logical_and(pl.program_id(0) == 0, pl.program_id(1) == 0))
    def _():
        for r in range(2):
            a = cond_ref[r]
            act_scr[r] = jnp.broadcast_to(a * jax.nn.sigmoid(a), (D_MODEL, LANES))

    def body(g, accs):
        r0 = pl.multiple_of(g * SUBLANES, SUBLANES)
        w = w_ref[pl.ds(r0, SUBLANES), :]
        out = []
        for r in range(2):
            a = jnp.tile(act_scr[r, pl.ds(r0, SUBLANES), :], (1, tn // LANES))
            out.append(accs[r] + w * a)
        return tuple(out)

    zero = jnp.zeros((SUBLANES, tn), F32)
    accs = lax.fori_loop(0, D_MODEL // SUBLANES, body, (zero, zero), unroll=8)
    o_ref[...] = jnp.zeros((ADA_ROWS, tn), F32)
    for r in range(2):
        o_ref[r:r + 1, :] = jnp.sum(accs[r], axis=0, keepdims=True) + b_ref[...]


def _ada(cond, w_ada, b_ada):
    n = N_MOD * D_MODEL
    return pl.pallas_call(
        _ada_kernel,
        grid=(DEPTH, n // ADA_TILE),
        in_specs=[
            pl.BlockSpec((2, D_MODEL, 1), lambda l, j: (0, 0, 0)),
            pl.BlockSpec((None, D_MODEL, ADA_TILE), lambda l, j: (l, 0, j)),
            pl.BlockSpec((None, 1, ADA_TILE), lambda l, j: (l, 0, j)),
        ],
        out_specs=pl.BlockSpec((None, ADA_ROWS, ADA_TILE), lambda l, j: (l, 0, j)),
        out_shape=jax.ShapeDtypeStruct((DEPTH, ADA_ROWS, n), F32),
        scratch_shapes=[pltpu.VMEM((2, D_MODEL, LANES), F32)],
        compiler_params=pltpu.CompilerParams(
            dimension_semantics=("arbitrary", "arbitrary"), vmem_limit_bytes=VMEM_LIMIT_BYTES),
        name="ada",
    )(cond, w_ada, b_ada.reshape(DEPTH, 1, n))


def _norm_kernel(x_ref, g_ref, shift_ref, scale_ref, h_ref, *, mod_row):
    mr = slice(mod_row, mod_row + 1)
    h_ref[...] = _norm_modulate(x_ref[...], g_ref[...], shift_ref[mr, :], scale_ref[mr, :]).astype(BF16)


def _norm_mod(x, l, norm1_g, mods_all, mod_row, tm):
    t = x.shape[0]
    return pl.pallas_call(
        functools.partial(_norm_kernel, mod_row=mod_row),
        grid=(t // tm,),
        in_specs=[pl.BlockSpec((tm, D_MODEL), lambda i: (i, 0)),
                  _gain_spec(l), _mod_spec(l, 0), _mod_spec(l, 1)],
        out_specs=pl.BlockSpec((tm, D_MODEL), lambda i: (i, 0)),
        out_shape=jax.ShapeDtypeStruct((t, D_MODEL), BF16),
        compiler_params=pltpu.CompilerParams(
            dimension_semantics=("arbitrary",), vmem_limit_bytes=VMEM_LIMIT_BYTES),
        name="norm_mod",
    )(x, norm1_g, mods_all, mods_all)


def _rope(t, cos, sin_lo, sin_hi):
    up = pltpu.roll(t, HEAD_DIM - 32, axis=1)
    down = pltpu.roll(t, 32, axis=1)
    return t * cos + up * sin_lo + down * sin_hi


def _inproj_kernel(*refs, use_rope):
    if use_rope:
        h_ref, w_ref, cos_ref, slo_ref, shi_ref, z_ref = refs
    else:
        h_ref, w_ref, z_ref = refs
    j = pl.program_id(1)

    @pl.when(j == 0)
    def _():
        for blk in range(QK_COLS // SUB_TILE):
            c0 = blk * SUB_TILE
            acc = jnp.dot(h_ref[...], w_ref[:, c0:c0 + SUB_TILE], preferred_element_type=F32)
            if c0 >= RET_WIDTH:
                acc = acc * K_SCALE
            if use_rope:
                cos, slo, shi = cos_ref[...], slo_ref[...], shi_ref[...]
                for hh in range(_HEADS_PER_SUB):
                    hs = slice(hh * HEAD_DIM, (hh + 1) * HEAD_DIM)
                    z_ref[:, c0 + hh * HEAD_DIM:c0 + (hh + 1) * HEAD_DIM] = (
                        _rope(acc[:, hs], cos, slo, shi).astype(BF16))
            else:
                z_ref[:, c0:c0 + SUB_TILE] = acc.astype(BF16)
        z_ref[:, QK_COLS:] = jnp.dot(h_ref[...], w_ref[:, QK_COLS:],
                                     preferred_element_type=F32).astype(BF16)

    @pl.when(j > 0)
    def _():
        z_ref[...] = jnp.dot(h_ref[...], w_ref[...], preferred_element_type=F32).astype(BF16)


def _in_proj(h, w_in, rope, tm):
    t = h.shape[0]
    use_rope = rope is not None
    in_specs = [
        pl.BlockSpec((tm, D_MODEL), lambda i, j: (i, 0)),
        pl.BlockSpec((D_MODEL, IN_TILE), lambda i, j: (0, j)),
    ]
    args = [h, w_in]
    if use_rope:
        in_specs += [pl.BlockSpec((tm, HEAD_DIM), lambda i, j: (i, 0))] * 3
        args += list(rope)
    return pl.pallas_call(
        functools.partial(_inproj_kernel, use_rope=use_rope),
        grid=(t // tm, N_IN_TILES),
        in_specs=in_specs,
        out_specs=pl.BlockSpec((tm, IN_TILE), lambda i, j: (i, j)),
        out_shape=jax.ShapeDtypeStruct((t, N_IN), BF16),
        compiler_params=pltpu.CompilerParams(
            dimension_semantics=("arbitrary", "arbitrary"), vmem_limit_bytes=VMEM_LIMIT_BYTES),
        name="in_proj",
    )(*args)


def _state_kernel(kf_ref, vf_ref, kb_ref, vb_ref, kdf_ref, kdb_ref, cdf_ref, cdb_ref, sf0_ref, sb0_ref,
                  sf_all_ref, sb_all_ref, sf_fin_ref, sb_fin_ref, sf_scr, sb_scr):
    t = pl.program_id(0)
    cps = kf_ref.shape[0] // CHUNK

    @pl.when(t == 0)
    def _():
        sf_scr[...] = sf0_ref[...]
        sb_scr[...] = sb0_ref[...]

    contract_rows = (((0,), (0,)), ((), ()))

    def advance(c, k_ref, v_ref, kd_ref, cd_ref, all_ref, scr):
        rows = slice(c * CHUNK, (c + 1) * CHUNK)
        all_ref[c] = scr[...].astype(BF16)
        for h in range(RET_HEADS):
            hs = slice(h * HEAD_DIM, (h + 1) * HEAD_DIM)
            kd = (k_ref[rows, hs].astype(F32) * kd_ref[:, hs]).astype(BF16)
            kv = lax.dot_general(kd, v_ref[rows, hs], contract_rows, preferred_element_type=F32)
            scr[h] = scr[h] * cd_ref[h] + kv

    for c in range(cps):
        advance(c, kf_ref, vf_ref, kdf_ref, cdf_ref, sf_all_ref, sf_scr)
        advance(cps - 1 - c, kb_ref, vb_ref, kdb_ref, cdb_ref, sb_all_ref, sb_scr)

    @pl.when(t == pl.num_programs(0) - 1)
    def _():
        sf_fin_ref[...] = sf_scr[...]
        sb_fin_ref[...] = sb_scr[...]


def _states(z, l, tabs, sf0, sb0, cps):
    t = z.shape[0]
    nb = t // (cps * CHUNK)
    fwd = lambda c: pl.BlockSpec((cps * CHUNK, RET_WIDTH), lambda i: (i, c))
    bwd = lambda c: pl.BlockSpec((cps * CHUNK, RET_WIDTH), lambda i: (nb - 1 - i, c))
    state_shape = (RET_HEADS, HEAD_DIM, HEAD_DIM)
    consts = [tabs["kdf"], tabs["kdb"], tabs["cdf"], tabs["cdb"]]
    in_specs = [fwd(1), fwd(2), bwd(1), bwd(2)] + [_layer_spec(a, l) for a in consts]
    in_specs += [_const_spec(sf0), _const_spec(sb0)]
    all_shape = jax.ShapeDtypeStruct((t // CHUNK,) + state_shape, BF16)
    fin_shape = jax.ShapeDtypeStruct(state_shape, F32)
    return pl.pallas_call(
        _state_kernel,
        grid=(nb,),
        in_specs=in_specs,
        out_specs=[
            pl.BlockSpec((cps,) + state_shape, lambda i: (i, 0, 0, 0)),
            pl.BlockSpec((cps,) + state_shape, lambda i: (nb - 1 - i, 0, 0, 0)),
            pl.BlockSpec(state_shape, lambda i: (0, 0, 0)),
            pl.BlockSpec(state_shape, lambda i: (0, 0, 0)),
        ],
        out_shape=[all_shape, all_shape, fin_shape, fin_shape],
        scratch_shapes=[pltpu.VMEM(state_shape, F32), pltpu.VMEM(state_shape, F32)],
        compiler_params=pltpu.CompilerParams(
            dimension_semantics=("arbitrary",), vmem_limit_bytes=VMEM_LIMIT_BYTES),
        name="ret_states",
    )(z, z, z, z, *consts, sf0, sb0)


ZERO_ROWS = 16


def _mixer_kernel(*refs, mod_row):
    zq_ref, zk_ref, zv_ref, zg_ref, zp_ref, zu_ref, zsv_ref, sf_ref, sb_ref = refs[:9]
    zgr_refs, zgp_refs, zgs_refs = (refs[9 + k * N_MERGE:9 + (k + 1) * N_MERGE] for k in range(3))
    (x_ref,
     dmat_ref, qdf_ref, qdb_ref, retg_ref, pmask_ref, pinv_ref, poolw_ref, pscale_ref,
     sgng_ref, sgw_ref, btab_ref, wbr_ref, wbp_ref, wbs_ref, wout_ref,
     gate_ref, n2g_ref, shift2_ref, scale2_ref,
     x1_ref, h2_ref,
     y_scr, ret_cur, pool_cur, sg_cur, ret_prev, pool_prev, sg_prev) = refs[9 + 3 * N_MERGE:]
    tb = x_ref.shape[0]
    nck = tb // CHUNK
    contract_last = (((1,), (1,)), ((), ()))
    mr = slice(mod_row, mod_row + 1)

    @pl.when(pl.program_id(0) == 0)
    def _():
        def zero_rows(r, carry):
            rows = pl.ds(pl.multiple_of(r * ZERO_ROWS, ZERO_ROWS), ZERO_ROWS)
            ret_prev[rows, :] = jnp.zeros((ZERO_ROWS, RET_WIDTH), BF16)
            pool_prev[rows, :] = jnp.zeros((ZERO_ROWS, POOL_WIDTH), BF16)
            sg_prev[rows, :] = jnp.zeros((ZERO_ROWS, SG_WIDTH), BF16)
            return carry

        lax.fori_loop(0, tb // ZERO_ROWS, zero_rows, 0)

    def merge_piece(i):
        cs = slice(i * MERGE_TILE, (i + 1) * MERGE_TILE)
        y = jax.nn.sigmoid(zgr_refs[i][...].astype(F32)) * jnp.dot(
            ret_prev[...], wbr_ref[:, cs], preferred_element_type=F32)
        y = y + jax.nn.sigmoid(zgp_refs[i][...].astype(F32)) * jnp.dot(
            pool_prev[...], wbp_ref[:, cs], preferred_element_type=F32)
        y = y + jax.nn.sigmoid(zgs_refs[i][...].astype(F32)) * jnp.dot(
            sg_prev[...], wbs_ref[:, cs], preferred_element_type=F32)
        y_scr[:, cs] = y.astype(BF16)

    def out_piece(i):
        cs = slice(i * MERGE_TILE, (i + 1) * MERGE_TILE)
        out = jnp.dot(y_scr[...], wout_ref[:, cs], preferred_element_type=F32)
        x1_ref[:, cs] = x_ref[:, cs] + gate_ref[mr, cs] * out

    def ret_unit(c, h):
        rows = slice(c * CHUNK, (c + 1) * CHUNK)
        hs = slice(h * HEAD_DIM, (h + 1) * HEAD_DIM)
        qb = zq_ref[rows, hs]
        scores = lax.dot_general(qb, zk_ref[rows, hs], contract_last, preferred_element_type=F32)
        scores = (scores * dmat_ref[h]).astype(BF16)
        cross = qdf_ref[:, hs] * jnp.dot(qb, sf_ref[c, h], preferred_element_type=F32)
        cross = cross + qdb_ref[:, hs] * jnp.dot(qb, sb_ref[c, h], preferred_element_type=F32)

        def second():
            o = jnp.dot(scores, zv_ref[rows, hs], preferred_element_type=F32) + cross
            g = zg_ref[rows, hs].astype(F32)
            ret = _rms(o) * retg_ref[:, hs] * (g * jax.nn.sigmoid(g))
            ret_cur[rows, hs] = ret.astype(BF16)

        return second

    def pool_unit(gi):
        gs = slice(gi * GROUP_DIM, (gi + 1) * GROUP_DIM)
        pb = zp_ref[:, gs]
        win = jnp.dot(pmask_ref[gi], pb, preferred_element_type=F32)
        pooled = (win * pinv_ref[:, gs] - pb.astype(F32)).astype(BF16)

        def second():
            mapped = jnp.dot(pooled, poolw_ref[gi], preferred_element_type=F32)
            pool_cur[:, gs] = (mapped * pscale_ref[:, gs]).astype(BF16)

        return second

    def sg_unit(c):
        rows = slice(c * CHUNK, (c + 1) * CHUNK)
        sv = (_rms(jax.nn.gelu(zsv_ref[rows, :].astype(F32))) * sgng_ref[...]).astype(BF16)

        def second():
            for gi in range(SG_GROUPS):
                gs = slice(gi * GROUP_DIM, (gi + 1) * GROUP_DIM)
                mixed = jnp.dot(sgw_ref[gi], sv[:, gs], preferred_element_type=F32)
                mixed = mixed + btab_ref[:, gs]
                u = jax.nn.gelu(zu_ref[rows, gs].astype(F32))
                sg_cur[rows, gs] = (u * mixed).astype(BF16)

        return second

    mix_units = [functools.partial(ret_unit, c, h) for c in range(nck) for h in range(RET_HEADS)]
    mix_units += [functools.partial(pool_unit, gi) for gi in range(POOL_GROUPS)]
    mix_units += [functools.partial(sg_unit, c) for c in range(nck)]
    pieces = ([functools.partial(merge_piece, i) for i in range(N_MERGE)]
              + [functools.partial(out_piece, i) for i in range(N_MERGE)])
    per_piece = -(-len(mix_units) // len(pieces))
    pending = [unit() for unit in mix_units[:per_piece]]
    for i, piece in enumerate(pieces):
        piece()
        for second in pending:
            second()
        pending = [unit() for unit in mix_units[(i + 1) * per_piece:(i + 2) * per_piece]]
    assert not pending
    h2_ref[...] = _norm_modulate(x1_ref[...], n2g_ref[...], shift2_ref[mr, :],
                                 scale2_ref[mr, :]).astype(BF16)

    ret_prev[...] = ret_cur[...]
    pool_prev[...] = pool_cur[...]
    sg_prev[...] = sg_cur[...]


def _mixer(z, x, sf_all, sb_all, l, tabs, pool_tabs, p, mods_all, mod_row, tb):
    t = x.shape[0]
    nb = t // tb
    nck = tb // CHUNK
    cur = lambda s: jnp.minimum(s, nb - 1)
    prev = lambda s: jnp.maximum(s - 1, 0)

    def zspec(width, idx, blk):
        return pl.BlockSpec((tb, width), lambda s: (blk(s), idx))

    state_spec = pl.BlockSpec((nck, RET_HEADS, HEAD_DIM, HEAD_DIM), lambda s: (cur(s), 0, 0, 0))
    row_spec = pl.BlockSpec((tb, D_MODEL), lambda s: (prev(s), 0))
    in_specs = [zspec(RET_WIDTH, 0, cur), zspec(RET_WIDTH, 1, cur), zspec(RET_WIDTH, 2, cur),
                zspec(RET_WIDTH, 3, cur), zspec(POOL_WIDTH, _POOL_BLOCK, cur),
                zspec(SG_WIDTH, _POOL_BLOCK + 1, cur), zspec(SG_WIDTH, _POOL_BLOCK + 2, cur),
                state_spec, state_spec]
    in_specs += [zspec(MERGE_TILE, _GATE_BLOCK + k, prev) for k in range(3 * N_MERGE)]
    in_specs.append(row_spec)
    args = [z] * 7 + [sf_all, sb_all] + [z] * (3 * N_MERGE) + [x]
    pmask, pinv = pool_tabs
    layer_consts = [tabs["dmat"], tabs["qdf"], tabs["qdb"], p["ret_norm_g"]]
    in_specs += [_layer_spec(a, l) for a in layer_consts] + [_const_spec(pmask), _const_spec(pinv)]
    args += layer_consts + [pmask, pinv]
    layer_consts = [p["pool_w"], p["pool_scale"], p["sg_norm_g"], p["sg_w"], tabs["btab"],
                    p["w_br"], p["w_bp"], p["w_bs"], p["w_out"]]
    in_specs += [_layer_spec(a, l) for a in layer_consts]
    args += layer_consts
    in_specs += [_mod_spec(l, 2), _gain_spec(l), _mod_spec(l, 3), _mod_spec(l, 4)]
    args += [mods_all, p["norm2_g"], mods_all, mods_all]
    mix_scratch = [pltpu.VMEM((tb, RET_WIDTH), BF16), pltpu.VMEM((tb, POOL_WIDTH), BF16),
                   pltpu.VMEM((tb, SG_WIDTH), BF16)]
    return pl.pallas_call(
        functools.partial(_mixer_kernel, mod_row=mod_row),
        grid=(nb + 1,),
        in_specs=in_specs,
        out_specs=[row_spec, row_spec],
        out_shape=[jax.ShapeDtypeStruct((t, D_MODEL), F32),
                   jax.ShapeDtypeStruct((t, D_MODEL), BF16)],
        scratch_shapes=[pltpu.VMEM((tb, D_MODEL), BF16)] + mix_scratch + mix_scratch,
        compiler_params=pltpu.CompilerParams(
            dimension_semantics=("arbitrary",), vmem_limit_bytes=VMEM_LIMIT_BYTES),
        name="mixer",
    )(*args)


EPILOGUE_ROWS = 32


def _mlp_kernel(*refs, mod_row, final, n_cast):
    refs = list(refs)
    cast_src = cast_dst = ()
    if n_cast:
        cast_dst = refs[-n_cast:]
        refs = refs[:-n_cast]
        n_out = 1 if final else 2
        cast_src = refs[-n_out - n_cast:-n_out]
        refs = refs[:-n_out - n_cast] + refs[-n_out:]
    if final:
        h2_ref, x1_ref, w1_ref, w2_ref, gate_ref, fng_ref, o_ref = refs
    else:
        (h2_ref, x1_ref, w1_ref, w2_ref, gate_ref, ng_ref, nshift_ref, nscale_ref,
         o_ref, hn_ref) = refs
    f = pl.program_id(1)
    mr = slice(mod_row, mod_row + 1)

    def step(first):
        a = jnp.maximum(jnp.dot(h2_ref[...], w1_ref[...], preferred_element_type=F32), 0.0)
        part = jnp.dot((a * a).astype(BF16), w2_ref[...], preferred_element_type=F32)
        if first:
            o_ref[...] = part
        else:
            o_ref[...] += part
        for src, dst in zip(cast_src, cast_dst):
            dst[...] = src[...].astype(BF16)

    @pl.when(f == 0)
    def _():
        step(True)

    @pl.when(f > 0)
    def _():
        step(False)

    @pl.when(f == pl.num_programs(1) - 1)
    def _():
        gate = gate_ref[mr, :]
        if final:
            gain = fng_ref[...]
        else:
            gain = ng_ref[...] * (1.0 + nscale_ref[mr, :])
            shift = nshift_ref[mr, :]

        for r in range(o_ref.shape[0] // EPILOGUE_ROWS):
            rows = slice(r * EPILOGUE_ROWS, (r + 1) * EPILOGUE_ROWS)
            x2 = x1_ref[rows, :] + gate * o_ref[rows, :]
            if final:
                o_ref[rows, :] = _rms(x2) * gain
            else:
                o_ref[rows, :] = x2
                hn_ref[rows, :] = (_rms(x2) * gain + shift).astype(BF16)


def _mlp(h2, x1, l, w1, w2, mods_all, mod_row, norm1_g, final_gain, tm, tf, cast_next=()):
    t = x1.shape[0]
    final = final_gain is not None
    n_f = D_FF // tf
    grid = (t // tm, n_f)
    rows = pl.BlockSpec((tm, D_MODEL), lambda i, f: (i, 0))
    in_specs = [rows, rows,
                pl.BlockSpec((D_MODEL, tf), lambda i, f: (0, f)),
                pl.BlockSpec((tf, D_MODEL), lambda i, f: (f, 0)),
                _mod_spec(l, 5)]
    args = [h2, x1, w1, w2, mods_all]
    x_shape = jax.ShapeDtypeStruct((t, D_MODEL), F32)
    if final:
        in_specs.append(pl.BlockSpec((1, D_MODEL), lambda i, f: (0, 0)))
        args.append(final_gain)
        out_specs, out_shape = [rows], [x_shape]
    else:
        in_specs += [_gain_spec(l + 1), _mod_spec(l + 1, 0), _mod_spec(l + 1, 1)]
        args += [norm1_g, mods_all, mods_all]
        out_specs = [rows, rows]
        out_shape = [x_shape, jax.ShapeDtypeStruct((t, D_MODEL), BF16)]
    for w in cast_next:
        slab, cols = w.shape[1] // (grid[0] * grid[1]), w.shape[2]
        assert slab * grid[0] * grid[1] == w.shape[1] and slab % (2 * SUBLANES) == 0
        in_specs.append(pl.BlockSpec((None, slab, cols), lambda i, f: (l + 1, i * n_f + f, 0)))
        args.append(w)
        out_specs.append(pl.BlockSpec((slab, cols), lambda i, f: (i * n_f + f, 0)))
        out_shape.append(jax.ShapeDtypeStruct(w.shape[1:], BF16))
    return pl.pallas_call(
        functools.partial(_mlp_kernel, mod_row=mod_row, final=final, n_cast=len(cast_next)),
        grid=grid,
        in_specs=in_specs,
        out_specs=out_specs,
        out_shape=out_shape,
        compiler_params=pltpu.CompilerParams(
            dimension_semantics=("arbitrary", "arbitrary"), vmem_limit_bytes=VMEM_LIMIT_BYTES),
        name="mlp",
    )(*args)


def _rope_tables(t):
    half = HEAD_DIM // 2
    nf = half // 2
    tok = np.arange(t)
    inv = ROPE_THETA ** (-np.arange(nf, dtype=np.float64) / nf)
    lane = np.arange(HEAD_DIM)
    pos = np.where(lane[None, :] < half, (tok // GRID_W)[:, None], (tok % GRID_W)[:, None])
    ang = pos * inv[lane % nf][None, :]
    lower = (lane % half) < nf
    sin = np.sin(ang)
    cos = np.cos(ang).astype(np.float32)
    sin_lo = np.where(lower[None, :], -sin, 0.0).astype(np.float32)
    sin_hi = np.where(lower[None, :], 0.0, sin).astype(np.float32)
    return jnp.asarray(cos), jnp.asarray(sin_lo), jnp.asarray(sin_hi)


def _pool_tables(tb, seg_len):
    pos = np.arange(tb) % seg_len
    base = np.arange(tb) - pos
    col = np.arange(tb)[None, :]
    masks, invs = [], []
    for w in POOL_WINDOWS:
        lo = np.maximum(pos - w // 2, 0)
        hi = np.minimum(pos + w // 2 - 1, seg_len - 1)
        masks.append((col >= (base + lo)[:, None]) & (col <= (base + hi)[:, None]))
        invs.append(np.repeat((1.0 / (hi - lo + 1))[:, None], GROUP_DIM, axis=1))
    pmask = jnp.asarray(np.stack(masks).astype(np.float32), dtype=BF16)
    pinv = jnp.asarray(np.concatenate(invs, axis=1).astype(np.float32))
    return pmask, pinv


def _decay_tables(logit):
    lg = jax.nn.log_sigmoid(logit.astype(F32))
    lgf, lgb = lg[0], lg[1]
    idx = jnp.arange(CHUNK, dtype=F32)
    dist = idx[:, None] - idx[None, :]
    fwd = jnp.exp(lgf[:, None, None] * jnp.maximum(dist, 0.0))
    bwd = jnp.exp(lgb[:, None, None] * jnp.maximum(-dist, 0.0))
    dmat = jnp.where(dist > 0, fwd, jnp.where(dist < 0, bwd, 2.0))

    def lanes(tab):
        return jnp.repeat(tab.T, HEAD_DIM, axis=1)

    return dict(
        dmat=dmat,
        qdf=lanes(jnp.exp(lgf[:, None] * (idx + 1.0)[None])),
        qdb=lanes(jnp.exp(lgb[:, None] * (CHUNK - idx)[None])),
        kdf=lanes(jnp.exp(lgf[:, None] * (CHUNK - 1.0 - idx)[None])),
        kdb=lanes(jnp.exp(lgb[:, None] * idx[None])),
        cdf=jnp.broadcast_to(jnp.exp(lgf * CHUNK)[:, None, None], (RET_HEADS, 1, HEAD_DIM)),
        cdb=jnp.broadcast_to(jnp.exp(lgb * CHUNK)[:, None, None], (RET_HEADS, 1, HEAD_DIM)),
    )


LATENT_ROWS_IN = 1024
LATENT_ROWS_MIX = 256
LATENT_ROWS_MLP = 512
LATENT_STATE_CHUNKS = 4
FF_TILE = 1024


def kernel(x, c, ctx, c_ctx, w_ada, b_ada, norm1_g, w_in, ret_decay_logit, ret_norm_g, pool_w,
           pool_scale, sg_norm_g, sg_w, sg_b, w_br, w_bp, w_bs, w_out, norm2_g, w1, w2, final_norm_g):
    assert x.shape[0] == 1 and ctx.shape[0] == 1
    t = x.shape[1]
    tc = ctx.shape[1]
    xs = x[0]
    xc = ctx[0]

    cond = jnp.stack([c[0], c_ctx])[:, :, None]
    mods_all = _ada(cond, w_ada, b_ada)

    rope = _rope_tables(t)
    pool_lat = _pool_tables(LATENT_ROWS_MIX, GRID_W)
    pool_ctx = _pool_tables(tc, tc)
    zero_state = jnp.zeros((RET_HEADS, HEAD_DIM, HEAD_DIM), F32)

    tabs = jax.vmap(_decay_tables)(ret_decay_logit)
    tabs["btab"] = jnp.repeat(jnp.swapaxes(sg_b, 1, 2), GROUP_DIM, axis=2)
    vec = lambda v: v.reshape(DEPTH, 1, -1)
    p = dict(ret_norm_g=vec(ret_norm_g), pool_w=pool_w.astype(BF16), pool_scale=vec(pool_scale),
             sg_norm_g=vec(sg_norm_g), sg_w=sg_w.astype(BF16),
             w_br=w_br.astype(BF16), w_bp=w_bp.astype(BF16), w_bs=w_bs.astype(BF16),
             w_out=w_out.astype(BF16), norm2_g=vec(norm2_g))
    n1g = vec(norm1_g)
    fng = final_norm_g.reshape(1, -1)
    streamed = (w_in, w1, w2)
    w_in_b, w1_b, w2_b = (w[0].astype(BF16) for w in streamed)

    hc = _norm_mod(xc, 0, n1g, mods_all, ROW_CONTEXT, tc)
    hx = _norm_mod(xs, 0, n1g, mods_all, ROW_LATENT, LATENT_ROWS_MLP)
    for l in range(DEPTH):
        last = l == DEPTH - 1

        zc = _in_proj(hc, w_in_b, None, tc)
        sfc, sbc, sf, sb = _states(zc, l, tabs, zero_state, zero_state, tc // CHUNK)
        if not last:
            xc1, hc2 = _mixer(zc, xc, sfc, sbc, l, tabs, pool_ctx, p, mods_all, ROW_CONTEXT, tc)
            xc, hc = _mlp(hc2, xc1, l, w1_b, w2_b, mods_all, ROW_CONTEXT, n1g, None, tc, FF_TILE)

        zx = _in_proj(hx, w_in_b, rope, LATENT_ROWS_IN)
        sfx, sbx, _, _ = _states(zx, l, tabs, sf, sb, LATENT_STATE_CHUNKS)
        x1, h2 = _mixer(zx, xs, sfx, sbx, l, tabs, pool_lat, p, mods_all, ROW_LATENT, LATENT_ROWS_MIX)
        if last:
            xs, = _mlp(h2, x1, l, w1_b, w2_b, mods_all, ROW_LATENT, n1g, fng, LATENT_ROWS_MLP, FF_TILE)
        else:
            xs, hx, w_in_b, w1_b, w2_b = _mlp(h2, x1, l, w1_b, w2_b, mods_all, ROW_LATENT, n1g, None,
                                              LATENT_ROWS_MLP, FF_TILE, cast_next=streamed)

    return xs[None]
```

```python
import functools

import numpy as np
import jax
import jax.numpy as jnp
from jax import lax
from jax.experimental import pallas as pl
from jax.experimental.pallas import tpu as pltpu

F32 = jnp.float32
BF16 = jnp.bfloat16

D_MODEL = 2048
DEPTH = 4
GRID_W = 64
EPS = 1e-6
N_MOD = 6

HEAD_DIM = 128
RET_WIDTH = D_MODEL // 2
RET_HEADS = RET_WIDTH // HEAD_DIM
CHUNK = 128
K_SCALE = HEAD_DIM ** -0.5
ROPE_THETA = 10000.0

POOL_WIDTH = D_MODEL // 4
POOL_WINDOWS = (2, 4, 8, 16)
POOL_GROUPS = len(POOL_WINDOWS)
GROUP_DIM = POOL_WIDTH // POOL_GROUPS

SG_WIDTH = D_MODEL // 4
SG_GROUPS = 4

D_FF = 4 * D_MODEL
N_IN = 4 * RET_WIDTH + POOL_WIDTH + 2 * SG_WIDTH + 3 * D_MODEL

SUBLANES = 8
LANES = 128
VMEM_LIMIT_BYTES = 56 * 1024 * 1024

IN_TILE = N_IN // 4
N_IN_TILES = N_IN // IN_TILE
QK_COLS = 2 * RET_WIDTH
SUB_TILE = 512
MERGE_TILE = 512
N_MERGE = D_MODEL // MERGE_TILE
_POOL_BLOCK = 4 * RET_WIDTH // POOL_WIDTH
_GATE_BLOCK = (4 * RET_WIDTH + POOL_WIDTH + 2 * SG_WIDTH) // MERGE_TILE

ADA_TILE = 1024
ADA_ROWS = 8
ROW_LATENT, ROW_CONTEXT = 0, 1


def _layer_spec(arr, l):
    nd = arr.ndim - 1
    return pl.BlockSpec((None,) + arr.shape[1:], lambda *_: (l,) + (0,) * nd,
                        pipeline_mode=pl.Buffered(1))


def _const_spec(arr):
    nd = arr.ndim
    return pl.BlockSpec(arr.shape, lambda *_: (0,) * nd, pipeline_mode=pl.Buffered(1))


def _mod_spec(l, k):
    return pl.BlockSpec((None, ADA_ROWS, D_MODEL), lambda *_: (l, 0, k))


def _gain_spec(l):
    return pl.BlockSpec((None, 1, D_MODEL), lambda *_: (l, 0, 0))


def _rms(x):
    return x * lax.rsqrt(jnp.mean(x * x, axis=-1, keepdims=True) + EPS)


def _norm_modulate(x, gain, shift, scale):
    return _rms(x) * gain * (1.0 + scale) + shift


def _ada_kernel(cond_ref, w_ref, b_ref, o_ref, act_scr):
    tn = w_ref.shape[1]

    @pl.when(jnp.logical_and(pl.program_id(0) == 0, pl.program_id(1) == 0))
    def _():
        for r in range(2):
            a = cond_ref[r]
            act_scr[r] = jnp.broadcast_to(a * jax.nn.sigmoid(a), (D_MODEL, LANES))

    def body(g, accs):
        r0 = pl.multiple_of(g * SUBLANES, SUBLANES)
        w = w_ref[pl.ds(r0, SUBLANES), :]
        out = []
        for r in range(2):
            a = jnp.tile(act_scr[r, pl.ds(r0, SUBLANES), :], (1, tn // LANES))
            out.append(accs[r] + w * a)
        return tuple(out)

    zero = jnp.zeros((SUBLANES, tn), F32)
    accs = lax.fori_loop(0, D_MODEL // SUBLANES, body, (zero, zero), unroll=8)
    o_ref[...] = jnp.zeros((ADA_ROWS, tn), F32)
    for r in range(2):
        o_ref[r:r + 1, :] = jnp.sum(accs[r], axis=0, keepdims=True) + b_ref[...]


def _ada(cond, w_ada, b_ada):
    n = N_MOD * D_MODEL
    return pl.pallas_call(
        _ada_kernel,
        grid=(DEPTH, n // ADA_TILE),
        in_specs=[
            pl.BlockSpec((2, D_MODEL, 1), lambda l, j: (0, 0, 0)),
            pl.BlockSpec((None, D_MODEL, ADA_TILE), lambda l, j: (l, 0, j)),
            pl.BlockSpec((None, 1, ADA_TILE), lambda l, j: (l, 0, j)),
        ],
        out_specs=pl.BlockSpec((None, ADA_ROWS, ADA_TILE), lambda l, j: (l, 0, j)),
        out_shape=jax.ShapeDtypeStruct((DEPTH, ADA_ROWS, n), F32),
        scratch_shapes=[pltpu.VMEM((2, D_MODEL, LANES), F32)],
        compiler_params=pltpu.CompilerParams(
            dimension_semantics=("arbitrary", "arbitrary"), vmem_limit_bytes=VMEM_LIMIT_BYTES),
        name="ada",
    )(cond, w_ada, b_ada.reshape(DEPTH, 1, n))


def _norm_kernel(x_ref, g_ref, shift_ref, scale_ref, h_ref, *, mod_row):
    mr = slice(mod_row, mod_row + 1)
    h_ref[...] = _norm_modulate(x_ref[...], g_ref[...], shift_ref[mr, :], scale_ref[mr, :]).astype(BF16)


def _norm_mod(x, l, norm1_g, mods_all, mod_row, tm):
    t = x.shape[0]
    return pl.pallas_call(
        functools.partial(_norm_kernel, mod_row=mod_row),
        grid=(t // tm,),
        in_specs=[pl.BlockSpec((tm, D_MODEL), lambda i: (i, 0)),
                  _gain_spec(l), _mod_spec(l, 0), _mod_spec(l, 1)],
        out_specs=pl.BlockSpec((tm, D_MODEL), lambda i: (i, 0)),
        out_shape=jax.ShapeDtypeStruct((t, D_MODEL), BF16),
        compiler_params=pltpu.CompilerParams(
            dimension_semantics=("arbitrary",), vmem_limit_bytes=VMEM_LIMIT_BYTES),
        name="norm_mod",
    )(x, norm1_g, mods_all, mods_all)


def _rope(t, cos, sin_lo, sin_hi):
    up = pltpu.roll(t, HEAD_DIM - 32, axis=1)
    down = pltpu.roll(t, 32, axis=1)
    return t * cos + up * sin_lo + down * sin_hi


_SEGMENTS = (("q", RET_WIDTH), ("k", RET_WIDTH), ("plain", RET_WIDTH), ("silu", RET_WIDTH),
             ("plain", POOL_WIDTH), ("gelu", SG_WIDTH), ("gelu_rms", SG_WIDTH), ("sigmoid", 3 * D_MODEL))


def _tile_pieces(j):
    lo, hi = j * IN_TILE, (j + 1) * IN_TILE
    pieces, c0 = [], 0
    for kind, width in _SEGMENTS:
        a, b = max(lo, c0), min(hi, c0 + width)
        if a < b:
            assert kind != "gelu_rms" or b - a == width
            step = SUB_TILE if kind in ("q", "k") else b - a
            pieces += [(s - lo, min(step, b - s), kind) for s in range(a, b, step)]
        c0 += width
    return tuple(pieces)


def _inproj_kernel(*refs, use_rope):
    if use_rope:
        h_ref, w_ref, sgng_ref, cos_ref, slo_ref, shi_ref, z_ref = refs
    else:
        h_ref, w_ref, sgng_ref, z_ref = refs
    j = pl.program_id(1)

    def emit(pieces):
        for c0, width, kind in pieces:
            acc = jnp.dot(h_ref[...], w_ref[:, c0:c0 + width], preferred_element_type=F32)
            if kind == "k":
                acc = acc * K_SCALE
            if kind in ("q", "k") and use_rope:
                cos, slo, shi = cos_ref[...], slo_ref[...], shi_ref[...]
                for h0 in range(0, width, HEAD_DIM):
                    z_ref[:, c0 + h0:c0 + h0 + HEAD_DIM] = (
                        _rope(acc[:, h0:h0 + HEAD_DIM], cos, slo, shi).astype(BF16))
                continue
            if kind == "silu":
                acc = acc * jax.nn.sigmoid(acc)
            elif kind == "gelu":
                acc = jax.nn.gelu(acc)
            elif kind == "gelu_rms":
                acc = _rms(jax.nn.gelu(acc)) * sgng_ref[...]
            elif kind == "sigmoid":
                acc = jax.nn.sigmoid(acc)
            z_ref[:, c0:c0 + width] = acc.astype(BF16)

    patterns = {}
    for jj in range(N_IN_TILES):
        patterns.setdefault(_tile_pieces(jj), []).append(jj)
    for pieces, tiles in patterns.items():
        cond = functools.reduce(jnp.logical_or, [j == jj for jj in tiles])
        pl.when(cond)(functools.partial(emit, pieces))


def _in_proj(h, w_in, sg_norm_g, l, rope, tm):
    t = h.shape[0]
    use_rope = rope is not None
    in_specs = [
        pl.BlockSpec((tm, D_MODEL), lambda i, j: (i, 0)),
        pl.BlockSpec((D_MODEL, IN_TILE), lambda i, j: (0, j)),
        pl.BlockSpec((None, 1, SG_WIDTH), lambda i, j: (l, 0, 0)),
    ]
    args = [h, w_in, sg_norm_g]
    if use_rope:
        in_specs += [pl.BlockSpec((tm, HEAD_DIM), lambda i, j: (i, 0))] * 3
        args += list(rope)
    return pl.pallas_call(
        functools.partial(_inproj_kernel, use_rope=use_rope),
        grid=(t // tm, N_IN_TILES),
        in_specs=in_specs,
        out_specs=pl.BlockSpec((tm, IN_TILE), lambda i, j: (i, j)),
        out_shape=jax.ShapeDtypeStruct((t, N_IN), BF16),
        compiler_params=pltpu.CompilerParams(
            dimension_semantics=("arbitrary", "arbitrary"), vmem_limit_bytes=VMEM_LIMIT_BYTES),
        name="in_proj",
    )(*args)


def _state_kernel(kf_ref, vf_ref, kb_ref, vb_ref, kdf_ref, kdb_ref, cdf_ref, cdb_ref, sf0_ref, sb0_ref,
                  sf_all_ref, sb_all_ref, sf_fin_ref, sb_fin_ref, sf_scr, sb_scr):
    t = pl.program_id(0)
    cps = kf_ref.shape[0] // CHUNK

    @pl.when(t == 0)
    def _():
        sf_scr[...] = sf0_ref[...]
        sb_scr[...] = sb0_ref[...]

    contract_rows = (((0,), (0,)), ((), ()))

    def advance(c, k_ref, v_ref, kd_ref, cd_ref, all_ref, scr):
        rows = slice(c * CHUNK, (c + 1) * CHUNK)
        all_ref[c] = scr[...].astype(BF16)
        for h in range(RET_HEADS):
            hs = slice(h * HEAD_DIM, (h + 1) * HEAD_DIM)
            kd = (k_ref[rows, hs].astype(F32) * kd_ref[:, hs]).astype(BF16)
            kv = lax.dot_general(kd, v_ref[rows, hs], contract_rows, preferred_element_type=F32)
            scr[h] = scr[h] * cd_ref[h] + kv

    for c in range(cps):
        advance(c, kf_ref, vf_ref, kdf_ref, cdf_ref, sf_all_ref, sf_scr)
        advance(cps - 1 - c, kb_ref, vb_ref, kdb_ref, cdb_ref, sb_all_ref, sb_scr)

    @pl.when(t == pl.num_programs(0) - 1)
    def _():
        sf_fin_ref[...] = sf_scr[...]
        sb_fin_ref[...] = sb_scr[...]


def _states(z, l, tabs, sf0, sb0, cps):
    t = z.shape[0]
    nb = t // (cps * CHUNK)
    fwd = lambda c: pl.BlockSpec((cps * CHUNK, RET_WIDTH), lambda i: (i, c))
    bwd = lambda c: pl.BlockSpec((cps * CHUNK, RET_WIDTH), lambda i: (nb - 1 - i, c))
    state_shape = (RET_HEADS, HEAD_DIM, HEAD_DIM)
    consts = [tabs["kdf"], tabs["kdb"], tabs["cdf"], tabs["cdb"]]
    in_specs = [fwd(1), fwd(2), bwd(1), bwd(2)] + [_layer_spec(a, l) for a in consts]
    in_specs += [_const_spec(sf0), _const_spec(sb0)]
    all_shape = jax.ShapeDtypeStruct((t // CHUNK,) + state_shape, BF16)
    fin_shape = jax.ShapeDtypeStruct(state_shape, F32)
    return pl.pallas_call(
        _state_kernel,
        grid=(nb,),
        in_specs=in_specs,
        out_specs=[
            pl.BlockSpec((cps,) + state_shape, lambda i: (i, 0, 0, 0)),
            pl.BlockSpec((cps,) + state_shape, lambda i: (nb - 1 - i, 0, 0, 0)),
            pl.BlockSpec(state_shape, lambda i: (0, 0, 0)),
            pl.BlockSpec(state_shape, lambda i: (0, 0, 0)),
        ],
        out_shape=[all_shape, all_shape, fin_shape, fin_shape],
        scratch_shapes=[pltpu.VMEM(state_shape, F32), pltpu.VMEM(state_shape, F32)],
        compiler_params=pltpu.CompilerParams(
            dimension_semantics=("arbitrary",), vmem_limit_bytes=VMEM_LIMIT_BYTES),
        name="ret_states",
    )(z, z, z, z, *consts, sf0, sb0)


ZERO_ROWS = 16


def _mixer_kernel(*refs, mod_row):
    zq_ref, zk_ref, zv_ref, zg_ref, zp_ref, zu_ref, zsv_ref, sf_ref, sb_ref = refs[:9]
    zgr_refs, zgp_refs, zgs_refs = (refs[9 + k * N_MERGE:9 + (k + 1) * N_MERGE] for k in range(3))
    (x_ref,
     dmat_ref, qdf_ref, qdb_ref, retg_ref, pmask_ref, pinv_ref, poolw_ref, pscale_ref,
     sgw_ref, btab_ref, wbr_ref, wbp_ref, wbs_ref, wout_ref,
     gate_ref, n2g_ref, shift2_ref, scale2_ref,
     x1_ref, h2_ref,
     y_scr, ret_cur, pool_cur, sg_cur, ret_prev, pool_prev, sg_prev) = refs[9 + 3 * N_MERGE:]
    tb = x_ref.shape[0]
    nck = tb // CHUNK
    contract_last = (((1,), (1,)), ((), ()))
    mr = slice(mod_row, mod_row + 1)

    @pl.when(pl.program_id(0) == 0)
    def _():
        def zero_rows(r, carry):
            rows = pl.ds(pl.multiple_of(r * ZERO_ROWS, ZERO_ROWS), ZERO_ROWS)
            ret_prev[rows, :] = jnp.zeros((ZERO_ROWS, RET_WIDTH), BF16)
            pool_prev[rows, :] = jnp.zeros((ZERO_ROWS, POOL_WIDTH), BF16)
            sg_prev[rows, :] = jnp.zeros((ZERO_ROWS, SG_WIDTH), BF16)
            return carry

        lax.fori_loop(0, tb // ZERO_ROWS, zero_rows, 0)

    def merge_piece(i):
        cs = slice(i * MERGE_TILE, (i + 1) * MERGE_TILE)
        y = zgr_refs[i][...].astype(F32) * jnp.dot(
            ret_prev[...], wbr_ref[:, cs], preferred_element_type=F32)
        y = y + zgp_refs[i][...].astype(F32) * jnp.dot(
            pool_prev[...], wbp_ref[:, cs], preferred_element_type=F32)
        y = y + zgs_refs[i][...].astype(F32) * jnp.dot(
            sg_prev[...], wbs_ref[:, cs], preferred_element_type=F32)
        y_scr[:, cs] = y.astype(BF16)

    def out_piece(i):
        cs = slice(i * MERGE_TILE, (i + 1) * MERGE_TILE)
        out = jnp.dot(y_scr[...], wout_ref[:, cs], preferred_element_type=F32)
        x1_ref[:, cs] = x_ref[:, cs] + gate_ref[mr, cs] * out

    def ret_unit(c, h):
        rows = slice(c * CHUNK, (c + 1) * CHUNK)
        hs = slice(h * HEAD_DIM, (h + 1) * HEAD_DIM)
        qb = zq_ref[rows, hs]
        scores = lax.dot_general(qb, zk_ref[rows, hs], contract_last, preferred_element_type=F32)
        scores = (scores * dmat_ref[h]).astype(BF16)
        cross = qdf_ref[:, hs] * jnp.dot(qb, sf_ref[c, h], preferred_element_type=F32)
        cross = cross + qdb_ref[:, hs] * jnp.dot(qb, sb_ref[c, h], preferred_element_type=F32)

        def second():
            o = jnp.dot(scores, zv_ref[rows, hs], preferred_element_type=F32) + cross
            ret = _rms(o) * retg_ref[:, hs] * zg_ref[rows, hs].astype(F32)
            ret_cur[rows, hs] = ret.astype(BF16)

        return second

    def pool_unit(gi):
        gs = slice(gi * GROUP_DIM, (gi + 1) * GROUP_DIM)
        pb = zp_ref[:, gs]
        win = jnp.dot(pmask_ref[gi], pb, preferred_element_type=F32)
        pooled = (win * pinv_ref[:, gs] - pb.astype(F32)).astype(BF16)

        def second():
            mapped = jnp.dot(pooled, poolw_ref[gi], preferred_element_type=F32)
            pool_cur[:, gs] = (mapped * pscale_ref[:, gs]).astype(BF16)

        return second

    def sg_unit(c, gi):
        rows = slice(c * CHUNK, (c + 1) * CHUNK)
        gs = slice(gi * GROUP_DIM, (gi + 1) * GROUP_DIM)
        mixed = jnp.dot(sgw_ref[gi], zsv_ref[rows, gs], preferred_element_type=F32)

        def second():
            gated = zu_ref[rows, gs].astype(F32) * (mixed + btab_ref[:, gs])
            sg_cur[rows, gs] = gated.astype(BF16)

        return second

    mix_units = [functools.partial(ret_unit, c, h) for c in range(nck) for h in range(RET_HEADS)]
    mix_units += [functools.partial(pool_unit, gi) for gi in range(POOL_GROUPS)]
    mix_units += [functools.partial(sg_unit, c, gi) for c in range(nck) for gi in range(SG_GROUPS)]
    pieces = ([functools.partial(merge_piece, i) for i in range(N_MERGE)]
              + [functools.partial(out_piece, i) for i in range(N_MERGE)])
    per_piece = -(-len(mix_units) // len(pieces))
    pending = [unit() for unit in mix_units[:per_piece]]
    for i, piece in enumerate(pieces):
        piece()
        for second in pending:
            second()
        pending = [unit() for unit in mix_units[(i + 1) * per_piece:(i + 2) * per_piece]]
    assert not pending
    h2_ref[...] = _norm_modulate(x1_ref[...], n2g_ref[...], shift2_ref[mr, :],
                                 scale2_ref[mr, :]).astype(BF16)

    ret_prev[...] = ret_cur[...]
    pool_prev[...] = pool_cur[...]
    sg_prev[...] = sg_cur[...]


def _mixer(z, x, sf_all, sb_all, l, tabs, pool_tabs, p, mods_all, mod_row, tb):
    t = x.shape[0]
    nb = t // tb
    nck = tb // CHUNK
    cur = lambda s: jnp.minimum(s, nb - 1)
    prev = lambda s: jnp.maximum(s - 1, 0)

    def zspec(width, idx, blk):
        return pl.BlockSpec((tb, width), lambda s: (blk(s), idx))

    state_spec = pl.BlockSpec((nck, RET_HEADS, HEAD_DIM, HEAD_DIM), lambda s: (cur(s), 0, 0, 0))
    row_spec = pl.BlockSpec((tb, D_MODEL), lambda s: (prev(s), 0))
    in_specs = [zspec(RET_WIDTH, 0, cur), zspec(RET_WIDTH, 1, cur), zspec(RET_WIDTH, 2, cur),
                zspec(RET_WIDTH, 3, cur), zspec(POOL_WIDTH, _POOL_BLOCK, cur),
                zspec(SG_WIDTH, _POOL_BLOCK + 1, cur), zspec(SG_WIDTH, _POOL_BLOCK + 2, cur),
                state_spec, state_spec]
    in_specs += [zspec(MERGE_TILE, _GATE_BLOCK + k, prev) for k in range(3 * N_MERGE)]
    in_specs.append(row_spec)
    args = [z] * 7 + [sf_all, sb_all] + [z] * (3 * N_MERGE) + [x]
    pmask, pinv = pool_tabs
    layer_consts = [tabs["dmat"], tabs["qdf"], tabs["qdb"], p["ret_norm_g"]]
    in_specs += [_layer_spec(a, l) for a in layer_consts] + [_const_spec(pmask), _const_spec(pinv)]
    args += layer_consts + [pmask, pinv]
    layer_consts = [p["pool_w"], p["pool_scale"], p["sg_w"], tabs["btab"],
                    p["w_br"], p["w_bp"], p["w_bs"], p["w_out"]]
    in_specs += [_layer_spec(a, l) for a in layer_consts]
    args += layer_consts
    in_specs += [_mod_spec(l, 2), _gain_spec(l), _mod_spec(l, 3), _mod_spec(l, 4)]
    args += [mods_all, p["norm2_g"], mods_all, mods_all]
    mix_scratch = [pltpu.VMEM((tb, RET_WIDTH), BF16), pltpu.VMEM((tb, POOL_WIDTH), BF16),
                   pltpu.VMEM((tb, SG_WIDTH), BF16)]
    return pl.pallas_call(
        functools.partial(_mixer_kernel, mod_row=mod_row),
        grid=(nb + 1,),
        in_specs=in_specs,
        out_specs=[row_spec, row_spec],
        out_shape=[jax.ShapeDtypeStruct((t, D_MODEL), F32),
                   jax.ShapeDtypeStruct((t, D_MODEL), BF16)],
        scratch_shapes=[pltpu.VMEM((tb, D_MODEL), BF16)] + mix_scratch + mix_scratch,
        compiler_params=pltpu.CompilerParams(
            dimension_semantics=("arbitrary",), vmem_limit_bytes=VMEM_LIMIT_BYTES),
        name="mixer",
    )(*args)


EPILOGUE_ROWS = 32


def _mlp_kernel(*refs, mod_row, final, n_cast):
    refs = list(refs)
    cast_src = cast_dst = ()
    if n_cast:
        cast_dst = refs[-n_cast:]
        refs = refs[:-n_cast]
        n_out = 1 if final else 2
        cast_src = refs[-n_out - n_cast:-n_out]
        refs = refs[:-n_out - n_cast] + refs[-n_out:]
    if final:
        h2_ref, x1_ref, w1_ref, w2_ref, gate_ref, fng_ref, o_ref = refs
    else:
        (h2_ref, x1_ref, w1_ref, w2_ref, gate_ref, ng_ref, nshift_ref, nscale_ref,
         o_ref, hn_ref) = refs
    f = pl.program_id(1)
    mr = slice(mod_row, mod_row + 1)

    def step(first):
        a = jnp.maximum(jnp.dot(h2_ref[...], w1_ref[...], preferred_element_type=F32), 0.0)
        part = jnp.dot((a * a).astype(BF16), w2_ref[...], preferred_element_type=F32)
        if first:
            o_ref[...] = part
        else:
            o_ref[...] += part
        for src, dst in zip(cast_src, cast_dst):
            dst[...] = src[...].astype(BF16)

    @pl.when(f == 0)
    def _():
        step(True)

    @pl.when(f > 0)
    def _():
        step(False)

    @pl.when(f == pl.num_programs(1) - 1)
    def _():
        gate = gate_ref[mr, :]
        if final:
            gain = fng_ref[...]
        else:
            gain = ng_ref[...] * (1.0 + nscale_ref[mr, :])
            shift = nshift_ref[mr, :]

        for r in range(o_ref.shape[0] // EPILOGUE_ROWS):
            rows = slice(r * EPILOGUE_ROWS, (r + 1) * EPILOGUE_ROWS)
            x2 = x1_ref[rows, :] + gate * o_ref[rows, :]
            if final:
                o_ref[rows, :] = _rms(x2) * gain
            else:
                o_ref[rows, :] = x2
                hn_ref[rows, :] = (_rms(x2) * gain + shift).astype(BF16)


def _mlp(h2, x1, l, w1, w2, mods_all, mod_row, norm1_g, final_gain, tm, tf, cast_next=()):
    t = x1.shape[0]
    final = final_gain is not None
    n_f = D_FF // tf
    grid = (t // tm, n_f)
    rows = pl.BlockSpec((tm, D_MODEL), lambda i, f: (i, 0))
    in_specs = [rows, rows,
                pl.BlockSpec((D_MODEL, tf), lambda i, f: (0, f)),
                pl.BlockSpec((tf, D_MODEL), lambda i, f: (f, 0)),
                _mod_spec(l, 5)]
    args = [h2, x1, w1, w2, mods_all]
    x_shape = jax.ShapeDtypeStruct((t, D_MODEL), F32)
    if final:
        in_specs.append(pl.BlockSpec((1, D_MODEL), lambda i, f: (0, 0)))
        args.append(final_gain)
        out_specs, out_shape = [rows], [x_shape]
    else:
        in_specs += [_gain_spec(l + 1), _mod_spec(l + 1, 0), _mod_spec(l + 1, 1)]
        args += [norm1_g, mods_all, mods_all]
        out_specs = [rows, rows]
        out_shape = [x_shape, jax.ShapeDtypeStruct((t, D_MODEL), BF16)]
    for w in cast_next:
        slab, cols = w.shape[1] // (grid[0] * grid[1]), w.shape[2]
        assert slab * grid[0] * grid[1] == w.shape[1] and slab % (2 * SUBLANES) == 0
        in_specs.append(pl.BlockSpec((None, slab, cols), lambda i, f: (l + 1, i * n_f + f, 0)))
        args.append(w)
        out_specs.append(pl.BlockSpec((slab, cols), lambda i, f: (i * n_f + f, 0)))
        out_shape.append(jax.ShapeDtypeStruct(w.shape[1:], BF16))
    return pl.pallas_call(
        functools.partial(_mlp_kernel, mod_row=mod_row, final=final, n_cast=len(cast_next)),
        grid=grid,
        in_specs=in_specs,
        out_specs=out_specs,
        out_shape=out_shape,
        compiler_params=pltpu.CompilerParams(
            dimension_semantics=("arbitrary", "arbitrary"), vmem_limit_bytes=VMEM_LIMIT_BYTES),
        name="mlp",
    )(*args)


def _rope_tables(t):
    half = HEAD_DIM // 2
    nf = half // 2
    tok = np.arange(t)
    inv = ROPE_THETA ** (-np.arange(nf, dtype=np.float64) / nf)
    lane = np.arange(HEAD_DIM)
    pos = np.where(lane[None, :] < half, (tok // GRID_W)[:, None], (tok % GRID_W)[:, None])
    ang = pos * inv[lane % nf][None, :]
    lower = (lane % half) < nf
    sin = np.sin(ang)
    cos = np.cos(ang).astype(np.float32)
    sin_lo = np.where(lower[None, :], -sin, 0.0).astype(np.float32)
    sin_hi = np.where(lower[None, :], 0.0, sin).astype(np.float32)
    return jnp.asarray(cos), jnp.asarray(sin_lo), jnp.asarray(sin_hi)


def _pool_tables(tb, seg_len):
    pos = np.arange(tb) % seg_len
    base = np.arange(tb) - pos
    col = np.arange(tb)[None, :]
    masks, invs = [], []
    for w in POOL_WINDOWS:
        lo = np.maximum(pos - w // 2, 0)
        hi = np.minimum(pos + w // 2 - 1, seg_len - 1)
        masks.append((col >= (base + lo)[:, None]) & (col <= (base + hi)[:, None]))
        invs.append(np.repeat((1.0 / (hi - lo + 1))[:, None], GROUP_DIM, axis=1))
    pmask = jnp.asarray(np.stack(masks).astype(np.float32), dtype=BF16)
    pinv = jnp.asarray(np.concatenate(invs, axis=1).astype(np.float32))
    return pmask, pinv


def _decay_tables(logit):
    lg = jax.nn.log_sigmoid(logit.astype(F32))
    lgf, lgb = lg[0], lg[1]
    idx = jnp.arange(CHUNK, dtype=F32)
    dist = idx[:, None] - idx[None, :]
    fwd = jnp.exp(lgf[:, None, None] * jnp.maximum(dist, 0.0))
    bwd = jnp.exp(lgb[:, None, None] * jnp.maximum(-dist, 0.0))
    dmat = jnp.where(dist > 0, fwd, jnp.where(dist < 0, bwd, 2.0))

    def lanes(tab):
        return jnp.repeat(tab.T, HEAD_DIM, axis=1)

    return dict(
        dmat=dmat,
        qdf=lanes(jnp.exp(lgf[:, None] * (idx + 1.0)[None])),
        qdb=lanes(jnp.exp(lgb[:, None] * (CHUNK - idx)[None])),
        kdf=lanes(jnp.exp(lgf[:, None] * (CHUNK - 1.0 - idx)[None])),
        kdb=lanes(jnp.exp(lgb[:, None] * idx[None])),
        cdf=jnp.broadcast_to(jnp.exp(lgf * CHUNK)[:, None, None], (RET_HEADS, 1, HEAD_DIM)),
        cdb=jnp.broadcast_to(jnp.exp(lgb * CHUNK)[:, None, None], (RET_HEADS, 1, HEAD_DIM)),
    )


LATENT_ROWS_IN = 1024
LATENT_ROWS_MIX = 256
LATENT_ROWS_MLP = 512
LATENT_STATE_CHUNKS = 4
FF_TILE = 1024


def kernel(x, c, ctx, c_ctx, w_ada, b_ada, norm1_g, w_in, ret_decay_logit, ret_norm_g, pool_w,
           pool_scale, sg_norm_g, sg_w, sg_b, w_br, w_bp, w_bs, w_out, norm2_g, w1, w2, final_norm_g):
    assert x.shape[0] == 1 and ctx.shape[0] == 1
    t = x.shape[1]
    tc = ctx.shape[1]
    xs = x[0]
    xc = ctx[0]

    cond = jnp.stack([c[0], c_ctx])[:, :, None]
    mods_all = _ada(cond, w_ada, b_ada)

    rope = _rope_tables(t)
    pool_lat = _pool_tables(LATENT_ROWS_MIX, GRID_W)
    pool_ctx = _pool_tables(tc, tc)
    zero_state = jnp.zeros((RET_HEADS, HEAD_DIM, HEAD_DIM), F32)

    tabs = jax.vmap(_decay_tables)(ret_decay_logit)
    tabs["btab"] = jnp.repeat(jnp.swapaxes(sg_b, 1, 2), GROUP_DIM, axis=2)
    vec = lambda v: v.reshape(DEPTH, 1, -1)
    p = dict(ret_norm_g=vec(ret_norm_g), pool_w=pool_w.astype(BF16), pool_scale=vec(pool_scale),
             sg_norm_g=vec(sg_norm_g), sg_w=sg_w.astype(BF16),
             w_br=w_br.astype(BF16), w_bp=w_bp.astype(BF16), w_bs=w_bs.astype(BF16),
             w_out=w_out.astype(BF16), norm2_g=vec(norm2_g))
    n1g = vec(norm1_g)
    fng = final_norm_g.reshape(1, -1)
    streamed = (w_in, w1, w2)
    w_in_b, w1_b, w2_b = (w[0].astype(BF16) for w in streamed)

    hc = _norm_mod(xc, 0, n1g, mods_all, ROW_CONTEXT, tc)
    hx = _norm_mod(xs, 0, n1g, mods_all, ROW_LATENT, LATENT_ROWS_MLP)
    for l in range(DEPTH):
        last = l == DEPTH - 1

        zc = _in_proj(hc, w_in_b, p["sg_norm_g"], l, None, tc)
        sfc, sbc, sf, sb = _states(zc, l, tabs, zero_state, zero_state, tc // CHUNK)
        if not last:
            xc1, hc2 = _mixer(zc, xc, sfc, sbc, l, tabs, pool_ctx, p, mods_all, ROW_CONTEXT, tc)
            xc, hc = _mlp(hc2, xc1, l, w1_b, w2_b, mods_all, ROW_CONTEXT, n1g, None, tc, FF_TILE)

        zx = _in_proj(hx, w_in_b, p["sg_norm_g"], l, rope, LATENT_ROWS_IN)
        sfx, sbx, _, _ = _states(zx, l, tabs, sf, sb, LATENT_STATE_CHUNKS)
        x1, h2 = _mixer(zx, xs, sfx, sbx, l, tabs, pool_lat, p, mods_all, ROW_LATENT, LATENT_ROWS_MIX)
        if last:
            xs, = _mlp(h2, x1, l, w1_b, w2_b, mods_all, ROW_LATENT, n1g, fng, LATENT_ROWS_MLP, FF_TILE)
        else:
            xs, hx, w_in_b, w1_b, w2_b = _mlp(h2, x1, l, w1_b, w2_b, mods_all, ROW_LATENT, n1g, None,
                                              LATENT_ROWS_MLP, FF_TILE, cast_next=streamed)

    return xs[None]
```

```python
import functools

import numpy as np
import jax
import jax.numpy as jnp
from jax import lax
from jax.experimental import pallas as pl
from jax.experimental.pallas import tpu as pltpu

F32 = jnp.float32
BF16 = jnp.bfloat16

D_MODEL = 2048
DEPTH = 4
GRID_W = 64
EPS = 1e-6
N_MOD = 6

HEAD_DIM = 128
RET_WIDTH = D_MODEL // 2
RET_HEADS = RET_WIDTH // HEAD_DIM
CHUNK = 128
K_SCALE = HEAD_DIM ** -0.5
ROPE_THETA = 10000.0

POOL_WIDTH = D_MODEL // 4
POOL_WINDOWS = (2, 4, 8, 16)
POOL_GROUPS = len(POOL_WINDOWS)
GROUP_DIM = POOL_WIDTH // POOL_GROUPS

SG_WIDTH = D_MODEL // 4
SG_GROUPS = 4

D_FF = 4 * D_MODEL
N_IN = 4 * RET_WIDTH + POOL_WIDTH + 2 * SG_WIDTH + 3 * D_MODEL

SUBLANES = 8
LANES = 128
VMEM_LIMIT_BYTES = 56 * 1024 * 1024

IN_TILE = N_IN // 4
N_IN_TILES = N_IN // IN_TILE
QK_COLS = 2 * RET_WIDTH
SUB_TILE = 512
_HEADS_PER_SUB = SUB_TILE // HEAD_DIM
MERGE_TILE = 512
N_MERGE = D_MODEL // MERGE_TILE
_POOL_BLOCK = 4 * RET_WIDTH // POOL_WIDTH
_GATE_BLOCK = (4 * RET_WIDTH + POOL_WIDTH + 2 * SG_WIDTH) // MERGE_TILE

ADA_TILE = 1024
ADA_ROWS = 8
ROW_LATENT, ROW_CONTEXT = 0, 1


def _layer_spec(arr, l):
    nd = arr.ndim - 1
    return pl.BlockSpec((None,) + arr.shape[1:], lambda *_: (l,) + (0,) * nd,
                        pipeline_mode=pl.Buffered(1))


def _const_spec(arr):
    nd = arr.ndim
    return pl.BlockSpec(arr.shape, lambda *_: (0,) * nd, pipeline_mode=pl.Buffered(1))


def _mod_spec(l, k):
    return pl.BlockSpec((None, ADA_ROWS, D_MODEL), lambda *_: (l, 0, k))


def _gain_spec(l):
    return pl.BlockSpec((None, 1, D_MODEL), lambda *_: (l, 0, 0))


def _rms(x):
    return x * lax.rsqrt(jnp.mean(x * x, axis=-1, keepdims=True) + EPS)


def _norm_modulate(x, gain, shift, scale):
    return _rms(x) * gain * (1.0 + scale) + shift


def _ada_kernel(cond_ref, w_ref, b_ref, o_ref, act_scr):
    tn = w_ref.shape[1]

    @pl.when(jnp.logical_and(pl.program_id(0) == 0, pl.program_id(1) == 0))
    def _():
        for r in range(2):
            a = cond_ref[r]
            act_scr[r] = jnp.broadcast_to(a * jax.nn.sigmoid(a), (D_MODEL, LANES))

    def body(g, accs):
        r0 = pl.multiple_of(g * SUBLANES, SUBLANES)
        w = w_ref[pl.ds(r0, SUBLANES), :]
        out = []
        for r in range(2):
            a = jnp.tile(act_scr[r, pl.ds(r0, SUBLANES), :], (1, tn // LANES))
            out.append(accs[r] + w * a)
        return tuple(out)

    zero = jnp.zeros((SUBLANES, tn), F32)
    accs = lax.fori_loop(0, D_MODEL // SUBLANES, body, (zero, zero), unroll=8)
    o_ref[...] = jnp.zeros((ADA_ROWS, tn), F32)
    for r in range(2):
        o_ref[r:r + 1, :] = jnp.sum(accs[r], axis=0, keepdims=True) + b_ref[...]


def _ada(cond, w_ada, b_ada):
    n = N_MOD * D_MODEL
    return pl.pallas_call(
        _ada_kernel,
        grid=(DEPTH, n // ADA_TILE),
        in_specs=[
            pl.BlockSpec((2, D_MODEL, 1), lambda l, j: (0, 0, 0)),
            pl.BlockSpec((None, D_MODEL, ADA_TILE), lambda l, j: (l, 0, j)),
            pl.BlockSpec((None, 1, ADA_TILE), lambda l, j: (l, 0, j)),
        ],
        out_specs=pl.BlockSpec((None, ADA_ROWS, ADA_TILE), lambda l, j: (l, 0, j)),
        out_shape=jax.ShapeDtypeStruct((DEPTH, ADA_ROWS, n), F32),
        scratch_shapes=[pltpu.VMEM((2, D_MODEL, LANES), F32)],
        compiler_params=pltpu.CompilerParams(
            dimension_semantics=("arbitrary", "arbitrary"), vmem_limit_bytes=VMEM_LIMIT_BYTES),
        name="ada",
    )(cond, w_ada, b_ada.reshape(DEPTH, 1, n))


def _norm_kernel(x_ref, g_ref, shift_ref, scale_ref, h_ref, *, mod_row):
    mr = slice(mod_row, mod_row + 1)
    h_ref[...] = _norm_modulate(x_ref[...], g_ref[...], shift_ref[mr, :], scale_ref[mr, :]).astype(BF16)


def _norm_mod(x, l, norm1_g, mods_all, mod_row, tm):
    t = x.shape[0]
    return pl.pallas_call(
        functools.partial(_norm_kernel, mod_row=mod_row),
        grid=(t // tm,),
        in_specs=[pl.BlockSpec((tm, D_MODEL), lambda i: (i, 0)),
                  _gain_spec(l), _mod_spec(l, 0), _mod_spec(l, 1)],
        out_specs=pl.BlockSpec((tm, D_MODEL), lambda i: (i, 0)),
        out_shape=jax.ShapeDtypeStruct((t, D_MODEL), BF16),
        compiler_params=pltpu.CompilerParams(
            dimension_semantics=("arbitrary",), vmem_limit_bytes=VMEM_LIMIT_BYTES),
        name="norm_mod",
    )(x, norm1_g, mods_all, mods_all)


def _rope(t, cos, sin_lo, sin_hi):
    up = pltpu.roll(t, HEAD_DIM - 32, axis=1)
    down = pltpu.roll(t, 32, axis=1)
    return t * cos + up * sin_lo + down * sin_hi


def _inproj_kernel(*refs, use_rope):
    if use_rope:
        h_ref, w_ref, cos_ref, slo_ref, shi_ref, z_ref = refs
    else:
        h_ref, w_ref, z_ref = refs
    j = pl.program_id(1)

    @pl.when(j == 0)
    def _():
        for blk in range(QK_COLS // SUB_TILE):
            c0 = blk * SUB_TILE
            acc = jnp.dot(h_ref[...], w_ref[:, c0:c0 + SUB_TILE], preferred_element_type=F32)
            if c0 >= RET_WIDTH:
                acc = acc * K_SCALE
            if use_rope:
                cos, slo, shi = cos_ref[...], slo_ref[...], shi_ref[...]
                for hh in range(_HEADS_PER_SUB):
                    hs = slice(hh * HEAD_DIM, (hh + 1) * HEAD_DIM)
                    z_ref[:, c0 + hh * HEAD_DIM:c0 + (hh + 1) * HEAD_DIM] = (
                        _rope(acc[:, hs], cos, slo, shi).astype(BF16))
            else:
                z_ref[:, c0:c0 + SUB_TILE] = acc.astype(BF16)
        z_ref[:, QK_COLS:] = jnp.dot(h_ref[...], w_ref[:, QK_COLS:],
                                     preferred_element_type=F32).astype(BF16)

    @pl.when(j > 0)
    def _():
        z_ref[...] = jnp.dot(h_ref[...], w_ref[...], preferred_element_type=F32).astype(BF16)


def _in_proj(h, w_in, rope, tm):
    t = h.shape[0]
    use_rope = rope is not None
    in_specs = [
        pl.BlockSpec((tm, D_MODEL), lambda i, j: (i, 0)),
        pl.BlockSpec((D_MODEL, IN_TILE), lambda i, j: (0, j)),
    ]
    args = [h, w_in]
    if use_rope:
        in_specs += [pl.BlockSpec((tm, HEAD_DIM), lambda i, j: (i, 0))] * 3
        args += list(rope)
    return pl.pallas_call(
        functools.partial(_inproj_kernel, use_rope=use_rope),
        grid=(t // tm, N_IN_TILES),
        in_specs=in_specs,
        out_specs=pl.BlockSpec((tm, IN_TILE), lambda i, j: (i, j)),
        out_shape=jax.ShapeDtypeStruct((t, N_IN), BF16),
        compiler_params=pltpu.CompilerParams(
            dimension_semantics=("arbitrary", "arbitrary"), vmem_limit_bytes=VMEM_LIMIT_BYTES),
        name="in_proj",
    )(*args)


def _state_kernel(kf_ref, vf_ref, kb_ref, vb_ref, kdf_ref, kdb_ref, cdf_ref, cdb_ref, sf0_ref, sb0_ref,
                  sf_all_ref, sb_all_ref, sf_fin_ref, sb_fin_ref, sf_scr, sb_scr):
    t = pl.program_id(0)
    cps = kf_ref.shape[0] // CHUNK

    @pl.when(t == 0)
    def _():
        sf_scr[...] = sf0_ref[...]
        sb_scr[...] = sb0_ref[...]

    contract_rows = (((0,), (0,)), ((), ()))

    def advance(c, k_ref, v_ref, kd_ref, cd_ref, all_ref, scr):
        rows = slice(c * CHUNK, (c + 1) * CHUNK)
        all_ref[c] = scr[...].astype(BF16)
        for h in range(RET_HEADS):
            hs = slice(h * HEAD_DIM, (h + 1) * HEAD_DIM)
            kd = (k_ref[rows, hs].astype(F32) * kd_ref[:, hs]).astype(BF16)
            kv = lax.dot_general(kd, v_ref[rows, hs], contract_rows, preferred_element_type=F32)
            scr[h] = scr[h] * cd_ref[h] + kv

    for c in range(cps):
        advance(c, kf_ref, vf_ref, kdf_ref, cdf_ref, sf_all_ref, sf_scr)
        advance(cps - 1 - c, kb_ref, vb_ref, kdb_ref, cdb_ref, sb_all_ref, sb_scr)

    @pl.when(t == pl.num_programs(0) - 1)
    def _():
        sf_fin_ref[...] = sf_scr[...]
        sb_fin_ref[...] = sb_scr[...]


def _states(z, l, tabs, sf0, sb0, cps):
    t = z.shape[0]
    nb = t // (cps * CHUNK)
    fwd = lambda c: pl.BlockSpec((cps * CHUNK, RET_WIDTH), lambda i: (i, c))
    bwd = lambda c: pl.BlockSpec((cps * CHUNK, RET_WIDTH), lambda i: (nb - 1 - i, c))
    state_shape = (RET_HEADS, HEAD_DIM, HEAD_DIM)
    consts = [tabs["kdf"], tabs["kdb"], tabs["cdf"], tabs["cdb"]]
    in_specs = [fwd(1), fwd(2), bwd(1), bwd(2)] + [_layer_spec(a, l) for a in consts]
    in_specs += [_const_spec(sf0), _const_spec(sb0)]
    all_shape = jax.ShapeDtypeStruct((t // CHUNK,) + state_shape, BF16)
    fin_shape = jax.ShapeDtypeStruct(state_shape, F32)
    return pl.pallas_call(
        _state_kernel,
        grid=(nb,),
        in_specs=in_specs,
        out_specs=[
            pl.BlockSpec((cps,) + state_shape, lambda i: (i, 0, 0, 0)),
            pl.BlockSpec((cps,) + state_shape, lambda i: (nb - 1 - i, 0, 0, 0)),
            pl.BlockSpec(state_shape, lambda i: (0, 0, 0)),
            pl.BlockSpec(state_shape, lambda i: (0, 0, 0)),
        ],
        out_shape=[all_shape, all_shape, fin_shape, fin_shape],
        scratch_shapes=[pltpu.VMEM(state_shape, F32), pltpu.VMEM(state_shape, F32)],
        compiler_params=pltpu.CompilerParams(
            dimension_semantics=("arbitrary",), vmem_limit_bytes=VMEM_LIMIT_BYTES),
        name="ret_states",
    )(z, z, z, z, *consts, sf0, sb0)


ZERO_ROWS = 16


def _mixer_kernel(*refs, mod_row):
    zq_ref, zk_ref, zv_ref, zg_ref, zp_ref, zu_ref, zsv_ref, sf_ref, sb_ref = refs[:9]
    zgr_refs, zgp_refs, zgs_refs = (refs[9 + k * N_MERGE:9 + (k + 1) * N_MERGE] for k in range(3))
    (x_ref,
     dmat_ref, qdf_ref, qdb_ref, retg_ref, pmask_ref, pinv_ref, poolw_ref, pscale_ref,
     sgng_ref, sgw_ref, btab_ref, wbr_ref, wbp_ref, wbs_ref, wout_ref,
     gate_ref, n2g_ref, shift2_ref, scale2_ref,
     x1_ref, h2_ref,
     y_scr, ret_cur, pool_cur, sg_cur, ret_prev, pool_prev, sg_prev) = refs[9 + 3 * N_MERGE:]
    tb = x_ref.shape[0]
    nck = tb // CHUNK
    contract_last = (((1,), (1,)), ((), ()))
    mr = slice(mod_row, mod_row + 1)

    @pl.when(pl.program_id(0) == 0)
    def _():
        def zero_rows(r, carry):
            rows = pl.ds(pl.multiple_of(r * ZERO_ROWS, ZERO_ROWS), ZERO_ROWS)
            ret_prev[rows, :] = jnp.zeros((ZERO_ROWS, RET_WIDTH), BF16)
            pool_prev[rows, :] = jnp.zeros((ZERO_ROWS, POOL_WIDTH), BF16)
            sg_prev[rows, :] = jnp.zeros((ZERO_ROWS, SG_WIDTH), BF16)
            return carry

        lax.fori_loop(0, tb // ZERO_ROWS, zero_rows, 0)

    def merge_piece(i):
        cs = slice(i * MERGE_TILE, (i + 1) * MERGE_TILE)
        y = jax.nn.sigmoid(zgr_refs[i][...].astype(F32)) * jnp.dot(
            ret_prev[...], wbr_ref[:, cs], preferred_element_type=F32)
        y = y + jax.nn.sigmoid(zgp_refs[i][...].astype(F32)) * jnp.dot(
            pool_prev[...], wbp_ref[:, cs], preferred_element_type=F32)
        y = y + jax.nn.sigmoid(zgs_refs[i][...].astype(F32)) * jnp.dot(
            sg_prev[...], wbs_ref[:, cs], preferred_element_type=F32)
        y_scr[:, cs] = y.astype(BF16)

    def out_piece(i):
        cs = slice(i * MERGE_TILE, (i + 1) * MERGE_TILE)
        out = jnp.dot(y_scr[...], wout_ref[:, cs], preferred_element_type=F32)
        x1_ref[:, cs] = x_ref[:, cs] + gate_ref[mr, cs] * out

    def ret_unit(c, h):
        rows = slice(c * CHUNK, (c + 1) * CHUNK)
        hs = slice(h * HEAD_DIM, (h + 1) * HEAD_DIM)
        qb = zq_ref[rows, hs]
        scores = lax.dot_general(qb, zk_ref[rows, hs], contract_last, preferred_element_type=F32)
        scores = (scores * dmat_ref[h]).astype(BF16)
        cross = qdf_ref[:, hs] * jnp.dot(qb, sf_ref[c, h], preferred_element_type=F32)
        cross = cross + qdb_ref[:, hs] * jnp.dot(qb, sb_ref[c, h], preferred_element_type=F32)

        def second():
            o = jnp.dot(scores, zv_ref[rows, hs], preferred_element_type=F32) + cross
            g = zg_ref[rows, hs].astype(F32)
            ret = _rms(o) * retg_ref[:, hs] * (g * jax.nn.sigmoid(g))
            ret_cur[rows, hs] = ret.astype(BF16)

        return second

    def pool_unit(gi):
        gs = slice(gi * GROUP_DIM, (gi + 1) * GROUP_DIM)
        pb = zp_ref[:, gs]
        win = jnp.dot(pmask_ref[gi], pb, preferred_element_type=F32)
        pooled = (win * pinv_ref[:, gs] - pb.astype(F32)).astype(BF16)

        def second():
            mapped = jnp.dot(pooled, poolw_ref[gi], preferred_element_type=F32)
            pool_cur[:, gs] = (mapped * pscale_ref[:, gs]).astype(BF16)

        return second

    def sg_unit(c):
        rows = slice(c * CHUNK, (c + 1) * CHUNK)
        sv = (_rms(jax.nn.gelu(zsv_ref[rows, :].astype(F32))) * sgng_ref[...]).astype(BF16)

        def second():
            for gi in range(SG_GROUPS):
                gs = slice(gi * GROUP_DIM, (gi + 1) * GROUP_DIM)
                mixed = jnp.dot(sgw_ref[gi], sv[:, gs], preferred_element_type=F32)
                mixed = mixed + btab_ref[:, gs]
                u = jax.nn.gelu(zu_ref[rows, gs].astype(F32))
                sg_cur[rows, gs] = (u * mixed).astype(BF16)

        return second

    mix_units = [functools.partial(ret_unit, c, h) for c in range(nck) for h in range(RET_HEADS)]
    mix_units += [functools.partial(pool_unit, gi) for gi in range(POOL_GROUPS)]
    mix_units += [functools.partial(sg_unit, c) for c in range(nck)]
    pieces = ([functools.partial(merge_piece, i) for i in range(N_MERGE)]
              + [functools.partial(out_piece, i) for i in range(N_MERGE)])
    per_piece = -(-len(mix_units) // len(pieces))
    pending = [unit() for unit in mix_units[:per_piece]]
    for i, piece in enumerate(pieces):
        piece()
        for second in pending:
            second()
        pending = [unit() for unit in mix_units[(i + 1) * per_piece:(i + 2) * per_piece]]
    assert not pending
    h2_ref[...] = _norm_modulate(x1_ref[...], n2g_ref[...], shift2_ref[mr, :],
                                 scale2_ref[mr, :]).astype(BF16)

    ret_prev[...] = ret_cur[...]
    pool_prev[...] = pool_cur[...]
    sg_prev[...] = sg_cur[...]


def _mixer(z, x, sf_all, sb_all, l, tabs, pool_tabs, p, mods_all, mod_row, tb):
    t = x.shape[0]
    nb = t // tb
    nck = tb // CHUNK
    cur = lambda s: jnp.minimum(s, nb - 1)
    prev = lambda s: jnp.maximum(s - 1, 0)

    def zspec(width, idx, blk):
        return pl.BlockSpec((tb, width), lambda s: (blk(s), idx))

    state_spec = pl.BlockSpec((nck, RET_HEADS, HEAD_DIM, HEAD_DIM), lambda s: (cur(s), 0, 0, 0))
    row_spec = pl.BlockSpec((tb, D_MODEL), lambda s: (prev(s), 0))
    in_specs = [zspec(RET_WIDTH, 0, cur), zspec(RET_WIDTH, 1, cur), zspec(RET_WIDTH, 2, cur),
                zspec(RET_WIDTH, 3, cur), zspec(POOL_WIDTH, _POOL_BLOCK, cur),
                zspec(SG_WIDTH, _POOL_BLOCK + 1, cur), zspec(SG_WIDTH, _POOL_BLOCK + 2, cur),
                state_spec, state_spec]
    in_specs += [zspec(MERGE_TILE, _GATE_BLOCK + k, prev) for k in range(3 * N_MERGE)]
    in_specs.append(row_spec)
    args = [z] * 7 + [sf_all, sb_all] + [z] * (3 * N_MERGE) + [x]
    pmask, pinv = pool_tabs
    layer_consts = [tabs["dmat"], tabs["qdf"], tabs["qdb"], p["ret_norm_g"]]
    in_specs += [_layer_spec(a, l) for a in layer_consts] + [_const_spec(pmask), _const_spec(pinv)]
    args += layer_consts + [pmask, pinv]
    layer_consts = [p["pool_w"], p["pool_scale"], p["sg_norm_g"], p["sg_w"], tabs["btab"],
                    p["w_br"], p["w_bp"], p["w_bs"], p["w_out"]]
    in_specs += [_layer_spec(a, l) for a in layer_consts]
    args += layer_consts
    in_specs += [_mod_spec(l, 2), _gain_spec(l), _mod_spec(l, 3), _mod_spec(l, 4)]
    args += [mods_all, p["norm2_g"], mods_all, mods_all]
    mix_scratch = [pltpu.VMEM((tb, RET_WIDTH), BF16), pltpu.VMEM((tb, POOL_WIDTH), BF16),
                   pltpu.VMEM((tb, SG_WIDTH), BF16)]
    return pl.pallas_call(
        functools.partial(_mixer_kernel, mod_row=mod_row),
        grid=(nb + 1,),
        in_specs=in_specs,
        out_specs=[row_spec, row_spec],
        out_shape=[jax.ShapeDtypeStruct((t, D_MODEL), F32),
                   jax.ShapeDtypeStruct((t, D_MODEL), BF16)],
        scratch_shapes=[pltpu.VMEM((tb, D_MODEL), BF16)] + mix_scratch + mix_scratch,
        compiler_params=pltpu.CompilerParams(
            dimension_semantics=("arbitrary",), vmem_limit_bytes=VMEM_LIMIT_BYTES),
        name="mixer",
    )(*args)


EPILOGUE_ROWS = 32
OUT_PIECE = 512


def _mlp_kernel(*refs, mod_row, final, n_cast):
    refs = list(refs)
    cast_src = cast_dst = ()
    if n_cast:
        cast_dst = refs[-n_cast:]
        refs = refs[:-n_cast]
        n_out = 1 if final else 2
        cast_src = refs[-n_out - n_cast:-n_out]
        refs = refs[:-n_out - n_cast] + refs[-n_out:]
    if final:
        h2_ref, x1_ref, w1_ref, w2_ref, gate_ref, fng_ref, o_ref = refs
    else:
        (h2_ref, x1_ref, w1_ref, w2_ref, gate_ref, ng_ref, nshift_ref, nscale_ref,
         o_ref, hn_ref) = refs
    f = pl.program_id(1)
    mr = slice(mod_row, mod_row + 1)

    def step(first):
        a = jnp.maximum(jnp.dot(h2_ref[...], w1_ref[...], preferred_element_type=F32), 0.0)
        a = (a * a).astype(BF16)
        for c0 in range(0, D_MODEL, OUT_PIECE):
            cs = slice(c0, c0 + OUT_PIECE)
            part = jnp.dot(a, w2_ref[:, cs], preferred_element_type=F32)
            if first:
                o_ref[:, cs] = part
            else:
                o_ref[:, cs] += part
        for src, dst in zip(cast_src, cast_dst):
            dst[...] = src[...].astype(BF16)

    @pl.when(f == 0)
    def _():
        step(True)

    @pl.when(f > 0)
    def _():
        step(False)

    @pl.when(f == pl.num_programs(1) - 1)
    def _():
        gate = gate_ref[mr, :]
        if final:
            gain = fng_ref[...]
        else:
            gain = ng_ref[...] * (1.0 + nscale_ref[mr, :])
            shift = nshift_ref[mr, :]

        for r in range(o_ref.shape[0] // EPILOGUE_ROWS):
            rows = slice(r * EPILOGUE_ROWS, (r + 1) * EPILOGUE_ROWS)
            x2 = x1_ref[rows, :] + gate * o_ref[rows, :]
            if final:
                o_ref[rows, :] = _rms(x2) * gain
            else:
                o_ref[rows, :] = x2
                hn_ref[rows, :] = (_rms(x2) * gain + shift).astype(BF16)


def _mlp(h2, x1, l, w1, w2, mods_all, mod_row, norm1_g, final_gain, tm, tf, cast_next=()):
    t = x1.shape[0]
    final = final_gain is not None
    n_f = D_FF // tf
    grid = (t // tm, n_f)
    rows = pl.BlockSpec((tm, D_MODEL), lambda i, f: (i, 0))
    resid = pl.BlockSpec((tm, D_MODEL), lambda i, f: (i, 0), pipeline_mode=pl.Buffered(1))
    in_specs = [rows, resid,
                pl.BlockSpec((D_MODEL, tf), lambda i, f: (0, f)),
                pl.BlockSpec((tf, D_MODEL), lambda i, f: (f, 0)),
                _mod_spec(l, 5)]
    args = [h2, x1, w1, w2, mods_all]
    x_shape = jax.ShapeDtypeStruct((t, D_MODEL), F32)
    if final:
        in_specs.append(pl.BlockSpec((1, D_MODEL), lambda i, f: (0, 0)))
        args.append(final_gain)
        out_specs, out_shape = [rows], [x_shape]
    else:
        in_specs += [_gain_spec(l + 1), _mod_spec(l + 1, 0), _mod_spec(l + 1, 1)]
        args += [norm1_g, mods_all, mods_all]
        out_specs = [rows, rows]
        out_shape = [x_shape, jax.ShapeDtypeStruct((t, D_MODEL), BF16)]
    for w in cast_next:
        slab, cols = w.shape[1] // (grid[0] * grid[1]), w.shape[2]
        assert slab * grid[0] * grid[1] == w.shape[1] and slab % (2 * SUBLANES) == 0
        in_specs.append(pl.BlockSpec((None, slab, cols), lambda i, f: (l + 1, i * n_f + f, 0)))
        args.append(w)
        out_specs.append(pl.BlockSpec((slab, cols), lambda i, f: (i * n_f + f, 0)))
        out_shape.append(jax.ShapeDtypeStruct(w.shape[1:], BF16))
    return pl.pallas_call(
        functools.partial(_mlp_kernel, mod_row=mod_row, final=final, n_cast=len(cast_next)),
        grid=grid,
        in_specs=in_specs,
        out_specs=out_specs,
        out_shape=out_shape,
        compiler_params=pltpu.CompilerParams(
            dimension_semantics=("arbitrary", "arbitrary"), vmem_limit_bytes=VMEM_LIMIT_BYTES),
        name="mlp",
    )(*args)


def _rope_tables(t):
    half = HEAD_DIM // 2
    nf = half // 2
    tok = np.arange(t)
    inv = ROPE_THETA ** (-np.arange(nf, dtype=np.float64) / nf)
    lane = np.arange(HEAD_DIM)
    pos = np.where(lane[None, :] < half, (tok // GRID_W)[:, None], (tok % GRID_W)[:, None])
    ang = pos * inv[lane % nf][None, :]
    lower = (lane % half) < nf
    sin = np.sin(ang)
    cos = np.cos(ang).astype(np.float32)
    sin_lo = np.where(lower[None, :], -sin, 0.0).astype(np.float32)
    sin_hi = np.where(lower[None, :], 0.0, sin).astype(np.float32)
    return jnp.asarray(cos), jnp.asarray(sin_lo), jnp.asarray(sin_hi)


def _pool_tables(tb, seg_len):
    pos = np.arange(tb) % seg_len
    base = np.arange(tb) - pos
    col = np.arange(tb)[None, :]
    masks, invs = [], []
    for w in POOL_WINDOWS:
        lo = np.maximum(pos - w // 2, 0)
        hi = np.minimum(pos + w // 2 - 1, seg_len - 1)
        masks.append((col >= (base + lo)[:, None]) & (col <= (base + hi)[:, None]))
        invs.append(np.repeat((1.0 / (hi - lo + 1))[:, None], GROUP_DIM, axis=1))
    pmask = jnp.asarray(np.stack(masks).astype(np.float32), dtype=BF16)
    pinv = jnp.asarray(np.concatenate(invs, axis=1).astype(np.float32))
    return pmask, pinv


def _decay_tables(logit):
    lg = jax.nn.log_sigmoid(logit.astype(F32))
    lgf, lgb = lg[0], lg[1]
    idx = jnp.arange(CHUNK, dtype=F32)
    dist = idx[:, None] - idx[None, :]
    fwd = jnp.exp(lgf[:, None, None] * jnp.maximum(dist, 0.0))
    bwd = jnp.exp(lgb[:, None, None] * jnp.maximum(-dist, 0.0))
    dmat = jnp.where(dist > 0, fwd, jnp.where(dist < 0, bwd, 2.0))

    def lanes(tab):
        return jnp.repeat(tab.T, HEAD_DIM, axis=1)

    return dict(
        dmat=dmat,
        qdf=lanes(jnp.exp(lgf[:, None] * (idx + 1.0)[None])),
        qdb=lanes(jnp.exp(lgb[:, None] * (CHUNK - idx)[None])),
        kdf=lanes(jnp.exp(lgf[:, None] * (CHUNK - 1.0 - idx)[None])),
        kdb=lanes(jnp.exp(lgb[:, None] * idx[None])),
        cdf=jnp.broadcast_to(jnp.exp(lgf * CHUNK)[:, None, None], (RET_HEADS, 1, HEAD_DIM)),
        cdb=jnp.broadcast_to(jnp.exp(lgb * CHUNK)[:, None, None], (RET_HEADS, 1, HEAD_DIM)),
    )


LATENT_ROWS_IN = 1024
LATENT_ROWS_MIX = 256
LATENT_ROWS_MLP = 1024
LATENT_FF_TILE = 512
LATENT_ROWS_NORM = 512
LATENT_STATE_CHUNKS = 4
FF_TILE = 1024


def kernel(x, c, ctx, c_ctx, w_ada, b_ada, norm1_g, w_in, ret_decay_logit, ret_norm_g, pool_w,
           pool_scale, sg_norm_g, sg_w, sg_b, w_br, w_bp, w_bs, w_out, norm2_g, w1, w2, final_norm_g):
    assert x.shape[0] == 1 and ctx.shape[0] == 1
    t = x.shape[1]
    tc = ctx.shape[1]
    xs = x[0]
    xc = ctx[0]

    cond = jnp.stack([c[0], c_ctx])[:, :, None]
    mods_all = _ada(cond, w_ada, b_ada)

    rope = _rope_tables(t)
    pool_lat = _pool_tables(LATENT_ROWS_MIX, GRID_W)
    pool_ctx = _pool_tables(tc, tc)
    zero_state = jnp.zeros((RET_HEADS, HEAD_DIM, HEAD_DIM), F32)

    tabs = jax.vmap(_decay_tables)(ret_decay_logit)
    tabs["btab"] = jnp.repeat(jnp.swapaxes(sg_b, 1, 2), GROUP_DIM, axis=2)
    vec = lambda v: v.reshape(DEPTH, 1, -1)
    p = dict(ret_norm_g=vec(ret_norm_g), pool_w=pool_w.astype(BF16), pool_scale=vec(pool_scale),
             sg_norm_g=vec(sg_norm_g), sg_w=sg_w.astype(BF16),
             w_br=w_br.astype(BF16), w_bp=w_bp.astype(BF16), w_bs=w_bs.astype(BF16),
             w_out=w_out.astype(BF16), norm2_g=vec(norm2_g))
    n1g = vec(norm1_g)
    fng = final_norm_g.reshape(1, -1)
    streamed = (w_in, w1, w2)
    w_in_b, w1_b, w2_b = (w[0].astype(BF16) for w in streamed)

    hc = _norm_mod(xc, 0, n1g, mods_all, ROW_CONTEXT, tc)
    hx = _norm_mod(xs, 0, n1g, mods_all, ROW_LATENT, LATENT_ROWS_NORM)
    for l in range(DEPTH):
        last = l == DEPTH - 1

        zc = _in_proj(hc, w_in_b, None, tc)
        sfc, sbc, sf, sb = _states(zc, l, tabs, zero_state, zero_state, tc // CHUNK)
        if not last:
            xc1, hc2 = _mixer(zc, xc, sfc, sbc, l, tabs, pool_ctx, p, mods_all, ROW_CONTEXT, tc)
            xc, hc = _mlp(hc2, xc1, l, w1_b, w2_b, mods_all, ROW_CONTEXT, n1g, None, tc, FF_TILE)

        zx = _in_proj(hx, w_in_b, rope, LATENT_ROWS_IN)
        sfx, sbx, _, _ = _states(zx, l, tabs, sf, sb, LATENT_STATE_CHUNKS)
        x1, h2 = _mixer(zx, xs, sfx, sbx, l, tabs, pool_lat, p, mods_all, ROW_LATENT, LATENT_ROWS_MIX)
        if last:
            xs, = _mlp(h2, x1, l, w1_b, w2_b, mods_all, ROW_LATENT, n1g, fng,
                       LATENT_ROWS_MLP, LATENT_FF_TILE)
        else:
            xs, hx, w_in_b, w1_b, w2_b = _mlp(h2, x1, l, w1_b, w2_b, mods_all, ROW_LATENT, n1g, None,
                                              LATENT_ROWS_MLP, LATENT_FF_TILE, cast_next=streamed)

    return xs[None]
```

```python
import functools

import numpy as np
import jax
import jax.numpy as jnp
from jax import lax
from jax.experimental import pallas as pl
from jax.experimental.pallas import tpu as pltpu

F32 = jnp.float32
BF16 = jnp.bfloat16

D_MODEL = 2048
DEPTH = 4
GRID_W = 64
EPS = 1e-6
N_MOD = 6

HEAD_DIM = 128
RET_WIDTH = D_MODEL // 2
RET_HEADS = RET_WIDTH // HEAD_DIM
CHUNK = 128
K_SCALE = HEAD_DIM ** -0.5
ROPE_THETA = 10000.0

POOL_WIDTH = D_MODEL // 4
POOL_WINDOWS = (2, 4, 8, 16)
POOL_GROUPS = len(POOL_WINDOWS)
GROUP_DIM = POOL_WIDTH // POOL_GROUPS

SG_WIDTH = D_MODEL // 4
SG_GROUPS = 4

D_FF = 4 * D_MODEL
N_IN = 4 * RET_WIDTH + POOL_WIDTH + 2 * SG_WIDTH + 3 * D_MODEL

SUBLANES = 8
LANES = 128
VMEM_LIMIT_BYTES = 56 * 1024 * 1024

MIX_COLS = 4 * RET_WIDTH + POOL_WIDTH + 2 * SG_WIDTH
HALF_WIDTH = 3 * D_MODEL
Z_WIDTH = 2 * HALF_WIDTH
IN_TILE = Z_WIDTH // 4
N_IN_TILES = Z_WIDTH // IN_TILE
QK_TILE = HALF_WIDTH // IN_TILE
QK_COLS = 2 * RET_WIDTH
SUB_TILE = 512
_HEADS_PER_SUB = SUB_TILE // HEAD_DIM
MERGE_TILE = 512
N_MERGE = D_MODEL // MERGE_TILE
Q_COL, K_COL, V_COL, G_COL = (k * RET_WIDTH for k in range(4))
POOL_COL = 4 * RET_WIDTH
U_COL = POOL_COL + POOL_WIDTH
SV_COL = U_COL + SG_WIDTH


def _cast_w_in(src, dst_ref):
    dst_ref[:, :HALF_WIDTH] = src[:, MIX_COLS:].astype(BF16)
    dst_ref[:, HALF_WIDTH:HALF_WIDTH + MIX_COLS] = src[:, :MIX_COLS].astype(BF16)
    dst_ref[:, HALF_WIDTH + MIX_COLS:] = jnp.zeros((src.shape[0], HALF_WIDTH - MIX_COLS), BF16)

ADA_TILE = 1024
ADA_ROWS = 8
ROW_LATENT, ROW_CONTEXT = 0, 1


def _layer_spec(arr, l):
    nd = arr.ndim - 1
    return pl.BlockSpec((None,) + arr.shape[1:], lambda *_: (l,) + (0,) * nd,
                        pipeline_mode=pl.Buffered(1))


def _const_spec(arr):
    nd = arr.ndim
    return pl.BlockSpec(arr.shape, lambda *_: (0,) * nd, pipeline_mode=pl.Buffered(1))


def _mod_spec(l, k):
    return pl.BlockSpec((None, ADA_ROWS, D_MODEL), lambda *_: (l, 0, k))


def _gain_spec(l):
    return pl.BlockSpec((None, 1, D_MODEL), lambda *_: (l, 0, 0))


def _rms(x):
    return x * lax.rsqrt(jnp.mean(x * x, axis=-1, keepdims=True) + EPS)


def _norm_modulate(x, gain, shift, scale):
    return _rms(x) * gain * (1.0 + scale) + shift


def _ada_kernel(cond_ref, w_ref, b_ref, o_ref, act_scr):
    tn = w_ref.shape[1]

    @pl.when(jnp.logical_and(pl.program_id(0) == 0, pl.program_id(1) == 0))
    def _():
        for r in range(2):
            a = cond_ref[r]
            act_scr[r] = jnp.broadcast_to(a * jax.nn.sigmoid(a), (D_MODEL, LANES))

    def body(g, accs):
        r0 = pl.multiple_of(g * SUBLANES, SUBLANES)
        w = w_ref[pl.ds(r0, SUBLANES), :]
        out = []
        for r in range(2):
            a = jnp.tile(act_scr[r, pl.ds(r0, SUBLANES), :], (1, tn // LANES))
            out.append(accs[r] + w * a)
        return tuple(out)

    zero = jnp.zeros((SUBLANES, tn), F32)
    accs = lax.fori_loop(0, D_MODEL // SUBLANES, body, (zero, zero), unroll=8)
    o_ref[...] = jnp.zeros((ADA_ROWS, tn), F32)
    for r in range(2):
        o_ref[r:r + 1, :] = jnp.sum(accs[r], axis=0, keepdims=True) + b_ref[...]


def _ada(cond, w_ada, b_ada):
    n = N_MOD * D_MODEL
    return pl.pallas_call(
        _ada_kernel,
        grid=(DEPTH, n // ADA_TILE),
        in_specs=[
            pl.BlockSpec((2, D_MODEL, 1), lambda l, j: (0, 0, 0)),
            pl.BlockSpec((None, D_MODEL, ADA_TILE), lambda l, j: (l, 0, j)),
            pl.BlockSpec((None, 1, ADA_TILE), lambda l, j: (l, 0, j)),
        ],
        out_specs=pl.BlockSpec((None, ADA_ROWS, ADA_TILE), lambda l, j: (l, 0, j)),
        out_shape=jax.ShapeDtypeStruct((DEPTH, ADA_ROWS, n), F32),
        scratch_shapes=[pltpu.VMEM((2, D_MODEL, LANES), F32)],
        compiler_params=pltpu.CompilerParams(
            dimension_semantics=("arbitrary", "arbitrary"), vmem_limit_bytes=VMEM_LIMIT_BYTES),
        name="ada",
    )(cond, w_ada, b_ada.reshape(DEPTH, 1, n))


def _norm_kernel(x_ref, g_ref, shift_ref, scale_ref, h_ref, *, mod_row):
    mr = slice(mod_row, mod_row + 1)
    h_ref[...] = _norm_modulate(x_ref[...], g_ref[...], shift_ref[mr, :], scale_ref[mr, :]).astype(BF16)


def _norm_mod(x, l, norm1_g, mods_all, mod_row, tm):
    t = x.shape[0]
    return pl.pallas_call(
        functools.partial(_norm_kernel, mod_row=mod_row),
        grid=(t // tm,),
        in_specs=[pl.BlockSpec((tm, D_MODEL), lambda i: (i, 0)),
                  _gain_spec(l), _mod_spec(l, 0), _mod_spec(l, 1)],
        out_specs=pl.BlockSpec((tm, D_MODEL), lambda i: (i, 0)),
        out_shape=jax.ShapeDtypeStruct((t, D_MODEL), BF16),
        compiler_params=pltpu.CompilerParams(
            dimension_semantics=("arbitrary",), vmem_limit_bytes=VMEM_LIMIT_BYTES),
        name="norm_mod",
    )(x, norm1_g, mods_all, mods_all)


CAST_ROWS = 128


def _cast_w_in_kernel(src_ref, dst_ref):
    _cast_w_in(src_ref[...], dst_ref)


def _cast_first_w_in(w_in):
    return pl.pallas_call(
        _cast_w_in_kernel,
        grid=(D_MODEL // CAST_ROWS,),
        in_specs=[pl.BlockSpec((None, CAST_ROWS, N_IN), lambda i: (0, i, 0))],
        out_specs=pl.BlockSpec((CAST_ROWS, Z_WIDTH), lambda i: (i, 0)),
        out_shape=jax.ShapeDtypeStruct((D_MODEL, Z_WIDTH), BF16),
        compiler_params=pltpu.CompilerParams(
            dimension_semantics=("arbitrary",), vmem_limit_bytes=VMEM_LIMIT_BYTES),
        name="cast_w_in",
    )(w_in)


def _rope(t, cos, sin_lo, sin_hi):
    up = pltpu.roll(t, HEAD_DIM - 32, axis=1)
    down = pltpu.roll(t, 32, axis=1)
    return t * cos + up * sin_lo + down * sin_hi


def _inproj_kernel(*refs, use_rope):
    if use_rope:
        h_ref, w_ref, cos_ref, slo_ref, shi_ref, z_ref = refs
    else:
        h_ref, w_ref, z_ref = refs
    j = pl.program_id(1)

    @pl.when(j == QK_TILE)
    def _():
        for blk in range(QK_COLS // SUB_TILE):
            c0 = blk * SUB_TILE
            acc = jnp.dot(h_ref[...], w_ref[:, c0:c0 + SUB_TILE], preferred_element_type=F32)
            if c0 >= RET_WIDTH:
                acc = acc * K_SCALE
            if use_rope:
                cos, slo, shi = cos_ref[...], slo_ref[...], shi_ref[...]
                for hh in range(_HEADS_PER_SUB):
                    hs = slice(hh * HEAD_DIM, (hh + 1) * HEAD_DIM)
                    z_ref[:, c0 + hh * HEAD_DIM:c0 + (hh + 1) * HEAD_DIM] = (
                        _rope(acc[:, hs], cos, slo, shi).astype(BF16))
            else:
                z_ref[:, c0:c0 + SUB_TILE] = acc.astype(BF16)
        z_ref[:, QK_COLS:] = jnp.dot(h_ref[...], w_ref[:, QK_COLS:],
                                     preferred_element_type=F32).astype(BF16)

    @pl.when(j != QK_TILE)
    def _():
        z_ref[...] = jnp.dot(h_ref[...], w_ref[...], preferred_element_type=F32).astype(BF16)


def _in_proj(h, w_in, rope, tm):
    t = h.shape[0]
    use_rope = rope is not None
    in_specs = [
        pl.BlockSpec((tm, D_MODEL), lambda i, j: (i, 0)),
        pl.BlockSpec((D_MODEL, IN_TILE), lambda i, j: (0, j)),
    ]
    args = [h, w_in]
    if use_rope:
        in_specs += [pl.BlockSpec((tm, HEAD_DIM), lambda i, j: (i, 0))] * 3
        args += list(rope)
    return pl.pallas_call(
        functools.partial(_inproj_kernel, use_rope=use_rope),
        grid=(t // tm, N_IN_TILES),
        in_specs=in_specs,
        out_specs=pl.BlockSpec((tm, IN_TILE), lambda i, j: (i, j)),
        out_shape=jax.ShapeDtypeStruct((t, Z_WIDTH), BF16),
        compiler_params=pltpu.CompilerParams(
            dimension_semantics=("arbitrary", "arbitrary"), vmem_limit_bytes=VMEM_LIMIT_BYTES),
        name="in_proj",
    )(*args)


def _state_kernel(kf_ref, vf_ref, kb_ref, vb_ref, kdf_ref, kdb_ref, cdf_ref, cdb_ref, sf0_ref, sb0_ref,
                  sf_all_ref, sb_all_ref, sf_fin_ref, sb_fin_ref, sf_scr, sb_scr):
    t = pl.program_id(0)
    cps = kf_ref.shape[0] // CHUNK

    @pl.when(t == 0)
    def _():
        sf_scr[...] = sf0_ref[...]
        sb_scr[...] = sb0_ref[...]

    contract_rows = (((0,), (0,)), ((), ()))

    def advance(c, k_ref, v_ref, kd_ref, cd_ref, all_ref, scr):
        rows = slice(c * CHUNK, (c + 1) * CHUNK)
        all_ref[c] = scr[...].astype(BF16)
        for h in range(RET_HEADS):
            hs = slice(h * HEAD_DIM, (h + 1) * HEAD_DIM)
            kd = (k_ref[rows, hs].astype(F32) * kd_ref[:, hs]).astype(BF16)
            kv = lax.dot_general(kd, v_ref[rows, hs], contract_rows, preferred_element_type=F32)
            scr[h] = scr[h] * cd_ref[h] + kv

    for c in range(cps):
        advance(c, kf_ref, vf_ref, kdf_ref, cdf_ref, sf_all_ref, sf_scr)
        advance(cps - 1 - c, kb_ref, vb_ref, kdb_ref, cdb_ref, sb_all_ref, sb_scr)

    @pl.when(t == pl.num_programs(0) - 1)
    def _():
        sf_fin_ref[...] = sf_scr[...]
        sb_fin_ref[...] = sb_scr[...]


def _states(z, l, tabs, sf0, sb0, cps):
    t = z.shape[0]
    nb = t // (cps * CHUNK)
    fwd = lambda c: pl.BlockSpec((cps * CHUNK, RET_WIDTH), lambda i: (i, c))
    bwd = lambda c: pl.BlockSpec((cps * CHUNK, RET_WIDTH), lambda i: (nb - 1 - i, c))
    state_shape = (RET_HEADS, HEAD_DIM, HEAD_DIM)
    consts = [tabs["kdf"], tabs["kdb"], tabs["cdf"], tabs["cdb"]]
    kb, vb = (HALF_WIDTH + K_COL) // RET_WIDTH, (HALF_WIDTH + V_COL) // RET_WIDTH
    in_specs = [fwd(kb), fwd(vb), bwd(kb), bwd(vb)] + [_layer_spec(a, l) for a in consts]
    in_specs += [_const_spec(sf0), _const_spec(sb0)]
    all_shape = jax.ShapeDtypeStruct((t // CHUNK,) + state_shape, BF16)
    fin_shape = jax.ShapeDtypeStruct(state_shape, F32)
    return pl.pallas_call(
        _state_kernel,
        grid=(nb,),
        in_specs=in_specs,
        out_specs=[
            pl.BlockSpec((cps,) + state_shape, lambda i: (i, 0, 0, 0)),
            pl.BlockSpec((cps,) + state_shape, lambda i: (nb - 1 - i, 0, 0, 0)),
            pl.BlockSpec(state_shape, lambda i: (0, 0, 0)),
            pl.BlockSpec(state_shape, lambda i: (0, 0, 0)),
        ],
        out_shape=[all_shape, all_shape, fin_shape, fin_shape],
        scratch_shapes=[pltpu.VMEM(state_shape, F32), pltpu.VMEM(state_shape, F32)],
        compiler_params=pltpu.CompilerParams(
            dimension_semantics=("arbitrary",), vmem_limit_bytes=VMEM_LIMIT_BYTES),
        name="ret_states",
    )(z, z, z, z, *consts, sf0, sb0)


ZERO_ROWS = 16


def _mixer_kernel(*refs, mod_row):
    (zmix_ref, sf_ref, sb_ref, zgate_ref, x_ref,
     dmat_ref, qdf_ref, qdb_ref, retg_ref, pmask_ref, pinv_ref, poolw_ref, pscale_ref,
     sgng_ref, sgw_ref, btab_ref, wbr_ref, wbp_ref, wbs_ref, wout_ref,
     gate_ref, n2g_ref, shift2_ref, scale2_ref,
     x1_ref, h2_ref,
     y_scr, ret_cur, pool_cur, sg_cur, ret_prev, pool_prev, sg_prev) = refs
    tb = x_ref.shape[0]
    nck = tb // CHUNK
    contract_last = (((1,), (1,)), ((), ()))
    mr = slice(mod_row, mod_row + 1)

    @pl.when(pl.program_id(0) == 0)
    def _():
        def zero_rows(r, carry):
            rows = pl.ds(pl.multiple_of(r * ZERO_ROWS, ZERO_ROWS), ZERO_ROWS)
            ret_prev[rows, :] = jnp.zeros((ZERO_ROWS, RET_WIDTH), BF16)
            pool_prev[rows, :] = jnp.zeros((ZERO_ROWS, POOL_WIDTH), BF16)
            sg_prev[rows, :] = jnp.zeros((ZERO_ROWS, SG_WIDTH), BF16)
            return carry

        lax.fori_loop(0, tb // ZERO_ROWS, zero_rows, 0)

    def merge_piece(i):
        cs = slice(i * MERGE_TILE, (i + 1) * MERGE_TILE)
        def gate(branch):
            c0 = branch * D_MODEL + i * MERGE_TILE
            return jax.nn.sigmoid(zgate_ref[:, c0:c0 + MERGE_TILE].astype(F32))

        y = gate(0) * jnp.dot(ret_prev[...], wbr_ref[:, cs], preferred_element_type=F32)
        y = y + gate(1) * jnp.dot(pool_prev[...], wbp_ref[:, cs], preferred_element_type=F32)
        y = y + gate(2) * jnp.dot(sg_prev[...], wbs_ref[:, cs], preferred_element_type=F32)
        y_scr[:, cs] = y.astype(BF16)

    def out_piece(i):
        cs = slice(i * MERGE_TILE, (i + 1) * MERGE_TILE)
        out = jnp.dot(y_scr[...], wout_ref[:, cs], preferred_element_type=F32)
        x1_ref[:, cs] = x_ref[:, cs] + gate_ref[mr, cs] * out

    def ret_unit(c, h):
        rows = slice(c * CHUNK, (c + 1) * CHUNK)
        hs = slice(h * HEAD_DIM, (h + 1) * HEAD_DIM)
        head = lambda col: zmix_ref[rows, col + h * HEAD_DIM:col + (h + 1) * HEAD_DIM]
        qb = head(Q_COL)
        scores = lax.dot_general(qb, head(K_COL), contract_last, preferred_element_type=F32)
        scores = (scores * dmat_ref[h]).astype(BF16)
        cross = qdf_ref[:, hs] * jnp.dot(qb, sf_ref[c, h], preferred_element_type=F32)
        cross = cross + qdb_ref[:, hs] * jnp.dot(qb, sb_ref[c, h], preferred_element_type=F32)

        def second():
            o = jnp.dot(scores, head(V_COL), preferred_element_type=F32) + cross
            g = head(G_COL).astype(F32)
            ret = _rms(o) * retg_ref[:, hs] * (g * jax.nn.sigmoid(g))
            ret_cur[rows, hs] = ret.astype(BF16)

        return second

    def pool_unit(gi):
        gs = slice(gi * GROUP_DIM, (gi + 1) * GROUP_DIM)
        pb = zmix_ref[:, POOL_COL + gi * GROUP_DIM:POOL_COL + (gi + 1) * GROUP_DIM]
        win = jnp.dot(pmask_ref[gi], pb, preferred_element_type=F32)
        pooled = (win * pinv_ref[:, gs] - pb.astype(F32)).astype(BF16)

        def second():
            mapped = jnp.dot(pooled, poolw_ref[gi], preferred_element_type=F32)
            pool_cur[:, gs] = (mapped * pscale_ref[:, gs]).astype(BF16)

        return second

    def sg_unit(c):
        rows = slice(c * CHUNK, (c + 1) * CHUNK)
        sv = zmix_ref[rows, SV_COL:SV_COL + SG_WIDTH].astype(F32)
        sv = (_rms(jax.nn.gelu(sv)) * sgng_ref[...]).astype(BF16)

        def second():
            for gi in range(SG_GROUPS):
                gs = slice(gi * GROUP_DIM, (gi + 1) * GROUP_DIM)
                mixed = jnp.dot(sgw_ref[gi], sv[:, gs], preferred_element_type=F32)
                mixed = mixed + btab_ref[:, gs]
                u = zmix_ref[rows, U_COL + gi * GROUP_DIM:U_COL + (gi + 1) * GROUP_DIM]
                u = jax.nn.gelu(u.astype(F32))
                sg_cur[rows, gs] = (u * mixed).astype(BF16)

        return second

    mix_units = [functools.partial(ret_unit, c, h) for c in range(nck) for h in range(RET_HEADS)]
    mix_units += [functools.partial(pool_unit, gi) for gi in range(POOL_GROUPS)]
    mix_units += [functools.partial(sg_unit, c) for c in range(nck)]
    pieces = ([functools.partial(merge_piece, i) for i in range(N_MERGE)]
              + [functools.partial(out_piece, i) for i in range(N_MERGE)])
    per_piece = -(-len(mix_units) // len(pieces))
    pending = [unit() for unit in mix_units[:per_piece]]
    for i, piece in enumerate(pieces):
        piece()
        for second in pending:
            second()
        pending = [unit() for unit in mix_units[(i + 1) * per_piece:(i + 2) * per_piece]]
    assert not pending
    h2_ref[...] = _norm_modulate(x1_ref[...], n2g_ref[...], shift2_ref[mr, :],
                                 scale2_ref[mr, :]).astype(BF16)

    ret_prev[...] = ret_cur[...]
    pool_prev[...] = pool_cur[...]
    sg_prev[...] = sg_cur[...]


def _mixer(z, x, sf_all, sb_all, l, tabs, pool_tabs, p, mods_all, mod_row, tb):
    t = x.shape[0]
    nb = t // tb
    nck = tb // CHUNK
    cur = lambda s: jnp.minimum(s, nb - 1)
    prev = lambda s: jnp.maximum(s - 1, 0)

    state_spec = pl.BlockSpec((nck, RET_HEADS, HEAD_DIM, HEAD_DIM), lambda s: (cur(s), 0, 0, 0))
    row_spec = pl.BlockSpec((tb, D_MODEL), lambda s: (prev(s), 0))
    in_specs = [pl.BlockSpec((tb, HALF_WIDTH), lambda s: (cur(s), 1)), state_spec, state_spec,
                pl.BlockSpec((tb, HALF_WIDTH), lambda s: (prev(s), 0)), row_spec]
    args = [z, sf_all, sb_all, z, x]
    pmask, pinv = pool_tabs
    layer_consts = [tabs["dmat"], tabs["qdf"], tabs["qdb"], p["ret_norm_g"]]
    in_specs += [_layer_spec(a, l) for a in layer_consts] + [_const_spec(pmask), _const_spec(pinv)]
    args += layer_consts + [pmask, pinv]
    layer_consts = [p["pool_w"], p["pool_scale"], p["sg_norm_g"], p["sg_w"], tabs["btab"],
                    p["w_br"], p["w_bp"], p["w_bs"], p["w_out"]]
    in_specs += [_layer_spec(a, l) for a in layer_consts]
    args += layer_consts
    in_specs += [_mod_spec(l, 2), _gain_spec(l), _mod_spec(l, 3), _mod_spec(l, 4)]
    args += [mods_all, p["norm2_g"], mods_all, mods_all]
    mix_scratch = [pltpu.VMEM((tb, RET_WIDTH), BF16), pltpu.VMEM((tb, POOL_WIDTH), BF16),
                   pltpu.VMEM((tb, SG_WIDTH), BF16)]
    return pl.pallas_call(
        functools.partial(_mixer_kernel, mod_row=mod_row),
        grid=(nb + 1,),
        in_specs=in_specs,
        out_specs=[row_spec, row_spec],
        out_shape=[jax.ShapeDtypeStruct((t, D_MODEL), F32),
                   jax.ShapeDtypeStruct((t, D_MODEL), BF16)],
        scratch_shapes=[pltpu.VMEM((tb, D_MODEL), BF16)] + mix_scratch + mix_scratch,
        compiler_params=pltpu.CompilerParams(
            dimension_semantics=("arbitrary",), vmem_limit_bytes=VMEM_LIMIT_BYTES),
        name="mixer",
    )(*args)


EPILOGUE_ROWS = 32
CAST_EVERY = 2


def _mlp_kernel(*refs, mod_row, final, n_cast):
    refs = list(refs)
    cast_src = cast_dst = ()
    if n_cast:
        cast_dst = refs[-n_cast:]
        refs = refs[:-n_cast]
        n_out = 1 if final else 2
        cast_src = refs[-n_out - n_cast:-n_out]
        refs = refs[:-n_out - n_cast] + refs[-n_out:]
    if final:
        h2_ref, x1_ref, w1_ref, w2_ref, gate_ref, fng_ref, o_ref = refs
    else:
        (h2_ref, x1_ref, w1_ref, w2_ref, gate_ref, ng_ref, nshift_ref, nscale_ref,
         o_ref, hn_ref) = refs
    f = pl.program_id(1)
    mr = slice(mod_row, mod_row + 1)

    def step(first):
        a = jnp.maximum(jnp.dot(h2_ref[...], w1_ref[...], preferred_element_type=F32), 0.0)
        part = jnp.dot((a * a).astype(BF16), w2_ref[...], preferred_element_type=F32)
        if first:
            o_ref[...] = part
        else:
            o_ref[...] += part
        part_id = (pl.program_id(0) * pl.num_programs(1) + f) % CAST_EVERY
        for k, (src, dst) in enumerate(zip(cast_src, cast_dst)):
            share = src.shape[0] // CAST_EVERY
            rows = pl.ds(pl.multiple_of(part_id * share, share), share)
            if k == 0:
                _cast_w_in(src[rows, :], dst.at[rows])
            else:
                dst[rows, :] = src[rows, :].astype(BF16)

    @pl.when(f == 0)
    def _():
        step(True)

    @pl.when(f > 0)
    def _():
        step(False)

    @pl.when(f == pl.num_programs(1) - 1)
    def _():
        gate = gate_ref[mr, :]
        if final:
            gain = fng_ref[...]
        else:
            gain = ng_ref[...] * (1.0 + nscale_ref[mr, :])
            shift = nshift_ref[mr, :]

        for r in range(o_ref.shape[0] // EPILOGUE_ROWS):
            rows = slice(r * EPILOGUE_ROWS, (r + 1) * EPILOGUE_ROWS)
            x2 = x1_ref[rows, :] + gate * o_ref[rows, :]
            if final:
                o_ref[rows, :] = _rms(x2) * gain
            else:
                o_ref[rows, :] = x2
                hn_ref[rows, :] = (_rms(x2) * gain + shift).astype(BF16)


def _mlp(h2, x1, l, w1, w2, mods_all, mod_row, norm1_g, final_gain, tm, tf, cast_next=()):
    t = x1.shape[0]
    final = final_gain is not None
    n_f = D_FF // tf
    grid = (t // tm, n_f)
    rows = pl.BlockSpec((tm, D_MODEL), lambda i, f: (i, 0))
    in_specs = [rows, rows,
                pl.BlockSpec((D_MODEL, tf), lambda i, f: (0, f)),
                pl.BlockSpec((tf, D_MODEL), lambda i, f: (f, 0)),
                _mod_spec(l, 5)]
    args = [h2, x1, w1, w2, mods_all]
    x_shape = jax.ShapeDtypeStruct((t, D_MODEL), F32)
    if final:
        in_specs.append(pl.BlockSpec((1, D_MODEL), lambda i, f: (0, 0)))
        args.append(final_gain)
        out_specs, out_shape = [rows], [x_shape]
    else:
        in_specs += [_gain_spec(l + 1), _mod_spec(l + 1, 0), _mod_spec(l + 1, 1)]
        args += [norm1_g, mods_all, mods_all]
        out_specs = [rows, rows]
        out_shape = [x_shape, jax.ShapeDtypeStruct((t, D_MODEL), BF16)]
    for k, w in enumerate(cast_next):
        n_slabs = grid[0] * grid[1] // CAST_EVERY
        slab, cols = w.shape[1] // n_slabs, w.shape[2]
        assert slab * n_slabs == w.shape[1] and slab % (2 * SUBLANES) == 0
        out_cols = Z_WIDTH if k == 0 else cols
        in_specs.append(pl.BlockSpec((None, slab, cols),
                                     lambda i, f: (l + 1, (i * n_f + f) // CAST_EVERY, 0)))
        args.append(w)
        out_specs.append(pl.BlockSpec((slab, out_cols), lambda i, f: ((i * n_f + f) // CAST_EVERY, 0)))
        out_shape.append(jax.ShapeDtypeStruct((w.shape[1], out_cols), BF16))
    return pl.pallas_call(
        functools.partial(_mlp_kernel, mod_row=mod_row, final=final, n_cast=len(cast_next)),
        grid=grid,
        in_specs=in_specs,
        out_specs=out_specs,
        out_shape=out_shape,
        compiler_params=pltpu.CompilerParams(
            dimension_semantics=("arbitrary", "arbitrary"), vmem_limit_bytes=VMEM_LIMIT_BYTES),
        name="mlp",
    )(*args)


def _rope_tables(t):
    half = HEAD_DIM // 2
    nf = half // 2
    tok = np.arange(t)
    inv = ROPE_THETA ** (-np.arange(nf, dtype=np.float64) / nf)
    lane = np.arange(HEAD_DIM)
    pos = np.where(lane[None, :] < half, (tok // GRID_W)[:, None], (tok % GRID_W)[:, None])
    ang = pos * inv[lane % nf][None, :]
    lower = (lane % half) < nf
    sin = np.sin(ang)
    cos = np.cos(ang).astype(np.float32)
    sin_lo = np.where(lower[None, :], -sin, 0.0).astype(np.float32)
    sin_hi = np.where(lower[None, :], 0.0, sin).astype(np.float32)
    return jnp.asarray(cos), jnp.asarray(sin_lo), jnp.asarray(sin_hi)


def _pool_tables(tb, seg_len):
    pos = np.arange(tb) % seg_len
    base = np.arange(tb) - pos
    col = np.arange(tb)[None, :]
    masks, invs = [], []
    for w in POOL_WINDOWS:
        lo = np.maximum(pos - w // 2, 0)
        hi = np.minimum(pos + w // 2 - 1, seg_len - 1)
        masks.append((col >= (base + lo)[:, None]) & (col <= (base + hi)[:, None]))
        invs.append(np.repeat((1.0 / (hi - lo + 1))[:, None], GROUP_DIM, axis=1))
    pmask = jnp.asarray(np.stack(masks).astype(np.float32), dtype=BF16)
    pinv = jnp.asarray(np.concatenate(invs, axis=1).astype(np.float32))
    return pmask, pinv


def _decay_tables(logit):
    lg = jax.nn.log_sigmoid(logit.astype(F32))
    lgf, lgb = lg[0], lg[1]
    idx = jnp.arange(CHUNK, dtype=F32)
    dist = idx[:, None] - idx[None, :]
    fwd = jnp.exp(lgf[:, None, None] * jnp.maximum(dist, 0.0))
    bwd = jnp.exp(lgb[:, None, None] * jnp.maximum(-dist, 0.0))
    dmat = jnp.where(dist > 0, fwd, jnp.where(dist < 0, bwd, 2.0))

    def lanes(tab):
        return jnp.repeat(tab.T, HEAD_DIM, axis=1)

    return dict(
        dmat=dmat,
        qdf=lanes(jnp.exp(lgf[:, None] * (idx + 1.0)[None])),
        qdb=lanes(jnp.exp(lgb[:, None] * (CHUNK - idx)[None])),
        kdf=lanes(jnp.exp(lgf[:, None] * (CHUNK - 1.0 - idx)[None])),
        kdb=lanes(jnp.exp(lgb[:, None] * idx[None])),
        cdf=jnp.broadcast_to(jnp.exp(lgf * CHUNK)[:, None, None], (RET_HEADS, 1, HEAD_DIM)),
        cdb=jnp.broadcast_to(jnp.exp(lgb * CHUNK)[:, None, None], (RET_HEADS, 1, HEAD_DIM)),
    )


LATENT_ROWS_IN = 1024
LATENT_ROWS_MIX = 256
LATENT_ROWS_MLP = 512
LATENT_FF_TILE = 1024
LATENT_ROWS_NORM = 512
LATENT_STATE_CHUNKS = 4
FF_TILE = 1024


def kernel(x, c, ctx, c_ctx, w_ada, b_ada, norm1_g, w_in, ret_decay_logit, ret_norm_g, pool_w,
           pool_scale, sg_norm_g, sg_w, sg_b, w_br, w_bp, w_bs, w_out, norm2_g, w1, w2, final_norm_g):
    assert x.shape[0] == 1 and ctx.shape[0] == 1
    t = x.shape[1]
    tc = ctx.shape[1]
    xs = x[0]
    xc = ctx[0]

    cond = jnp.stack([c[0], c_ctx])[:, :, None]
    mods_all = _ada(cond, w_ada, b_ada)

    rope = _rope_tables(t)
    pool_lat = _pool_tables(LATENT_ROWS_MIX, GRID_W)
    pool_ctx = _pool_tables(tc, tc)
    zero_state = jnp.zeros((RET_HEADS, HEAD_DIM, HEAD_DIM), F32)

    tabs = jax.vmap(_decay_tables)(ret_decay_logit)
    tabs["btab"] = jnp.repeat(jnp.swapaxes(sg_b, 1, 2), GROUP_DIM, axis=2)
    vec = lambda v: v.reshape(DEPTH, 1, -1)
    p = dict(ret_norm_g=vec(ret_norm_g), pool_w=pool_w.astype(BF16), pool_scale=vec(pool_scale),
             sg_norm_g=vec(sg_norm_g), sg_w=sg_w.astype(BF16),
             w_br=w_br.astype(BF16), w_bp=w_bp.astype(BF16), w_bs=w_bs.astype(BF16),
             w_out=w_out.astype(BF16), norm2_g=vec(norm2_g))
    n1g = vec(norm1_g)
    fng = final_norm_g.reshape(1, -1)
    streamed = (w_in, w1, w2)
    w_in_b = _cast_first_w_in(w_in)
    w1_b, w2_b = w1[0].astype(BF16), w2[0].astype(BF16)

    hc = _norm_mod(xc, 0, n1g, mods_all, ROW_CONTEXT, tc)
    hx = _norm_mod(xs, 0, n1g, mods_all, ROW_LATENT, LATENT_ROWS_NORM)
    for l in range(DEPTH):
        last = l == DEPTH - 1

        zc = _in_proj(hc, w_in_b, None, tc)
        sfc, sbc, sf, sb = _states(zc, l, tabs, zero_state, zero_state, tc // CHUNK)
        if not last:
            xc1, hc2 = _mixer(zc, xc, sfc, sbc, l, tabs, pool_ctx, p, mods_all, ROW_CONTEXT, tc)
            xc, hc = _mlp(hc2, xc1, l, w1_b, w2_b, mods_all, ROW_CONTEXT, n1g, None, tc, FF_TILE)

        zx = _in_proj(hx, w_in_b, rope, LATENT_ROWS_IN)
        sfx, sbx, _, _ = _states(zx, l, tabs, sf, sb, LATENT_STATE_CHUNKS)
        x1, h2 = _mixer(zx, xs, sfx, sbx, l, tabs, pool_lat, p, mods_all, ROW_LATENT, LATENT_ROWS_MIX)
        if last:
            xs, = _mlp(h2, x1, l, w1_b, w2_b, mods_all, ROW_LATENT, n1g, fng,
                       LATENT_ROWS_MLP, LATENT_FF_TILE)
        else:
            xs, hx, w_in_b, w1_b, w2_b = _mlp(h2, x1, l, w1_b, w2_b, mods_all, ROW_LATENT, n1g, None,
                                              LATENT_ROWS_MLP, LATENT_FF_TILE, cast_next=streamed)

    return xs[None]
```

```python
import functools

import numpy as np
import jax
import jax.numpy as jnp
from jax import lax
from jax.experimental import pallas as pl
from jax.experimental.pallas import tpu as pltpu

F32 = jnp.float32
BF16 = jnp.bfloat16

D_MODEL = 2048
DEPTH = 4
GRID_W = 64
EPS = 1e-6
N_MOD = 6

HEAD_DIM = 128
RET_WIDTH = D_MODEL // 2
RET_HEADS = RET_WIDTH // HEAD_DIM
CHUNK = 128
K_SCALE = HEAD_DIM ** -0.5
ROPE_THETA = 10000.0

POOL_WIDTH = D_MODEL // 4
POOL_WINDOWS = (2, 4, 8, 16)
POOL_GROUPS = len(POOL_WINDOWS)
GROUP_DIM = POOL_WIDTH // POOL_GROUPS

SG_WIDTH = D_MODEL // 4
SG_GROUPS = 4

D_FF = 4 * D_MODEL
N_IN = 4 * RET_WIDTH + POOL_WIDTH + 2 * SG_WIDTH + 3 * D_MODEL

SUBLANES = 8
LANES = 128
VMEM_LIMIT_BYTES = 56 * 1024 * 1024

MIX_COLS = 4 * RET_WIDTH + POOL_WIDTH + 2 * SG_WIDTH
HALF_WIDTH = 3 * D_MODEL
Z_WIDTH = 2 * HALF_WIDTH
IN_TILE = Z_WIDTH // 4
N_IN_TILES = Z_WIDTH // IN_TILE
QK_TILE = HALF_WIDTH // IN_TILE
QK_COLS = 2 * RET_WIDTH
SUB_TILE = 512
_HEADS_PER_SUB = SUB_TILE // HEAD_DIM
MERGE_TILE = 512
N_MERGE = D_MODEL // MERGE_TILE
Q_COL, K_COL, V_COL, G_COL = (k * RET_WIDTH for k in range(4))
POOL_COL = 4 * RET_WIDTH
U_COL = POOL_COL + POOL_WIDTH
SV_COL = U_COL + SG_WIDTH


def _cast_w_in(src, dst_ref):
    dst_ref[:, :HALF_WIDTH] = src[:, MIX_COLS:].astype(BF16)
    dst_ref[:, HALF_WIDTH:HALF_WIDTH + MIX_COLS] = src[:, :MIX_COLS].astype(BF16)
    dst_ref[:, HALF_WIDTH + MIX_COLS:] = jnp.zeros((src.shape[0], HALF_WIDTH - MIX_COLS), BF16)

ADA_TILE = 1024
ADA_ROWS = 8
ROW_LATENT, ROW_CONTEXT = 0, 1


def _layer_spec(arr, l):
    nd = arr.ndim - 1
    return pl.BlockSpec((None,) + arr.shape[1:], lambda *_: (l,) + (0,) * nd,
                        pipeline_mode=pl.Buffered(1))


def _const_spec(arr):
    nd = arr.ndim
    return pl.BlockSpec(arr.shape, lambda *_: (0,) * nd, pipeline_mode=pl.Buffered(1))


def _mod_spec(l, k):
    return pl.BlockSpec((None, ADA_ROWS, D_MODEL), lambda *_: (l, 0, k))


def _gain_spec(l):
    return pl.BlockSpec((None, 1, D_MODEL), lambda *_: (l, 0, 0))


def _rms(x):
    return x * lax.rsqrt(jnp.mean(x * x, axis=-1, keepdims=True) + EPS)


def _sigmoid(x):
    return 0.5 * jnp.tanh(0.5 * x) + 0.5


def _norm_modulate(x, gain, shift, scale):
    return _rms(x) * gain * (1.0 + scale) + shift


def _ada_kernel(cond_ref, w_ref, b_ref, o_ref, act_scr):
    tn = w_ref.shape[1]

    @pl.when(jnp.logical_and(pl.program_id(0) == 0, pl.program_id(1) == 0))
    def _():
        for r in range(2):
            a = cond_ref[r]
            act_scr[r] = jnp.broadcast_to(a * jax.nn.sigmoid(a), (D_MODEL, LANES))

    def body(g, accs):
        r0 = pl.multiple_of(g * SUBLANES, SUBLANES)
        w = w_ref[pl.ds(r0, SUBLANES), :]
        out = []
        for r in range(2):
            a = jnp.tile(act_scr[r, pl.ds(r0, SUBLANES), :], (1, tn // LANES))
            out.append(accs[r] + w * a)
        return tuple(out)

    zero = jnp.zeros((SUBLANES, tn), F32)
    accs = lax.fori_loop(0, D_MODEL // SUBLANES, body, (zero, zero), unroll=8)
    o_ref[...] = jnp.zeros((ADA_ROWS, tn), F32)
    for r in range(2):
        o_ref[r:r + 1, :] = jnp.sum(accs[r], axis=0, keepdims=True) + b_ref[...]


def _ada(cond, w_ada, b_ada):
    n = N_MOD * D_MODEL
    return pl.pallas_call(
        _ada_kernel,
        grid=(DEPTH, n // ADA_TILE),
        in_specs=[
            pl.BlockSpec((2, D_MODEL, 1), lambda l, j: (0, 0, 0)),
            pl.BlockSpec((None, D_MODEL, ADA_TILE), lambda l, j: (l, 0, j)),
            pl.BlockSpec((None, 1, ADA_TILE), lambda l, j: (l, 0, j)),
        ],
        out_specs=pl.BlockSpec((None, ADA_ROWS, ADA_TILE), lambda l, j: (l, 0, j)),
        out_shape=jax.ShapeDtypeStruct((DEPTH, ADA_ROWS, n), F32),
        scratch_shapes=[pltpu.VMEM((2, D_MODEL, LANES), F32)],
        compiler_params=pltpu.CompilerParams(
            dimension_semantics=("arbitrary", "arbitrary"), vmem_limit_bytes=VMEM_LIMIT_BYTES),
        name="ada",
    )(cond, w_ada, b_ada.reshape(DEPTH, 1, n))


def _norm_kernel(x_ref, g_ref, shift_ref, scale_ref, h_ref, *, mod_row):
    mr = slice(mod_row, mod_row + 1)
    h_ref[...] = _norm_modulate(x_ref[...], g_ref[...], shift_ref[mr, :], scale_ref[mr, :]).astype(BF16)


def _norm_mod(x, l, norm1_g, mods_all, mod_row, tm):
    t = x.shape[0]
    return pl.pallas_call(
        functools.partial(_norm_kernel, mod_row=mod_row),
        grid=(t // tm,),
        in_specs=[pl.BlockSpec((tm, D_MODEL), lambda i: (i, 0)),
                  _gain_spec(l), _mod_spec(l, 0), _mod_spec(l, 1)],
        out_specs=pl.BlockSpec((tm, D_MODEL), lambda i: (i, 0)),
        out_shape=jax.ShapeDtypeStruct((t, D_MODEL), BF16),
        compiler_params=pltpu.CompilerParams(
            dimension_semantics=("arbitrary",), vmem_limit_bytes=VMEM_LIMIT_BYTES),
        name="norm_mod",
    )(x, norm1_g, mods_all, mods_all)


CAST_ROWS = 128


def _cast_w_in_kernel(src_ref, dst_ref):
    _cast_w_in(src_ref[...], dst_ref)


def _cast_first_w_in(w_in):
    return pl.pallas_call(
        _cast_w_in_kernel,
        grid=(D_MODEL // CAST_ROWS,),
        in_specs=[pl.BlockSpec((None, CAST_ROWS, N_IN), lambda i: (0, i, 0))],
        out_specs=pl.BlockSpec((CAST_ROWS, Z_WIDTH), lambda i: (i, 0)),
        out_shape=jax.ShapeDtypeStruct((D_MODEL, Z_WIDTH), BF16),
        compiler_params=pltpu.CompilerParams(
            dimension_semantics=("arbitrary",), vmem_limit_bytes=VMEM_LIMIT_BYTES),
        name="cast_w_in",
    )(w_in)


def _rope(t, cos, sin_lo, sin_hi):
    up = pltpu.roll(t, HEAD_DIM - 32, axis=1)
    down = pltpu.roll(t, 32, axis=1)
    return t * cos + up * sin_lo + down * sin_hi


def _inproj_kernel(*refs, use_rope):
    if use_rope:
        h_ref, w_ref, cos_ref, slo_ref, shi_ref, z_ref = refs
    else:
        h_ref, w_ref, z_ref = refs
    j = pl.program_id(1)

    @pl.when(j == QK_TILE)
    def _():
        for blk in range(QK_COLS // SUB_TILE):
            c0 = blk * SUB_TILE
            acc = jnp.dot(h_ref[...], w_ref[:, c0:c0 + SUB_TILE], preferred_element_type=F32)
            if c0 >= RET_WIDTH:
                acc = acc * K_SCALE
            if use_rope:
                cos, slo, shi = cos_ref[...], slo_ref[...], shi_ref[...]
                for hh in range(_HEADS_PER_SUB):
                    hs = slice(hh * HEAD_DIM, (hh + 1) * HEAD_DIM)
                    z_ref[:, c0 + hh * HEAD_DIM:c0 + (hh + 1) * HEAD_DIM] = (
                        _rope(acc[:, hs], cos, slo, shi).astype(BF16))
            else:
                z_ref[:, c0:c0 + SUB_TILE] = acc.astype(BF16)
        z_ref[:, QK_COLS:] = jnp.dot(h_ref[...], w_ref[:, QK_COLS:],
                                     preferred_element_type=F32).astype(BF16)

    @pl.when(j != QK_TILE)
    def _():
        z_ref[...] = jnp.dot(h_ref[...], w_ref[...], preferred_element_type=F32).astype(BF16)


def _in_proj(h, w_in, rope, tm):
    t = h.shape[0]
    use_rope = rope is not None
    in_specs = [
        pl.BlockSpec((tm, D_MODEL), lambda i, j: (i, 0)),
        pl.BlockSpec((D_MODEL, IN_TILE), lambda i, j: (0, j)),
    ]
    args = [h, w_in]
    if use_rope:
        in_specs += [pl.BlockSpec((tm, HEAD_DIM), lambda i, j: (i, 0))] * 3
        args += list(rope)
    return pl.pallas_call(
        functools.partial(_inproj_kernel, use_rope=use_rope),
        grid=(t // tm, N_IN_TILES),
        in_specs=in_specs,
        out_specs=pl.BlockSpec((tm, IN_TILE), lambda i, j: (i, j)),
        out_shape=jax.ShapeDtypeStruct((t, Z_WIDTH), BF16),
        compiler_params=pltpu.CompilerParams(
            dimension_semantics=("arbitrary", "arbitrary"), vmem_limit_bytes=VMEM_LIMIT_BYTES),
        name="in_proj",
    )(*args)


def _state_kernel(k_ref, v_ref, kdf_ref, kdb_ref, cdf_ref, cdb_ref, sf0_ref, sb0_ref,
                  sf_all_ref, sb_all_ref, sf_fin_ref, sb_fin_ref, sf_scr, sb_scr):
    t = pl.program_id(0)
    nb = pl.num_programs(0)
    cps = sf_all_ref.shape[0]

    @pl.when(t == 0)
    def _():
        sf_scr[...] = sf0_ref[...]
        sb_scr[...] = sb0_ref[...]

    contract_rows = (((0,), (0,)), ((), ()))

    def advance(block, c, kd_ref, cd_ref, all_ref, scr):
        rows = pl.ds(pl.multiple_of((block * cps + c) * CHUNK, CHUNK), CHUNK)
        all_ref[c] = scr[...].astype(BF16)
        for h in range(RET_HEADS):
            hs = slice(h * HEAD_DIM, (h + 1) * HEAD_DIM)
            kd = (k_ref[rows, hs].astype(F32) * kd_ref[:, hs]).astype(BF16)
            kv = lax.dot_general(kd, v_ref[rows, hs], contract_rows, preferred_element_type=F32)
            scr[h] = scr[h] * cd_ref[h] + kv

    for c in range(cps):
        advance(t, c, kdf_ref, cdf_ref, sf_all_ref, sf_scr)
        advance(nb - 1 - t, cps - 1 - c, kdb_ref, cdb_ref, sb_all_ref, sb_scr)

    @pl.when(t == pl.num_programs(0) - 1)
    def _():
        sf_fin_ref[...] = sf_scr[...]
        sb_fin_ref[...] = sb_scr[...]


def _states(z, l, tabs, sf0, sb0, cps):
    t = z.shape[0]
    nb = t // (cps * CHUNK)
    whole = lambda c: pl.BlockSpec((t, RET_WIDTH), lambda i: (0, c), pipeline_mode=pl.Buffered(1))
    state_shape = (RET_HEADS, HEAD_DIM, HEAD_DIM)
    consts = [tabs["kdf"], tabs["kdb"], tabs["cdf"], tabs["cdb"]]
    kb, vb = (HALF_WIDTH + K_COL) // RET_WIDTH, (HALF_WIDTH + V_COL) // RET_WIDTH
    in_specs = [whole(kb), whole(vb)] + [_layer_spec(a, l) for a in consts]
    in_specs += [_const_spec(sf0), _const_spec(sb0)]
    all_shape = jax.ShapeDtypeStruct((t // CHUNK,) + state_shape, BF16)
    fin_shape = jax.ShapeDtypeStruct(state_shape, F32)
    return pl.pallas_call(
        _state_kernel,
        grid=(nb,),
        in_specs=in_specs,
        out_specs=[
            pl.BlockSpec((cps,) + state_shape, lambda i: (i, 0, 0, 0)),
            pl.BlockSpec((cps,) + state_shape, lambda i: (nb - 1 - i, 0, 0, 0)),
            pl.BlockSpec(state_shape, lambda i: (0, 0, 0)),
            pl.BlockSpec(state_shape, lambda i: (0, 0, 0)),
        ],
        out_shape=[all_shape, all_shape, fin_shape, fin_shape],
        scratch_shapes=[pltpu.VMEM(state_shape, F32), pltpu.VMEM(state_shape, F32)],
        compiler_params=pltpu.CompilerParams(
            dimension_semantics=("arbitrary",), vmem_limit_bytes=VMEM_LIMIT_BYTES),
        name="ret_states",
    )(z, z, *consts, sf0, sb0)


ZERO_ROWS = 16


def _mixer_kernel(*refs, mod_row):
    (zmix_ref, sf_ref, sb_ref, zgate_ref, x_ref,
     dmat_ref, qdf_ref, qdb_ref, retg_ref, pmask_ref, pinv_ref, poolw_ref, pscale_ref,
     sgng_ref, sgw_ref, btab_ref, wbr_ref, wbp_ref, wbs_ref, wout_ref,
     gate_ref, n2g_ref, shift2_ref, scale2_ref,
     x1_ref, h2_ref,
     y_scr, ret_cur, pool_cur, sg_cur, ret_prev, pool_prev, sg_prev) = refs
    tb = x_ref.shape[0]
    nck = tb // CHUNK
    contract_last = (((1,), (1,)), ((), ()))
    mr = slice(mod_row, mod_row + 1)

    @pl.when(pl.program_id(0) == 0)
    def _():
        def zero_rows(r, carry):
            rows = pl.ds(pl.multiple_of(r * ZERO_ROWS, ZERO_ROWS), ZERO_ROWS)
            ret_prev[rows, :] = jnp.zeros((ZERO_ROWS, RET_WIDTH), BF16)
            pool_prev[rows, :] = jnp.zeros((ZERO_ROWS, POOL_WIDTH), BF16)
            sg_prev[rows, :] = jnp.zeros((ZERO_ROWS, SG_WIDTH), BF16)
            return carry

        lax.fori_loop(0, tb // ZERO_ROWS, zero_rows, 0)

    def merge_piece(i):
        cs = slice(i * MERGE_TILE, (i + 1) * MERGE_TILE)
        def gate(branch):
            c0 = branch * D_MODEL + i * MERGE_TILE
            return _sigmoid(zgate_ref[:, c0:c0 + MERGE_TILE].astype(F32))

        y = gate(0) * jnp.dot(ret_prev[...], wbr_ref[:, cs], preferred_element_type=F32)
        y = y + gate(1) * jnp.dot(pool_prev[...], wbp_ref[:, cs], preferred_element_type=F32)
        y = y + gate(2) * jnp.dot(sg_prev[...], wbs_ref[:, cs], preferred_element_type=F32)
        y_scr[:, cs] = y.astype(BF16)

    def out_piece(i):
        cs = slice(i * MERGE_TILE, (i + 1) * MERGE_TILE)
        out = jnp.dot(y_scr[...], wout_ref[:, cs], preferred_element_type=F32)
        x1_ref[:, cs] = x_ref[:, cs] + gate_ref[mr, cs] * out

    def ret_unit(c, h):
        rows = slice(c * CHUNK, (c + 1) * CHUNK)
        hs = slice(h * HEAD_DIM, (h + 1) * HEAD_DIM)
        head = lambda col: zmix_ref[rows, col + h * HEAD_DIM:col + (h + 1) * HEAD_DIM]
        qb = head(Q_COL)
        scores = lax.dot_general(qb, head(K_COL), contract_last, preferred_element_type=F32)
        scores = (scores * dmat_ref[h]).astype(BF16)
        cross = qdf_ref[:, hs] * jnp.dot(qb, sf_ref[c, h], preferred_element_type=F32)
        cross = cross + qdb_ref[:, hs] * jnp.dot(qb, sb_ref[c, h], preferred_element_type=F32)

        def second():
            o = jnp.dot(scores, head(V_COL), preferred_element_type=F32) + cross
            g = head(G_COL).astype(F32)
            ret = _rms(o) * retg_ref[:, hs] * (g * _sigmoid(g))
            ret_cur[rows, hs] = ret.astype(BF16)

        return second

    def pool_unit(gi):
        gs = slice(gi * GROUP_DIM, (gi + 1) * GROUP_DIM)
        pb = zmix_ref[:, POOL_COL + gi * GROUP_DIM:POOL_COL + (gi + 1) * GROUP_DIM]
        win = jnp.dot(pmask_ref[gi], pb, preferred_element_type=F32)
        pooled = (win * pinv_ref[:, gs] - pb.astype(F32)).astype(BF16)

        def second():
            mapped = jnp.dot(pooled, poolw_ref[gi], preferred_element_type=F32)
            pool_cur[:, gs] = (mapped * pscale_ref[:, gs]).astype(BF16)

        return second

    def sg_unit(c):
        rows = slice(c * CHUNK, (c + 1) * CHUNK)
        sv = zmix_ref[rows, SV_COL:SV_COL + SG_WIDTH].astype(F32)
        sv = (_rms(jax.nn.gelu(sv)) * sgng_ref[...]).astype(BF16)

        def second():
            for gi in range(SG_GROUPS):
                gs = slice(gi * GROUP_DIM, (gi + 1) * GROUP_DIM)
                mixed = jnp.dot(sgw_ref[gi], sv[:, gs], preferred_element_type=F32)
                mixed = mixed + btab_ref[:, gs]
                u = zmix_ref[rows, U_COL + gi * GROUP_DIM:U_COL + (gi + 1) * GROUP_DIM]
                u = jax.nn.gelu(u.astype(F32))
                sg_cur[rows, gs] = (u * mixed).astype(BF16)

        return second

    mix_units = [functools.partial(ret_unit, c, h) for c in range(nck) for h in range(RET_HEADS)]
    mix_units += [functools.partial(pool_unit, gi) for gi in range(POOL_GROUPS)]
    mix_units += [functools.partial(sg_unit, c) for c in range(nck)]
    pieces = ([functools.partial(merge_piece, i) for i in range(N_MERGE)]
              + [functools.partial(out_piece, i) for i in range(N_MERGE)])
    per_piece = -(-len(mix_units) // len(pieces))
    pending = [unit() for unit in mix_units[:per_piece]]
    for i, piece in enumerate(pieces):
        piece()
        for second in pending:
            second()
        pending = [unit() for unit in mix_units[(i + 1) * per_piece:(i + 2) * per_piece]]
    assert not pending
    h2_ref[...] = _norm_modulate(x1_ref[...], n2g_ref[...], shift2_ref[mr, :],
                                 scale2_ref[mr, :]).astype(BF16)

    ret_prev[...] = ret_cur[...]
    pool_prev[...] = pool_cur[...]
    sg_prev[...] = sg_cur[...]


def _mixer(z, x, sf_all, sb_all, l, tabs, pool_tabs, p, mods_all, mod_row, tb):
    t = x.shape[0]
    nb = t // tb
    nck = tb // CHUNK
    cur = lambda s: jnp.minimum(s, nb - 1)
    prev = lambda s: jnp.maximum(s - 1, 0)

    state_spec = pl.BlockSpec((nck, RET_HEADS, HEAD_DIM, HEAD_DIM), lambda s: (cur(s), 0, 0, 0))
    row_spec = pl.BlockSpec((tb, D_MODEL), lambda s: (prev(s), 0))
    in_specs = [pl.BlockSpec((tb, HALF_WIDTH), lambda s: (cur(s), 1)), state_spec, state_spec,
                pl.BlockSpec((tb, HALF_WIDTH), lambda s: (prev(s), 0)), row_spec]
    args = [z, sf_all, sb_all, z, x]
    pmask, pinv = pool_tabs
    layer_consts = [tabs["dmat"], tabs["qdf"], tabs["qdb"], p["ret_norm_g"]]
    in_specs += [_layer_spec(a, l) for a in layer_consts] + [_const_spec(pmask), _const_spec(pinv)]
    args += layer_consts + [pmask, pinv]
    layer_consts = [p["pool_w"], p["pool_scale"], p["sg_norm_g"], p["sg_w"], tabs["btab"],
                    p["w_br"], p["w_bp"], p["w_bs"], p["w_out"]]
    in_specs += [_layer_spec(a, l) for a in layer_consts]
    args += layer_consts
    in_specs += [_mod_spec(l, 2), _gain_spec(l), _mod_spec(l, 3), _mod_spec(l, 4)]
    args += [mods_all, p["norm2_g"], mods_all, mods_all]
    mix_scratch = [pltpu.VMEM((tb, RET_WIDTH), BF16), pltpu.VMEM((tb, POOL_WIDTH), BF16),
                   pltpu.VMEM((tb, SG_WIDTH), BF16)]
    return pl.pallas_call(
        functools.partial(_mixer_kernel, mod_row=mod_row),
        grid=(nb + 1,),
        in_specs=in_specs,
        out_specs=[row_spec, row_spec],
        out_shape=[jax.ShapeDtypeStruct((t, D_MODEL), F32),
                   jax.ShapeDtypeStruct((t, D_MODEL), BF16)],
        scratch_shapes=[pltpu.VMEM((tb, D_MODEL), BF16)] + mix_scratch + mix_scratch,
        compiler_params=pltpu.CompilerParams(
            dimension_semantics=("arbitrary",), vmem_limit_bytes=VMEM_LIMIT_BYTES),
        name="mixer",
    )(*args)


EPILOGUE_ROWS = 32
CAST_EVERY = 2


def _mlp_kernel(*refs, mod_row, final, n_cast):
    refs = list(refs)
    cast_src = cast_dst = ()
    if n_cast:
        cast_dst = refs[-n_cast:]
        refs = refs[:-n_cast]
        n_out = 1 if final else 2
        cast_src = refs[-n_out - n_cast:-n_out]
        refs = refs[:-n_out - n_cast] + refs[-n_out:]
    if final:
        h2_ref, x1_ref, w1_ref, w2_ref, gate_ref, fng_ref, o_ref = refs
    else:
        (h2_ref, x1_ref, w1_ref, w2_ref, gate_ref, ng_ref, nshift_ref, nscale_ref,
         o_ref, hn_ref) = refs
    f = pl.program_id(1)
    mr = slice(mod_row, mod_row + 1)

    def step(first):
        a = jnp.maximum(jnp.dot(h2_ref[...], w1_ref[...], preferred_element_type=F32), 0.0)
        part = jnp.dot((a * a).astype(BF16), w2_ref[...], preferred_element_type=F32)
        if first:
            o_ref[...] = part
        else:
            o_ref[...] += part
        part_id = (pl.program_id(0) * pl.num_programs(1) + f) % CAST_EVERY
        for k, (src, dst) in enumerate(zip(cast_src, cast_dst)):
            share = src.shape[0] // CAST_EVERY
            rows = pl.ds(pl.multiple_of(part_id * share, share), share)
            if k == 0:
                _cast_w_in(src[rows, :], dst.at[rows])
            else:
                dst[rows, :] = src[rows, :].astype(BF16)

    @pl.when(f == 0)
    def _():
        step(True)

    @pl.when(f > 0)
    def _():
        step(False)

    @pl.when(f == pl.num_programs(1) - 1)
    def _():
        gate = gate_ref[mr, :]
        if final:
            gain = fng_ref[...]
        else:
            gain = ng_ref[...] * (1.0 + nscale_ref[mr, :])
            shift = nshift_ref[mr, :]

        for r in range(o_ref.shape[0] // EPILOGUE_ROWS):
            rows = slice(r * EPILOGUE_ROWS, (r + 1) * EPILOGUE_ROWS)
            x2 = x1_ref[rows, :] + gate * o_ref[rows, :]
            if final:
                o_ref[rows, :] = _rms(x2) * gain
            else:
                o_ref[rows, :] = x2
                hn_ref[rows, :] = (_rms(x2) * gain + shift).astype(BF16)


def _mlp(h2, x1, l, w1, w2, mods_all, mod_row, norm1_g, final_gain, tm, tf, cast_next=()):
    t = x1.shape[0]
    final = final_gain is not None
    n_f = D_FF // tf
    grid = (t // tm, n_f)
    rows = pl.BlockSpec((tm, D_MODEL), lambda i, f: (i, 0))
    in_specs = [rows, rows,
                pl.BlockSpec((D_MODEL, tf), lambda i, f: (0, f)),
                pl.BlockSpec((tf, D_MODEL), lambda i, f: (f, 0)),
                _mod_spec(l, 5)]
    args = [h2, x1, w1, w2, mods_all]
    x_shape = jax.ShapeDtypeStruct((t, D_MODEL), F32)
    if final:
        in_specs.append(pl.BlockSpec((1, D_MODEL), lambda i, f: (0, 0)))
        args.append(final_gain)
        out_specs, out_shape = [rows], [x_shape]
    else:
        in_specs += [_gain_spec(l + 1), _mod_spec(l + 1, 0), _mod_spec(l + 1, 1)]
        args += [norm1_g, mods_all, mods_all]
        out_specs = [rows, rows]
        out_shape = [x_shape, jax.ShapeDtypeStruct((t, D_MODEL), BF16)]
    for k, w in enumerate(cast_next):
        n_slabs = grid[0] * grid[1] // CAST_EVERY
        slab, cols = w.shape[1] // n_slabs, w.shape[2]
        assert slab * n_slabs == w.shape[1] and slab % (2 * SUBLANES) == 0
        out_cols = Z_WIDTH if k == 0 else cols
        in_specs.append(pl.BlockSpec((None, slab, cols),
                                     lambda i, f: (l + 1, (i * n_f + f) // CAST_EVERY, 0)))
        args.append(w)
        out_specs.append(pl.BlockSpec((slab, out_cols), lambda i, f: ((i * n_f + f) // CAST_EVERY, 0)))
        out_shape.append(jax.ShapeDtypeStruct((w.shape[1], out_cols), BF16))
    return pl.pallas_call(
        functools.partial(_mlp_kernel, mod_row=mod_row, final=final, n_cast=len(cast_next)),
        grid=grid,
        in_specs=in_specs,
        out_specs=out_specs,
        out_shape=out_shape,
        compiler_params=pltpu.CompilerParams(
            dimension_semantics=("arbitrary", "arbitrary"), vmem_limit_bytes=VMEM_LIMIT_BYTES),
        name="mlp",
    )(*args)


def _rope_tables(t):
    half = HEAD_DIM // 2
    nf = half // 2
    tok = np.arange(t)
    inv = ROPE_THETA ** (-np.arange(nf, dtype=np.float64) / nf)
    lane = np.arange(HEAD_DIM)
    pos = np.where(lane[None, :] < half, (tok // GRID_W)[:, None], (tok % GRID_W)[:, None])
    ang = pos * inv[lane % nf][None, :]
    lower = (lane % half) < nf
    sin = np.sin(ang)
    cos = np.cos(ang).astype(np.float32)
    sin_lo = np.where(lower[None, :], -sin, 0.0).astype(np.float32)
    sin_hi = np.where(lower[None, :], 0.0, sin).astype(np.float32)
    return jnp.asarray(cos), jnp.asarray(sin_lo), jnp.asarray(sin_hi)


def _pool_tables(tb, seg_len):
    pos = np.arange(tb) % seg_len
    base = np.arange(tb) - pos
    col = np.arange(tb)[None, :]
    masks, invs = [], []
    for w in POOL_WINDOWS:
        lo = np.maximum(pos - w // 2, 0)
        hi = np.minimum(pos + w // 2 - 1, seg_len - 1)
        masks.append((col >= (base + lo)[:, None]) & (col <= (base + hi)[:, None]))
        invs.append(np.repeat((1.0 / (hi - lo + 1))[:, None], GROUP_DIM, axis=1))
    pmask = jnp.asarray(np.stack(masks).astype(np.float32), dtype=BF16)
    pinv = jnp.asarray(np.concatenate(invs, axis=1).astype(np.float32))
    return pmask, pinv


def _decay_tables(logit):
    lg = jax.nn.log_sigmoid(logit.astype(F32))
    lgf, lgb = lg[0], lg[1]
    idx = jnp.arange(CHUNK, dtype=F32)
    dist = idx[:, None] - idx[None, :]
    fwd = jnp.exp(lgf[:, None, None] * jnp.maximum(dist, 0.0))
    bwd = jnp.exp(lgb[:, None, None] * jnp.maximum(-dist, 0.0))
    dmat = jnp.where(dist > 0, fwd, jnp.where(dist < 0, bwd, 2.0))

    def lanes(tab):
        return jnp.repeat(tab.T, HEAD_DIM, axis=1)

    return dict(
        dmat=dmat,
        qdf=lanes(jnp.exp(lgf[:, None] * (idx + 1.0)[None])),
        qdb=lanes(jnp.exp(lgb[:, None] * (CHUNK - idx)[None])),
        kdf=lanes(jnp.exp(lgf[:, None] * (CHUNK - 1.0 - idx)[None])),
        kdb=lanes(jnp.exp(lgb[:, None] * idx[None])),
        cdf=jnp.broadcast_to(jnp.exp(lgf * CHUNK)[:, None, None], (RET_HEADS, 1, HEAD_DIM)),
        cdb=jnp.broadcast_to(jnp.exp(lgb * CHUNK)[:, None, None], (RET_HEADS, 1, HEAD_DIM)),
    )


LATENT_ROWS_IN = 1024
LATENT_ROWS_MIX = 256
LATENT_ROWS_MLP = 512
LATENT_FF_TILE = 1024
LATENT_ROWS_NORM = 512
LATENT_STATE_CHUNKS = 4
FF_TILE = 1024


def kernel(x, c, ctx, c_ctx, w_ada, b_ada, norm1_g, w_in, ret_decay_logit, ret_norm_g, pool_w,
           pool_scale, sg_norm_g, sg_w, sg_b, w_br, w_bp, w_bs, w_out, norm2_g, w1, w2, final_norm_g):
    assert x.shape[0] == 1 and ctx.shape[0] == 1
    t = x.shape[1]
    tc = ctx.shape[1]
    xs = x[0]
    xc = ctx[0]

    cond = jnp.stack([c[0], c_ctx])[:, :, None]
    mods_all = _ada(cond, w_ada, b_ada)

    rope = _rope_tables(t)
    pool_lat = _pool_tables(LATENT_ROWS_MIX, GRID_W)
    pool_ctx = _pool_tables(tc, tc)
    zero_state = jnp.zeros((RET_HEADS, HEAD_DIM, HEAD_DIM), F32)

    tabs = jax.vmap(_decay_tables)(ret_decay_logit)
    tabs["btab"] = jnp.repeat(jnp.swapaxes(sg_b, 1, 2), GROUP_DIM, axis=2)
    vec = lambda v: v.reshape(DEPTH, 1, -1)
    p = dict(ret_norm_g=vec(ret_norm_g), pool_w=pool_w.astype(BF16), pool_scale=vec(pool_scale),
             sg_norm_g=vec(sg_norm_g), sg_w=sg_w.astype(BF16),
             w_br=w_br.astype(BF16), w_bp=w_bp.astype(BF16), w_bs=w_bs.astype(BF16),
             w_out=w_out.astype(BF16), norm2_g=vec(norm2_g))
    n1g = vec(norm1_g)
    fng = final_norm_g.reshape(1, -1)
    streamed = (w_in, w1, w2)
    w_in_b = _cast_first_w_in(w_in)
    w1_b, w2_b = w1[0].astype(BF16), w2[0].astype(BF16)

    hc = _norm_mod(xc, 0, n1g, mods_all, ROW_CONTEXT, tc)
    hx = _norm_mod(xs, 0, n1g, mods_all, ROW_LATENT, LATENT_ROWS_NORM)
    for l in range(DEPTH):
        last = l == DEPTH - 1

        zc = _in_proj(hc, w_in_b, None, tc)
        sfc, sbc, sf, sb = _states(zc, l, tabs, zero_state, zero_state, tc // CHUNK)
        if not last:
            xc1, hc2 = _mixer(zc, xc, sfc, sbc, l, tabs, pool_ctx, p, mods_all, ROW_CONTEXT, tc)
            xc, hc = _mlp(hc2, xc1, l, w1_b, w2_b, mods_all, ROW_CONTEXT, n1g, None, tc, FF_TILE)

        zx = _in_proj(hx, w_in_b, rope, LATENT_ROWS_IN)
        sfx, sbx, _, _ = _states(zx, l, tabs, sf, sb, LATENT_STATE_CHUNKS)
        x1, h2 = _mixer(zx, xs, sfx, sbx, l, tabs, pool_lat, p, mods_all, ROW_LATENT, LATENT_ROWS_MIX)
        if last:
            xs, = _mlp(h2, x1, l, w1_b, w2_b, mods_all, ROW_LATENT, n1g, fng,
                       LATENT_ROWS_MLP, LATENT_FF_TILE)
        else:
            xs, hx, w_in_b, w1_b, w2_b = _mlp(h2, x1, l, w1_b, w2_b, mods_all, ROW_LATENT, n1g, None,
                                              LATENT_ROWS_MLP, LATENT_FF_TILE, cast_next=streamed)

    return xs[None]
```

```python
import functools

import numpy as np
import jax
import jax.numpy as jnp
from jax import lax
from jax.experimental import pallas as pl
from jax.experimental.pallas import tpu as pltpu

F32 = jnp.float32
BF16 = jnp.bfloat16

D_MODEL = 2048
DEPTH = 4
GRID_W = 64
EPS = 1e-6
N_MOD = 6

HEAD_DIM = 128
RET_WIDTH = D_MODEL // 2
RET_HEADS = RET_WIDTH // HEAD_DIM
CHUNK = 128
K_SCALE = HEAD_DIM ** -0.5
ROPE_THETA = 10000.0

POOL_WIDTH = D_MODEL // 4
POOL_WINDOWS = (2, 4, 8, 16)
POOL_GROUPS = len(POOL_WINDOWS)
GROUP_DIM = POOL_WIDTH // POOL_GROUPS

SG_WIDTH = D_MODEL // 4
SG_GROUPS = 4

D_FF = 4 * D_MODEL
N_IN = 4 * RET_WIDTH + POOL_WIDTH + 2 * SG_WIDTH + 3 * D_MODEL

SUBLANES = 8
LANES = 128
VMEM_LIMIT_BYTES = 56 * 1024 * 1024

MIX_COLS = 4 * RET_WIDTH + POOL_WIDTH + 2 * SG_WIDTH
HALF_WIDTH = 3 * D_MODEL
Z_WIDTH = 2 * HALF_WIDTH
IN_TILE = Z_WIDTH // 4
N_IN_TILES = Z_WIDTH // IN_TILE
QK_TILE = HALF_WIDTH // IN_TILE
QK_COLS = 2 * RET_WIDTH
SUB_TILE = 512
_HEADS_PER_SUB = SUB_TILE // HEAD_DIM
MERGE_TILE = 512
N_MERGE = D_MODEL // MERGE_TILE
Q_COL, K_COL, V_COL, G_COL = (k * RET_WIDTH for k in range(4))
POOL_COL = 4 * RET_WIDTH
U_COL = POOL_COL + POOL_WIDTH
SV_COL = U_COL + SG_WIDTH


def _cast_w_in(src, dst_ref):
    dst_ref[:, :HALF_WIDTH] = src[:, MIX_COLS:].astype(BF16)
    dst_ref[:, HALF_WIDTH:HALF_WIDTH + MIX_COLS] = src[:, :MIX_COLS].astype(BF16)
    dst_ref[:, HALF_WIDTH + MIX_COLS:] = jnp.zeros((src.shape[0], HALF_WIDTH - MIX_COLS), BF16)

ADA_TILE = 1024
ADA_ROWS = 8
ROW_LATENT, ROW_CONTEXT = 0, 1


def _layer_spec(arr, l):
    nd = arr.ndim - 1
    return pl.BlockSpec((None,) + arr.shape[1:], lambda *_: (l,) + (0,) * nd,
                        pipeline_mode=pl.Buffered(1))


def _const_spec(arr):
    nd = arr.ndim
    return pl.BlockSpec(arr.shape, lambda *_: (0,) * nd, pipeline_mode=pl.Buffered(1))


def _mod_spec(l, k):
    return pl.BlockSpec((None, ADA_ROWS, D_MODEL), lambda *_: (l, 0, k))


def _gain_spec(l):
    return pl.BlockSpec((None, 1, D_MODEL), lambda *_: (l, 0, 0))


def _rms(x):
    return x * lax.rsqrt(jnp.mean(x * x, axis=-1, keepdims=True) + EPS)


def _norm_modulate(x, gain, shift, scale):
    return _rms(x) * gain * (1.0 + scale) + shift


def _ada_kernel(cond_ref, w_ref, b_ref, o_ref, act_scr):
    tn = w_ref.shape[1]

    @pl.when(jnp.logical_and(pl.program_id(0) == 0, pl.program_id(1) == 0))
    def _():
        for r in range(2):
            a = cond_ref[r]
            act_scr[r] = jnp.broadcast_to(a * jax.nn.sigmoid(a), (D_MODEL, LANES))

    def body(g, accs):
        r0 = pl.multiple_of(g * SUBLANES, SUBLANES)
        w = w_ref[pl.ds(r0, SUBLANES), :]
        out = []
        for r in range(2):
            a = jnp.tile(act_scr[r, pl.ds(r0, SUBLANES), :], (1, tn // LANES))
            out.append(accs[r] + w * a)
        return tuple(out)

    zero = jnp.zeros((SUBLANES, tn), F32)
    accs = lax.fori_loop(0, D_MODEL // SUBLANES, body, (zero, zero), unroll=8)
    o_ref[...] = jnp.zeros((ADA_ROWS, tn), F32)
    for r in range(2):
        o_ref[r:r + 1, :] = jnp.sum(accs[r], axis=0, keepdims=True) + b_ref[...]


def _ada(cond, w_ada, b_ada):
    n = N_MOD * D_MODEL
    return pl.pallas_call(
        _ada_kernel,
        grid=(DEPTH, n // ADA_TILE),
        in_specs=[
            pl.BlockSpec((2, D_MODEL, 1), lambda l, j: (0, 0, 0)),
            pl.BlockSpec((None, D_MODEL, ADA_TILE), lambda l, j: (l, 0, j)),
            pl.BlockSpec((None, 1, ADA_TILE), lambda l, j: (l, 0, j)),
        ],
        out_specs=pl.BlockSpec((None, ADA_ROWS, ADA_TILE), lambda l, j: (l, 0, j)),
        out_shape=jax.ShapeDtypeStruct((DEPTH, ADA_ROWS, n), F32),
        scratch_shapes=[pltpu.VMEM((2, D_MODEL, LANES), F32)],
        compiler_params=pltpu.CompilerParams(
            dimension_semantics=("arbitrary", "arbitrary"), vmem_limit_bytes=VMEM_LIMIT_BYTES),
        name="ada",
    )(cond, w_ada, b_ada.reshape(DEPTH, 1, n))


def _norm_kernel(x_ref, g_ref, shift_ref, scale_ref, h_ref, *, mod_row):
    mr = slice(mod_row, mod_row + 1)
    h_ref[...] = _norm_modulate(x_ref[...], g_ref[...], shift_ref[mr, :], scale_ref[mr, :]).astype(BF16)


def _norm_mod(x, l, norm1_g, mods_all, mod_row, tm):
    t = x.shape[0]
    return pl.pallas_call(
        functools.partial(_norm_kernel, mod_row=mod_row),
        grid=(t // tm,),
        in_specs=[pl.BlockSpec((tm, D_MODEL), lambda i: (i, 0)),
                  _gain_spec(l), _mod_spec(l, 0), _mod_spec(l, 1)],
        out_specs=pl.BlockSpec((tm, D_MODEL), lambda i: (i, 0)),
        out_shape=jax.ShapeDtypeStruct((t, D_MODEL), BF16),
        compiler_params=pltpu.CompilerParams(
            dimension_semantics=("arbitrary",), vmem_limit_bytes=VMEM_LIMIT_BYTES),
        name="norm_mod",
    )(x, norm1_g, mods_all, mods_all)


CAST_ROWS = 128


def _cast_w_in_kernel(src_ref, dst_ref):
    _cast_w_in(src_ref[...], dst_ref)


def _cast_first_w_in(w_in):
    return pl.pallas_call(
        _cast_w_in_kernel,
        grid=(D_MODEL // CAST_ROWS,),
        in_specs=[pl.BlockSpec((None, CAST_ROWS, N_IN), lambda i: (0, i, 0))],
        out_specs=pl.BlockSpec((CAST_ROWS, Z_WIDTH), lambda i: (i, 0)),
        out_shape=jax.ShapeDtypeStruct((D_MODEL, Z_WIDTH), BF16),
        compiler_params=pltpu.CompilerParams(
            dimension_semantics=("arbitrary",), vmem_limit_bytes=VMEM_LIMIT_BYTES),
        name="cast_w_in",
    )(w_in)


def _rope(t, cos, sin_lo, sin_hi):
    up = pltpu.roll(t, HEAD_DIM - 32, axis=1)
    down = pltpu.roll(t, 32, axis=1)
    return t * cos + up * sin_lo + down * sin_hi


def _inproj_kernel(*refs, use_rope):
    if use_rope:
        h_ref, w_ref, cos_ref, slo_ref, shi_ref, z_ref = refs
    else:
        h_ref, w_ref, z_ref = refs
    j = pl.program_id(1)

    @pl.when(j == QK_TILE)
    def _():
        for blk in range(QK_COLS // SUB_TILE):
            c0 = blk * SUB_TILE
            acc = jnp.dot(h_ref[...], w_ref[:, c0:c0 + SUB_TILE], preferred_element_type=F32)
            if c0 >= RET_WIDTH:
                acc = acc * K_SCALE
            if use_rope:
                cos, slo, shi = cos_ref[...], slo_ref[...], shi_ref[...]
                for hh in range(_HEADS_PER_SUB):
                    hs = slice(hh * HEAD_DIM, (hh + 1) * HEAD_DIM)
                    z_ref[:, c0 + hh * HEAD_DIM:c0 + (hh + 1) * HEAD_DIM] = (
                        _rope(acc[:, hs], cos, slo, shi).astype(BF16))
            else:
                z_ref[:, c0:c0 + SUB_TILE] = acc.astype(BF16)
        z_ref[:, QK_COLS:] = jnp.dot(h_ref[...], w_ref[:, QK_COLS:],
                                     preferred_element_type=F32).astype(BF16)

    @pl.when(j != QK_TILE)
    def _():
        z_ref[...] = jnp.dot(h_ref[...], w_ref[...], preferred_element_type=F32).astype(BF16)


def _in_proj(h, w_in, rope, tm):
    t = h.shape[0]
    use_rope = rope is not None
    in_specs = [
        pl.BlockSpec((tm, D_MODEL), lambda i, j: (i, 0)),
        pl.BlockSpec((D_MODEL, IN_TILE), lambda i, j: (0, j)),
    ]
    args = [h, w_in]
    if use_rope:
        in_specs += [pl.BlockSpec((tm, HEAD_DIM), lambda i, j: (i, 0))] * 3
        args += list(rope)
    return pl.pallas_call(
        functools.partial(_inproj_kernel, use_rope=use_rope),
        grid=(t // tm, N_IN_TILES),
        in_specs=in_specs,
        out_specs=pl.BlockSpec((tm, IN_TILE), lambda i, j: (i, j)),
        out_shape=jax.ShapeDtypeStruct((t, Z_WIDTH), BF16),
        compiler_params=pltpu.CompilerParams(
            dimension_semantics=("arbitrary", "arbitrary"), vmem_limit_bytes=VMEM_LIMIT_BYTES),
        name="in_proj",
    )(*args)


def _state_kernel(k_ref, v_ref, kdf_ref, kdb_ref, cdf_ref, cdb_ref, sf0_ref, sb0_ref,
                  sf_all_ref, sb_all_ref, sf_fin_ref, sb_fin_ref, sf_scr, sb_scr):
    t = pl.program_id(0)
    nb = pl.num_programs(0)
    cps = sf_all_ref.shape[0]

    @pl.when(t == 0)
    def _():
        sf_scr[...] = sf0_ref[...]
        sb_scr[...] = sb0_ref[...]

    contract_rows = (((0,), (0,)), ((), ()))

    def advance(block, c, kd_ref, cd_ref, all_ref, scr):
        rows = pl.ds(pl.multiple_of((block * cps + c) * CHUNK, CHUNK), CHUNK)
        all_ref[c] = scr[...].astype(BF16)
        for h in range(RET_HEADS):
            hs = slice(h * HEAD_DIM, (h + 1) * HEAD_DIM)
            kd = (k_ref[rows, hs].astype(F32) * kd_ref[:, hs]).astype(BF16)
            kv = lax.dot_general(kd, v_ref[rows, hs], contract_rows, preferred_element_type=F32)
            scr[h] = scr[h] * cd_ref[h] + kv

    for c in range(cps):
        advance(t, c, kdf_ref, cdf_ref, sf_all_ref, sf_scr)
        advance(nb - 1 - t, cps - 1 - c, kdb_ref, cdb_ref, sb_all_ref, sb_scr)

    @pl.when(t == pl.num_programs(0) - 1)
    def _():
        sf_fin_ref[...] = sf_scr[...]
        sb_fin_ref[...] = sb_scr[...]


def _states(z, l, tabs, sf0, sb0, cps):
    t = z.shape[0]
    nb = t // (cps * CHUNK)
    whole = lambda c: pl.BlockSpec((t, RET_WIDTH), lambda i: (0, c), pipeline_mode=pl.Buffered(1))
    state_shape = (RET_HEADS, HEAD_DIM, HEAD_DIM)
    consts = [tabs["kdf"], tabs["kdb"], tabs["cdf"], tabs["cdb"]]
    kb, vb = (HALF_WIDTH + K_COL) // RET_WIDTH, (HALF_WIDTH + V_COL) // RET_WIDTH
    in_specs = [whole(kb), whole(vb)] + [_layer_spec(a, l) for a in consts]
    in_specs += [_const_spec(sf0), _const_spec(sb0)]
    all_shape = jax.ShapeDtypeStruct((t // CHUNK,) + state_shape, BF16)
    fin_shape = jax.ShapeDtypeStruct(state_shape, F32)
    return pl.pallas_call(
        _state_kernel,
        grid=(nb,),
        in_specs=in_specs,
        out_specs=[
            pl.BlockSpec((cps,) + state_shape, lambda i: (i, 0, 0, 0)),
            pl.BlockSpec((cps,) + state_shape, lambda i: (nb - 1 - i, 0, 0, 0)),
            pl.BlockSpec(state_shape, lambda i: (0, 0, 0)),
            pl.BlockSpec(state_shape, lambda i: (0, 0, 0)),
        ],
        out_shape=[all_shape, all_shape, fin_shape, fin_shape],
        scratch_shapes=[pltpu.VMEM(state_shape, F32), pltpu.VMEM(state_shape, F32)],
        compiler_params=pltpu.CompilerParams(
            dimension_semantics=("arbitrary",), vmem_limit_bytes=VMEM_LIMIT_BYTES),
        name="ret_states",
    )(z, z, *consts, sf0, sb0)


def _mixer_kernel(*refs, mod_row):
    (zmix_ref, sf_ref, sb_ref, zgate_ref, x_ref,
     dmat_ref, qdf_ref, qdb_ref, retg_ref, pmask_ref, pinv_ref, poolw_ref, pscale_ref,
     sgng_ref, sgw_ref, btab_ref, wbr_ref, wbp_ref, wbs_ref, wout_ref,
     gate_ref, n2g_ref, shift2_ref, scale2_ref,
     x1_ref, h2_ref,
     y_scr, ret_cur, pool_cur, sg_cur, ret_prev, pool_prev, sg_prev) = refs
    tb = x_ref.shape[0]
    nck = tb // CHUNK
    contract_last = (((1,), (1,)), ((), ()))
    mr = slice(mod_row, mod_row + 1)

    def merge_piece(i):
        cs = slice(i * MERGE_TILE, (i + 1) * MERGE_TILE)
        def gate(branch):
            c0 = branch * D_MODEL + i * MERGE_TILE
            return jax.nn.sigmoid(zgate_ref[:, c0:c0 + MERGE_TILE].astype(F32))

        y = gate(0) * jnp.dot(ret_prev[...], wbr_ref[:, cs], preferred_element_type=F32)
        y = y + gate(1) * jnp.dot(pool_prev[...], wbp_ref[:, cs], preferred_element_type=F32)
        y = y + gate(2) * jnp.dot(sg_prev[...], wbs_ref[:, cs], preferred_element_type=F32)
        y_scr[:, cs] = y.astype(BF16)

    def out_piece(i):
        cs = slice(i * MERGE_TILE, (i + 1) * MERGE_TILE)
        out = jnp.dot(y_scr[...], wout_ref[:, cs], preferred_element_type=F32)
        x1_ref[:, cs] = x_ref[:, cs] + gate_ref[mr, cs] * out

    def ret_unit(c, h):
        rows = slice(c * CHUNK, (c + 1) * CHUNK)
        hs = slice(h * HEAD_DIM, (h + 1) * HEAD_DIM)
        head = lambda col: zmix_ref[rows, col + h * HEAD_DIM:col + (h + 1) * HEAD_DIM]
        qb = head(Q_COL)
        scores = lax.dot_general(qb, head(K_COL), contract_last, preferred_element_type=F32)
        scores = (scores * dmat_ref[h]).astype(BF16)
        cross = qdf_ref[:, hs] * jnp.dot(qb, sf_ref[c, h], preferred_element_type=F32)
        cross = cross + qdb_ref[:, hs] * jnp.dot(qb, sb_ref[c, h], preferred_element_type=F32)

        def second():
            o = jnp.dot(scores, head(V_COL), preferred_element_type=F32) + cross
            g = head(G_COL).astype(F32)
            ret = _rms(o) * retg_ref[:, hs] * (g * jax.nn.sigmoid(g))
            ret_cur[rows, hs] = ret.astype(BF16)

        return second

    def pool_unit(gi):
        gs = slice(gi * GROUP_DIM, (gi + 1) * GROUP_DIM)
        pb = zmix_ref[:, POOL_COL + gi * GROUP_DIM:POOL_COL + (gi + 1) * GROUP_DIM]
        win = jnp.dot(pmask_ref[gi], pb, preferred_element_type=F32)
        pooled = (win * pinv_ref[:, gs] - pb.astype(F32)).astype(BF16)

        def second():
            mapped = jnp.dot(pooled, poolw_ref[gi], preferred_element_type=F32)
            pool_cur[:, gs] = (mapped * pscale_ref[:, gs]).astype(BF16)

        return second

    def sg_unit(c):
        rows = slice(c * CHUNK, (c + 1) * CHUNK)
        sv = zmix_ref[rows, SV_COL:SV_COL + SG_WIDTH].astype(F32)
        sv = (_rms(jax.nn.gelu(sv)) * sgng_ref[...]).astype(BF16)

        def second():
            for gi in range(SG_GROUPS):
                gs = slice(gi * GROUP_DIM, (gi + 1) * GROUP_DIM)
                mixed = jnp.dot(sgw_ref[gi], sv[:, gs], preferred_element_type=F32)
                mixed = mixed + btab_ref[:, gs]
                u = zmix_ref[rows, U_COL + gi * GROUP_DIM:U_COL + (gi + 1) * GROUP_DIM]
                u = jax.nn.gelu(u.astype(F32))
                sg_cur[rows, gs] = (u * mixed).astype(BF16)

        return second

    mix_units = [functools.partial(ret_unit, c, h) for c in range(nck) for h in range(RET_HEADS)]
    mix_units += [functools.partial(pool_unit, gi) for gi in range(POOL_GROUPS)]
    mix_units += [functools.partial(sg_unit, c) for c in range(nck)]
    pieces = ([functools.partial(merge_piece, i) for i in range(N_MERGE)]
              + [functools.partial(out_piece, i) for i in range(N_MERGE)])

    def run(mix, merge):
        units = mix_units if mix else []
        per_piece = -(-len(units) // len(pieces))
        pending = [unit() for unit in units[:per_piece]]
        for i, piece in enumerate(pieces):
            if merge:
                piece()
            for second in pending:
                second()
            pending = [unit() for unit in units[(i + 1) * per_piece:(i + 2) * per_piece]]
        assert not pending
        if merge:
            h2_ref[...] = _norm_modulate(x1_ref[...], n2g_ref[...], shift2_ref[mr, :],
                                         scale2_ref[mr, :]).astype(BF16)
        if mix:
            ret_prev[...] = ret_cur[...]
            pool_prev[...] = pool_cur[...]
            sg_prev[...] = sg_cur[...]

    s = pl.program_id(0)
    last = pl.num_programs(0) - 1
    pl.when(s == 0)(functools.partial(run, True, False))
    pl.when(jnp.logical_and(s > 0, s < last))(functools.partial(run, True, True))
    pl.when(s == last)(functools.partial(run, False, True))


def _mixer(z, x, sf_all, sb_all, l, tabs, pool_tabs, p, mods_all, mod_row, tb):
    t = x.shape[0]
    nb = t // tb
    nck = tb // CHUNK
    cur = lambda s: jnp.minimum(s, nb - 1)
    prev = lambda s: jnp.maximum(s - 1, 0)

    state_spec = pl.BlockSpec((nck, RET_HEADS, HEAD_DIM, HEAD_DIM), lambda s: (cur(s), 0, 0, 0))
    row_spec = pl.BlockSpec((tb, D_MODEL), lambda s: (prev(s), 0))
    in_specs = [pl.BlockSpec((tb, HALF_WIDTH), lambda s: (cur(s), 1)), state_spec, state_spec,
                pl.BlockSpec((tb, HALF_WIDTH), lambda s: (prev(s), 0)), row_spec]
    args = [z, sf_all, sb_all, z, x]
    pmask, pinv = pool_tabs
    layer_consts = [tabs["dmat"], tabs["qdf"], tabs["qdb"], p["ret_norm_g"]]
    in_specs += [_layer_spec(a, l) for a in layer_consts] + [_const_spec(pmask), _const_spec(pinv)]
    args += layer_consts + [pmask, pinv]
    layer_consts = [p["pool_w"], p["pool_scale"], p["sg_norm_g"], p["sg_w"], tabs["btab"],
                    p["w_br"], p["w_bp"], p["w_bs"], p["w_out"]]
    in_specs += [_layer_spec(a, l) for a in layer_consts]
    args += layer_consts
    in_specs += [_mod_spec(l, 2), _gain_spec(l), _mod_spec(l, 3), _mod_spec(l, 4)]
    args += [mods_all, p["norm2_g"], mods_all, mods_all]
    mix_scratch = [pltpu.VMEM((tb, RET_WIDTH), BF16), pltpu.VMEM((tb, POOL_WIDTH), BF16),
                   pltpu.VMEM((tb, SG_WIDTH), BF16)]
    return pl.pallas_call(
        functools.partial(_mixer_kernel, mod_row=mod_row),
        grid=(nb + 1,),
        in_specs=in_specs,
        out_specs=[row_spec, row_spec],
        out_shape=[jax.ShapeDtypeStruct((t, D_MODEL), F32),
                   jax.ShapeDtypeStruct((t, D_MODEL), BF16)],
        scratch_shapes=[pltpu.VMEM((tb, D_MODEL), BF16)] + mix_scratch + mix_scratch,
        compiler_params=pltpu.CompilerParams(
            dimension_semantics=("arbitrary",), vmem_limit_bytes=VMEM_LIMIT_BYTES),
        name="mixer",
    )(*args)


EPILOGUE_ROWS = 32
CAST_EVERY = 2


def _mlp_kernel(*refs, mod_row, final, n_cast):
    refs = list(refs)
    cast_src = cast_dst = ()
    if n_cast:
        cast_dst = refs[-n_cast:]
        refs = refs[:-n_cast]
        n_out = 1 if final else 2
        cast_src = refs[-n_out - n_cast:-n_out]
        refs = refs[:-n_out - n_cast] + refs[-n_out:]
    if final:
        h2_ref, x1_ref, w1_ref, w2_ref, gate_ref, fng_ref, o_ref = refs
    else:
        (h2_ref, x1_ref, w1_ref, w2_ref, gate_ref, ng_ref, nshift_ref, nscale_ref,
         o_ref, hn_ref) = refs
    f = pl.program_id(1)
    mr = slice(mod_row, mod_row + 1)

    def step(first):
        a = jnp.maximum(jnp.dot(h2_ref[...], w1_ref[...], preferred_element_type=F32), 0.0)
        part = jnp.dot((a * a).astype(BF16), w2_ref[...], preferred_element_type=F32)
        if first:
            o_ref[...] = part
        else:
            o_ref[...] += part
        part_id = (pl.program_id(0) * pl.num_programs(1) + f) % CAST_EVERY
        for k, (src, dst) in enumerate(zip(cast_src, cast_dst)):
            share = src.shape[0] // CAST_EVERY
            rows = pl.ds(pl.multiple_of(part_id * share, share), share)
            if k == 0:
                _cast_w_in(src[rows, :], dst.at[rows])
            else:
                dst[rows, :] = src[rows, :].astype(BF16)

    @pl.when(f == 0)
    def _():
        step(True)

    @pl.when(f > 0)
    def _():
        step(False)

    @pl.when(f == pl.num_programs(1) - 1)
    def _():
        gate = gate_ref[mr, :]
        if final:
            gain = fng_ref[...]
        else:
            gain = ng_ref[...] * (1.0 + nscale_ref[mr, :])
            shift = nshift_ref[mr, :]

        for r in range(o_ref.shape[0] // EPILOGUE_ROWS):
            rows = slice(r * EPILOGUE_ROWS, (r + 1) * EPILOGUE_ROWS)
            x2 = x1_ref[rows, :] + gate * o_ref[rows, :]
            if final:
                o_ref[rows, :] = _rms(x2) * gain
            else:
                o_ref[rows, :] = x2
                hn_ref[rows, :] = (_rms(x2) * gain + shift).astype(BF16)


def _mlp(h2, x1, l, w1, w2, mods_all, mod_row, norm1_g, final_gain, tm, tf, cast_next=()):
    t = x1.shape[0]
    final = final_gain is not None
    n_f = D_FF // tf
    grid = (t // tm, n_f)
    rows = pl.BlockSpec((tm, D_MODEL), lambda i, f: (i, 0))
    in_specs = [rows, rows,
                pl.BlockSpec((D_MODEL, tf), lambda i, f: (0, f)),
                pl.BlockSpec((tf, D_MODEL), lambda i, f: (f, 0)),
                _mod_spec(l, 5)]
    args = [h2, x1, w1, w2, mods_all]
    x_shape = jax.ShapeDtypeStruct((t, D_MODEL), F32)
    if final:
        in_specs.append(pl.BlockSpec((1, D_MODEL), lambda i, f: (0, 0)))
        args.append(final_gain)
        out_specs, out_shape = [rows], [x_shape]
    else:
        in_specs += [_gain_spec(l + 1), _mod_spec(l + 1, 0), _mod_spec(l + 1, 1)]
        args += [norm1_g, mods_all, mods_all]
        out_specs = [rows, rows]
        out_shape = [x_shape, jax.ShapeDtypeStruct((t, D_MODEL), BF16)]
    for k, w in enumerate(cast_next):
        n_slabs = grid[0] * grid[1] // CAST_EVERY
        slab, cols = w.shape[1] // n_slabs, w.shape[2]
        assert slab * n_slabs == w.shape[1] and slab % (2 * SUBLANES) == 0
        out_cols = Z_WIDTH if k == 0 else cols
        in_specs.append(pl.BlockSpec((None, slab, cols),
                                     lambda i, f: (l + 1, (i * n_f + f) // CAST_EVERY, 0)))
        args.append(w)
        out_specs.append(pl.BlockSpec((slab, out_cols), lambda i, f: ((i * n_f + f) // CAST_EVERY, 0)))
        out_shape.append(jax.ShapeDtypeStruct((w.shape[1], out_cols), BF16))
    return pl.pallas_call(
        functools.partial(_mlp_kernel, mod_row=mod_row, final=final, n_cast=len(cast_next)),
        grid=grid,
        in_specs=in_specs,
        out_specs=out_specs,
        out_shape=out_shape,
        compiler_params=pltpu.CompilerParams(
            dimension_semantics=("arbitrary", "arbitrary"), vmem_limit_bytes=VMEM_LIMIT_BYTES),
        name="mlp",
    )(*args)


def _rope_tables(t):
    half = HEAD_DIM // 2
    nf = half // 2
    tok = np.arange(t)
    inv = ROPE_THETA ** (-np.arange(nf, dtype=np.float64) / nf)
    lane = np.arange(HEAD_DIM)
    pos = np.where(lane[None, :] < half, (tok // GRID_W)[:, None], (tok % GRID_W)[:, None])
    ang = pos * inv[lane % nf][None, :]
    lower = (lane % half) < nf
    sin = np.sin(ang)
    cos = np.cos(ang).astype(np.float32)
    sin_lo = np.where(lower[None, :], -sin, 0.0).astype(np.float32)
    sin_hi = np.where(lower[None, :], 0.0, sin).astype(np.float32)
    return jnp.asarray(cos), jnp.asarray(sin_lo), jnp.asarray(sin_hi)


def _pool_tables(tb, seg_len):
    pos = np.arange(tb) % seg_len
    base = np.arange(tb) - pos
    col = np.arange(tb)[None, :]
    masks, invs = [], []
    for w in POOL_WINDOWS:
        lo = np.maximum(pos - w // 2, 0)
        hi = np.minimum(pos + w // 2 - 1, seg_len - 1)
        masks.append((col >= (base + lo)[:, None]) & (col <= (base + hi)[:, None]))
        invs.append(np.repeat((1.0 / (hi - lo + 1))[:, None], GROUP_DIM, axis=1))
    pmask = jnp.asarray(np.stack(masks).astype(np.float32), dtype=BF16)
    pinv = jnp.asarray(np.concatenate(invs, axis=1).astype(np.float32))
    return pmask, pinv


def _decay_tables(logit):
    lg = jax.nn.log_sigmoid(logit.astype(F32))
    lgf, lgb = lg[0], lg[1]
    idx = jnp.arange(CHUNK, dtype=F32)
    dist = idx[:, None] - idx[None, :]
    fwd = jnp.exp(lgf[:, None, None] * jnp.maximum(dist, 0.0))
    bwd = jnp.exp(lgb[:, None, None] * jnp.maximum(-dist, 0.0))
    dmat = jnp.where(dist > 0, fwd, jnp.where(dist < 0, bwd, 2.0))

    def lanes(tab):
        return jnp.repeat(tab.T, HEAD_DIM, axis=1)

    return dict(
        dmat=dmat,
        qdf=lanes(jnp.exp(lgf[:, None] * (idx + 1.0)[None])),
        qdb=lanes(jnp.exp(lgb[:, None] * (CHUNK - idx)[None])),
        kdf=lanes(jnp.exp(lgf[:, None] * (CHUNK - 1.0 - idx)[None])),
        kdb=lanes(jnp.exp(lgb[:, None] * idx[None])),
        cdf=jnp.broadcast_to(jnp.exp(lgf * CHUNK)[:, None, None], (RET_HEADS, 1, HEAD_DIM)),
        cdb=jnp.broadcast_to(jnp.exp(lgb * CHUNK)[:, None, None], (RET_HEADS, 1, HEAD_DIM)),
    )


LATENT_ROWS_IN = 1024
LATENT_ROWS_MIX = 256
LATENT_ROWS_MLP = 512
LATENT_FF_TILE = 1024
LATENT_ROWS_NORM = 512
LATENT_STATE_CHUNKS = 4
FF_TILE = 1024


def kernel(x, c, ctx, c_ctx, w_ada, b_ada, norm1_g, w_in, ret_decay_logit, ret_norm_g, pool_w,
           pool_scale, sg_norm_g, sg_w, sg_b, w_br, w_bp, w_bs, w_out, norm2_g, w1, w2, final_norm_g):
    assert x.shape[0] == 1 and ctx.shape[0] == 1
    t = x.shape[1]
    tc = ctx.shape[1]
    xs = x[0]
    xc = ctx[0]

    cond = jnp.stack([c[0], c_ctx])[:, :, None]
    mods_all = _ada(cond, w_ada, b_ada)

    rope = _rope_tables(t)
    pool_lat = _pool_tables(LATENT_ROWS_MIX, GRID_W)
    pool_ctx = _pool_tables(tc, tc)
    zero_state = jnp.zeros((RET_HEADS, HEAD_DIM, HEAD_DIM), F32)

    tabs = jax.vmap(_decay_tables)(ret_decay_logit)
    tabs["btab"] = jnp.repeat(jnp.swapaxes(sg_b, 1, 2), GROUP_DIM, axis=2)
    vec = lambda v: v.reshape(DEPTH, 1, -1)
    p = dict(ret_norm_g=vec(ret_norm_g), pool_w=pool_w.astype(BF16), pool_scale=vec(pool_scale),
             sg_norm_g=vec(sg_norm_g), sg_w=sg_w.astype(BF16),
             w_br=w_br.astype(BF16), w_bp=w_bp.astype(BF16), w_bs=w_bs.astype(BF16),
             w_out=w_out.astype(BF16), norm2_g=vec(norm2_g))
    n1g = vec(norm1_g)
    fng = final_norm_g.reshape(1, -1)
    streamed = (w_in, w1, w2)
    w_in_b = _cast_first_w_in(w_in)
    w1_b, w2_b = w1[0].astype(BF16), w2[0].astype(BF16)

    hc = _norm_mod(xc, 0, n1g, mods_all, ROW_CONTEXT, tc)
    hx = _norm_mod(xs, 0, n1g, mods_all, ROW_LATENT, LATENT_ROWS_NORM)
    for l in range(DEPTH):
        last = l == DEPTH - 1

        zc = _in_proj(hc, w_in_b, None, tc)
        sfc, sbc, sf, sb = _states(zc, l, tabs, zero_state, zero_state, tc // CHUNK)
        if not last:
            xc1, hc2 = _mixer(zc, xc, sfc, sbc, l, tabs, pool_ctx, p, mods_all, ROW_CONTEXT, tc)
            xc, hc = _mlp(hc2, xc1, l, w1_b, w2_b, mods_all, ROW_CONTEXT, n1g, None, tc, FF_TILE)

        zx = _in_proj(hx, w_in_b, rope, LATENT_ROWS_IN)
        sfx, sbx, _, _ = _states(zx, l, tabs, sf, sb, LATENT_STATE_CHUNKS)
        x1, h2 = _mixer(zx, xs, sfx, sbx, l, tabs, pool_lat, p, mods_all, ROW_LATENT, LATENT_ROWS_MIX)
        if last:
            xs, = _mlp(h2, x1, l, w1_b, w2_b, mods_all, ROW_LATENT, n1g, fng,
                       LATENT_ROWS_MLP, LATENT_FF_TILE)
        else:
            xs, hx, w_in_b, w1_b, w2_b = _mlp(h2, x1, l, w1_b, w2_b, mods_all, ROW_LATENT, n1g, None,
                                              LATENT_ROWS_MLP, LATENT_FF_TILE, cast_next=streamed)

    return xs[None]
```

```python
import functools

import numpy as np
import jax
import jax.numpy as jnp
from jax import lax
from jax.experimental import pallas as pl
from jax.experimental.pallas import tpu as pltpu

F32 = jnp.float32
BF16 = jnp.bfloat16

D_MODEL = 2048
DEPTH = 4
GRID_W = 64
EPS = 1e-6
N_MOD = 6

HEAD_DIM = 128
RET_WIDTH = D_MODEL // 2
RET_HEADS = RET_WIDTH // HEAD_DIM
CHUNK = 128
K_SCALE = HEAD_DIM ** -0.5
ROPE_THETA = 10000.0

POOL_WIDTH = D_MODEL // 4
POOL_WINDOWS = (2, 4, 8, 16)
POOL_GROUPS = len(POOL_WINDOWS)
GROUP_DIM = POOL_WIDTH // POOL_GROUPS

SG_WIDTH = D_MODEL // 4
SG_GROUPS = 4

D_FF = 4 * D_MODEL
N_IN = 4 * RET_WIDTH + POOL_WIDTH + 2 * SG_WIDTH + 3 * D_MODEL

SUBLANES = 8
LANES = 128
VMEM_LIMIT_BYTES = 56 * 1024 * 1024

MIX_COLS = 4 * RET_WIDTH + POOL_WIDTH + 2 * SG_WIDTH
HALF_WIDTH = 3 * D_MODEL
Z_WIDTH = 2 * HALF_WIDTH
IN_TILE = Z_WIDTH // 4
N_IN_TILES = Z_WIDTH // IN_TILE
QK_TILE = HALF_WIDTH // IN_TILE
QK_COLS = 2 * RET_WIDTH
SUB_TILE = 512
_HEADS_PER_SUB = SUB_TILE // HEAD_DIM
MERGE_TILE = 512
N_MERGE = D_MODEL // MERGE_TILE
Q_COL, K_COL, V_COL, G_COL = (k * RET_WIDTH for k in range(4))
POOL_COL = 4 * RET_WIDTH
U_COL = POOL_COL + POOL_WIDTH
SV_COL = U_COL + SG_WIDTH


def _cast_w_in(src, dst_ref):
    dst_ref[:, :HALF_WIDTH] = src[:, MIX_COLS:].astype(BF16)
    dst_ref[:, HALF_WIDTH:HALF_WIDTH + MIX_COLS] = src[:, :MIX_COLS].astype(BF16)
    dst_ref[:, HALF_WIDTH + MIX_COLS:] = jnp.zeros((src.shape[0], HALF_WIDTH - MIX_COLS), BF16)

ADA_TILE = 1024
ADA_ROWS = 8
ROW_LATENT, ROW_CONTEXT = 0, 1


def _layer_spec(arr, l):
    nd = arr.ndim - 1
    return pl.BlockSpec((None,) + arr.shape[1:], lambda *_: (l,) + (0,) * nd,
                        pipeline_mode=pl.Buffered(1))


def _const_spec(arr):
    nd = arr.ndim
    return pl.BlockSpec(arr.shape, lambda *_: (0,) * nd, pipeline_mode=pl.Buffered(1))


def _mod_spec(l, k):
    return pl.BlockSpec((None, ADA_ROWS, D_MODEL), lambda *_: (l, 0, k))


def _gain_spec(l):
    return pl.BlockSpec((None, 1, D_MODEL), lambda *_: (l, 0, 0))


def _rms(x):
    return x * lax.rsqrt(jnp.mean(x * x, axis=-1, keepdims=True) + EPS)


def _norm_modulate(x, gain, shift, scale):
    return _rms(x) * gain * (1.0 + scale) + shift


def _ada_kernel(cond_ref, w_ref, b_ref, o_ref, act_scr):
    tn = w_ref.shape[1]

    @pl.when(jnp.logical_and(pl.program_id(0) == 0, pl.program_id(1) == 0))
    def _():
        for r in range(2):
            a = cond_ref[r]
            act_scr[r] = jnp.broadcast_to(a * jax.nn.sigmoid(a), (D_MODEL, LANES))

    def body(g, accs):
        r0 = pl.multiple_of(g * SUBLANES, SUBLANES)
        w = w_ref[pl.ds(r0, SUBLANES), :]
        out = []
        for r in range(2):
            a = jnp.tile(act_scr[r, pl.ds(r0, SUBLANES), :], (1, tn // LANES))
            out.append(accs[r] + w * a)
        return tuple(out)

    zero = jnp.zeros((SUBLANES, tn), F32)
    accs = lax.fori_loop(0, D_MODEL // SUBLANES, body, (zero, zero), unroll=8)
    o_ref[...] = jnp.zeros((ADA_ROWS, tn), F32)
    for r in range(2):
        o_ref[r:r + 1, :] = jnp.sum(accs[r], axis=0, keepdims=True) + b_ref[...]


def _ada(cond, w_ada, b_ada):
    n = N_MOD * D_MODEL
    return pl.pallas_call(
        _ada_kernel,
        grid=(DEPTH, n // ADA_TILE),
        in_specs=[
            pl.BlockSpec((2, D_MODEL, 1), lambda l, j: (0, 0, 0)),
            pl.BlockSpec((None, D_MODEL, ADA_TILE), lambda l, j: (l, 0, j)),
            pl.BlockSpec((None, 1, ADA_TILE), lambda l, j: (l, 0, j)),
        ],
        out_specs=pl.BlockSpec((None, ADA_ROWS, ADA_TILE), lambda l, j: (l, 0, j)),
        out_shape=jax.ShapeDtypeStruct((DEPTH, ADA_ROWS, n), F32),
        scratch_shapes=[pltpu.VMEM((2, D_MODEL, LANES), F32)],
        compiler_params=pltpu.CompilerParams(
            dimension_semantics=("arbitrary", "arbitrary"), vmem_limit_bytes=VMEM_LIMIT_BYTES),
        name="ada",
    )(cond, w_ada, b_ada.reshape(DEPTH, 1, n))


def _norm_kernel(x_ref, g_ref, shift_ref, scale_ref, h_ref, *, mod_row):
    mr = slice(mod_row, mod_row + 1)
    h_ref[...] = _norm_modulate(x_ref[...], g_ref[...], shift_ref[mr, :], scale_ref[mr, :]).astype(BF16)


def _norm_mod(x, l, norm1_g, mods_all, mod_row, tm):
    t = x.shape[0]
    return pl.pallas_call(
        functools.partial(_norm_kernel, mod_row=mod_row),
        grid=(t // tm,),
        in_specs=[pl.BlockSpec((tm, D_MODEL), lambda i: (i, 0)),
                  _gain_spec(l), _mod_spec(l, 0), _mod_spec(l, 1)],
        out_specs=pl.BlockSpec((tm, D_MODEL), lambda i: (i, 0)),
        out_shape=jax.ShapeDtypeStruct((t, D_MODEL), BF16),
        compiler_params=pltpu.CompilerParams(
            dimension_semantics=("arbitrary",), vmem_limit_bytes=VMEM_LIMIT_BYTES),
        name="norm_mod",
    )(x, norm1_g, mods_all, mods_all)


CAST_ROWS = 128


def _cast_w_in_kernel(src_ref, dst_ref):
    _cast_w_in(src_ref[...], dst_ref)


def _cast_first_w_in(w_in):
    return pl.pallas_call(
        _cast_w_in_kernel,
        grid=(D_MODEL // CAST_ROWS,),
        in_specs=[pl.BlockSpec((None, CAST_ROWS, N_IN), lambda i: (0, i, 0))],
        out_specs=pl.BlockSpec((CAST_ROWS, Z_WIDTH), lambda i: (i, 0)),
        out_shape=jax.ShapeDtypeStruct((D_MODEL, Z_WIDTH), BF16),
        compiler_params=pltpu.CompilerParams(
            dimension_semantics=("arbitrary",), vmem_limit_bytes=VMEM_LIMIT_BYTES),
        name="cast_w_in",
    )(w_in)


def _rope(t, cos, sin_lo, sin_hi):
    up = pltpu.roll(t, HEAD_DIM - 32, axis=1)
    down = pltpu.roll(t, 32, axis=1)
    return t * cos + up * sin_lo + down * sin_hi


def _inproj_kernel(*refs, use_rope):
    if use_rope:
        h_ref, w_ref, cos_ref, slo_ref, shi_ref, z_ref = refs
    else:
        h_ref, w_ref, z_ref = refs
    j = pl.program_id(1)

    @pl.when(j == QK_TILE)
    def _():
        for blk in range(QK_COLS // SUB_TILE):
            c0 = blk * SUB_TILE
            acc = jnp.dot(h_ref[...], w_ref[:, c0:c0 + SUB_TILE], preferred_element_type=F32)
            if c0 >= RET_WIDTH:
                acc = acc * K_SCALE
            if use_rope:
                cos, slo, shi = cos_ref[...], slo_ref[...], shi_ref[...]
                for hh in range(_HEADS_PER_SUB):
                    hs = slice(hh * HEAD_DIM, (hh + 1) * HEAD_DIM)
                    z_ref[:, c0 + hh * HEAD_DIM:c0 + (hh + 1) * HEAD_DIM] = (
                        _rope(acc[:, hs], cos, slo, shi).astype(BF16))
            else:
                z_ref[:, c0:c0 + SUB_TILE] = acc.astype(BF16)
        z_ref[:, QK_COLS:] = jnp.dot(h_ref[...], w_ref[:, QK_COLS:],
                                     preferred_element_type=F32).astype(BF16)

    @pl.when(j != QK_TILE)
    def _():
        z_ref[...] = jnp.dot(h_ref[...], w_ref[...], preferred_element_type=F32).astype(BF16)


def _in_proj(h, w_in, rope, tm):
    t = h.shape[0]
    use_rope = rope is not None
    in_specs = [
        pl.BlockSpec((tm, D_MODEL), lambda i, j: (i, 0)),
        pl.BlockSpec((D_MODEL, IN_TILE), lambda i, j: (0, j)),
    ]
    args = [h, w_in]
    if use_rope:
        in_specs += [pl.BlockSpec((tm, HEAD_DIM), lambda i, j: (i, 0))] * 3
        args += list(rope)
    return pl.pallas_call(
        functools.partial(_inproj_kernel, use_rope=use_rope),
        grid=(t // tm, N_IN_TILES),
        in_specs=in_specs,
        out_specs=pl.BlockSpec((tm, IN_TILE), lambda i, j: (i, j)),
        out_shape=jax.ShapeDtypeStruct((t, Z_WIDTH), BF16),
        compiler_params=pltpu.CompilerParams(
            dimension_semantics=("arbitrary", "arbitrary"), vmem_limit_bytes=VMEM_LIMIT_BYTES),
        name="in_proj",
    )(*args)


def _state_kernel(k_ref, v_ref, kdf_ref, kdb_ref, cdf_ref, cdb_ref, sf0_ref, sb0_ref,
                  sf_all_ref, sb_all_ref, sf_fin_ref, sb_fin_ref, sf_scr, sb_scr):
    t = pl.program_id(0)
    nb = pl.num_programs(0)
    cps = sf_all_ref.shape[0]

    @pl.when(t == 0)
    def _():
        sf_scr[...] = sf0_ref[...]
        sb_scr[...] = sb0_ref[...]

    contract_rows = (((0,), (0,)), ((), ()))

    def advance(block, c, kd_ref, cd_ref, all_ref, scr):
        rows = pl.ds(pl.multiple_of((block * cps + c) * CHUNK, CHUNK), CHUNK)
        all_ref[c] = scr[...].astype(BF16)
        for h in range(RET_HEADS):
            hs = slice(h * HEAD_DIM, (h + 1) * HEAD_DIM)
            kd = (k_ref[rows, hs].astype(F32) * kd_ref[:, hs]).astype(BF16)
            kv = lax.dot_general(kd, v_ref[rows, hs], contract_rows, preferred_element_type=F32)
            scr[h] = scr[h] * cd_ref[h] + kv

    for c in range(cps):
        advance(t, c, kdf_ref, cdf_ref, sf_all_ref, sf_scr)
        advance(nb - 1 - t, cps - 1 - c, kdb_ref, cdb_ref, sb_all_ref, sb_scr)

    @pl.when(t == pl.num_programs(0) - 1)
    def _():
        sf_fin_ref[...] = sf_scr[...]
        sb_fin_ref[...] = sb_scr[...]


def _states(z, l, tabs, sf0, sb0, cps):
    t = z.shape[0]
    nb = t // (cps * CHUNK)
    whole = lambda c: pl.BlockSpec((t, RET_WIDTH), lambda i: (0, c), pipeline_mode=pl.Buffered(1))
    state_shape = (RET_HEADS, HEAD_DIM, HEAD_DIM)
    consts = [tabs["kdf"], tabs["kdb"], tabs["cdf"], tabs["cdb"]]
    kb, vb = (HALF_WIDTH + K_COL) // RET_WIDTH, (HALF_WIDTH + V_COL) // RET_WIDTH
    in_specs = [whole(kb), whole(vb)] + [_layer_spec(a, l) for a in consts]
    in_specs += [_const_spec(sf0), _const_spec(sb0)]
    all_shape = jax.ShapeDtypeStruct((t // CHUNK,) + state_shape, BF16)
    fin_shape = jax.ShapeDtypeStruct(state_shape, F32)
    return pl.pallas_call(
        _state_kernel,
        grid=(nb,),
        in_specs=in_specs,
        out_specs=[
            pl.BlockSpec((cps,) + state_shape, lambda i: (i, 0, 0, 0)),
            pl.BlockSpec((cps,) + state_shape, lambda i: (nb - 1 - i, 0, 0, 0)),
            pl.BlockSpec(state_shape, lambda i: (0, 0, 0)),
            pl.BlockSpec(state_shape, lambda i: (0, 0, 0)),
        ],
        out_shape=[all_shape, all_shape, fin_shape, fin_shape],
        scratch_shapes=[pltpu.VMEM(state_shape, F32), pltpu.VMEM(state_shape, F32)],
        compiler_params=pltpu.CompilerParams(
            dimension_semantics=("arbitrary",), vmem_limit_bytes=VMEM_LIMIT_BYTES),
        name="ret_states",
    )(z, z, *consts, sf0, sb0)


def _mixer_kernel(*refs, mod_row):
    (zmix_ref, sf_ref, sb_ref, zgate_ref, x_ref,
     dmat_ref, qdf_ref, qdb_ref, retg_ref, pmask_ref, pinv_ref, poolw_ref, pscale_ref,
     sgng_ref, sgw_ref, btab_ref, wbr_ref, wbp_ref, wbs_ref, wout_ref,
     gate_ref, n2g_ref, shift2_ref, scale2_ref,
     x1_ref, h2_ref,
     y_scr, ret_cur, pool_cur, sg_cur, ret_prev, pool_prev, sg_prev) = refs
    tb = x_ref.shape[0]
    nck = tb // CHUNK
    contract_last = (((1,), (1,)), ((), ()))
    mr = slice(mod_row, mod_row + 1)

    def merge_piece(i):
        cs = slice(i * MERGE_TILE, (i + 1) * MERGE_TILE)
        def gate(branch):
            c0 = branch * D_MODEL + i * MERGE_TILE
            return jax.nn.sigmoid(zgate_ref[:, c0:c0 + MERGE_TILE].astype(F32))

        y = gate(0) * jnp.dot(ret_prev[...], wbr_ref[:, cs], preferred_element_type=F32)
        y = y + gate(1) * jnp.dot(pool_prev[...], wbp_ref[:, cs], preferred_element_type=F32)
        y = y + gate(2) * jnp.dot(sg_prev[...], wbs_ref[:, cs], preferred_element_type=F32)
        y_scr[:, cs] = y.astype(BF16)

    def out_piece(i):
        cs = slice(i * MERGE_TILE, (i + 1) * MERGE_TILE)
        out = jnp.dot(y_scr[...], wout_ref[:, cs], preferred_element_type=F32)
        x1_ref[:, cs] = x_ref[:, cs] + gate_ref[mr, cs] * out

    def ret_unit(c, h):
        rows = slice(c * CHUNK, (c + 1) * CHUNK)
        hs = slice(h * HEAD_DIM, (h + 1) * HEAD_DIM)
        head = lambda col: zmix_ref[rows, col + h * HEAD_DIM:col + (h + 1) * HEAD_DIM]
        qb = head(Q_COL)
        scores = lax.dot_general(qb, head(K_COL), contract_last, preferred_element_type=F32)
        scores = (scores * dmat_ref[h]).astype(BF16)
        cross = qdf_ref[:, hs] * jnp.dot(qb, sf_ref[c, h], preferred_element_type=F32)
        cross = cross + qdb_ref[:, hs] * jnp.dot(qb, sb_ref[c, h], preferred_element_type=F32)

        def second():
            o = jnp.dot(scores, head(V_COL), preferred_element_type=F32) + cross
            g = head(G_COL).astype(F32)
            ret = _rms(o) * retg_ref[:, hs] * (g * jax.nn.sigmoid(g))
            ret_cur[rows, hs] = ret.astype(BF16)

        return second

    def pool_unit(gi):
        gs = slice(gi * GROUP_DIM, (gi + 1) * GROUP_DIM)
        pb = zmix_ref[:, POOL_COL + gi * GROUP_DIM:POOL_COL + (gi + 1) * GROUP_DIM]
        win = jnp.dot(pmask_ref[gi], pb, preferred_element_type=F32)
        pooled = (win * pinv_ref[:, gs] - pb.astype(F32)).astype(BF16)

        def second():
            mapped = jnp.dot(pooled, poolw_ref[gi], preferred_element_type=F32)
            pool_cur[:, gs] = (mapped * pscale_ref[:, gs]).astype(BF16)

        return second

    def sg_unit(c):
        rows = slice(c * CHUNK, (c + 1) * CHUNK)
        sv = zmix_ref[rows, SV_COL:SV_COL + SG_WIDTH].astype(F32)
        sv = (_rms(jax.nn.gelu(sv)) * sgng_ref[...]).astype(BF16)

        def second():
            for gi in range(SG_GROUPS):
                gs = slice(gi * GROUP_DIM, (gi + 1) * GROUP_DIM)
                mixed = jnp.dot(sgw_ref[gi], sv[:, gs], preferred_element_type=F32)
                mixed = mixed + btab_ref[:, gs]
                u = zmix_ref[rows, U_COL + gi * GROUP_DIM:U_COL + (gi + 1) * GROUP_DIM]
                u = jax.nn.gelu(u.astype(F32))
                sg_cur[rows, gs] = (u * mixed).astype(BF16)

        return second

    mix_units = [functools.partial(ret_unit, c, h) for c in range(nck) for h in range(RET_HEADS)]
    mix_units += [functools.partial(pool_unit, gi) for gi in range(POOL_GROUPS)]
    mix_units += [functools.partial(sg_unit, c) for c in range(nck)]
    pieces = ([functools.partial(merge_piece, i) for i in range(N_MERGE)]
              + [functools.partial(out_piece, i) for i in range(N_MERGE)])

    def run(mix, merge):
        units = mix_units if mix else []
        per_piece = -(-len(units) // len(pieces))
        pending = [unit() for unit in units[:per_piece]]
        for i, piece in enumerate(pieces):
            if merge:
                piece()
            for second in pending:
                second()
            pending = [unit() for unit in units[(i + 1) * per_piece:(i + 2) * per_piece]]
        assert not pending
        if merge:
            gain2 = n2g_ref[...] * (1.0 + scale2_ref[mr, :])
            h2_ref[...] = (_rms(x1_ref[...]) * gain2 + shift2_ref[mr, :]).astype(BF16)
        if mix:
            ret_prev[...] = ret_cur[...]
            pool_prev[...] = pool_cur[...]
            sg_prev[...] = sg_cur[...]

    s = pl.program_id(0)
    last = pl.num_programs(0) - 1
    pl.when(s == 0)(functools.partial(run, True, False))
    pl.when(jnp.logical_and(s > 0, s < last))(functools.partial(run, True, True))
    pl.when(s == last)(functools.partial(run, False, True))


def _mixer(z, x, sf_all, sb_all, l, tabs, pool_tabs, p, mods_all, mod_row, tb):
    t = x.shape[0]
    nb = t // tb
    nck = tb // CHUNK
    cur = lambda s: jnp.minimum(s, nb - 1)
    prev = lambda s: jnp.maximum(s - 1, 0)

    state_spec = pl.BlockSpec((nck, RET_HEADS, HEAD_DIM, HEAD_DIM), lambda s: (cur(s), 0, 0, 0))
    row_spec = pl.BlockSpec((tb, D_MODEL), lambda s: (prev(s), 0))
    in_specs = [pl.BlockSpec((tb, HALF_WIDTH), lambda s: (cur(s), 1)), state_spec, state_spec,
                pl.BlockSpec((tb, HALF_WIDTH), lambda s: (prev(s), 0)), row_spec]
    args = [z, sf_all, sb_all, z, x]
    pmask, pinv = pool_tabs
    layer_consts = [tabs["dmat"], tabs["qdf"], tabs["qdb"], p["ret_norm_g"]]
    in_specs += [_layer_spec(a, l) for a in layer_consts] + [_const_spec(pmask), _const_spec(pinv)]
    args += layer_consts + [pmask, pinv]
    layer_consts = [p["pool_w"], p["pool_scale"], p["sg_norm_g"], p["sg_w"], tabs["btab"],
                    p["w_br"], p["w_bp"], p["w_bs"], p["w_out"]]
    in_specs += [_layer_spec(a, l) for a in layer_consts]
    args += layer_consts
    in_specs += [_mod_spec(l, 2), _gain_spec(l), _mod_spec(l, 3), _mod_spec(l, 4)]
    args += [mods_all, p["norm2_g"], mods_all, mods_all]
    mix_scratch = [pltpu.VMEM((tb, RET_WIDTH), BF16), pltpu.VMEM((tb, POOL_WIDTH), BF16),
                   pltpu.VMEM((tb, SG_WIDTH), BF16)]
    return pl.pallas_call(
        functools.partial(_mixer_kernel, mod_row=mod_row),
        grid=(nb + 1,),
        in_specs=in_specs,
        out_specs=[row_spec, row_spec],
        out_shape=[jax.ShapeDtypeStruct((t, D_MODEL), F32),
                   jax.ShapeDtypeStruct((t, D_MODEL), BF16)],
        scratch_shapes=[pltpu.VMEM((tb, D_MODEL), BF16)] + mix_scratch + mix_scratch,
        compiler_params=pltpu.CompilerParams(
            dimension_semantics=("arbitrary",), vmem_limit_bytes=VMEM_LIMIT_BYTES),
        name="mixer",
    )(*args)


EPILOGUE_ROWS = 16
CAST_EVERY = 2


def _mlp_kernel(*refs, mod_row, final, n_cast):
    refs = list(refs)
    cast_src = cast_dst = ()
    if n_cast:
        cast_dst = refs[-n_cast:]
        refs = refs[:-n_cast]
        n_out = 1 if final else 2
        cast_src = refs[-n_out - n_cast:-n_out]
        refs = refs[:-n_out - n_cast] + refs[-n_out:]
    if final:
        h2_ref, x1_ref, w1_ref, w2_ref, gate_ref, fng_ref, o_ref = refs
    else:
        (h2_ref, x1_ref, w1_ref, w2_ref, gate_ref, ng_ref, nshift_ref, nscale_ref,
         o_ref, hn_ref) = refs
    f = pl.program_id(1)
    mr = slice(mod_row, mod_row + 1)

    def step(first):
        a = jnp.maximum(jnp.dot(h2_ref[...], w1_ref[...], preferred_element_type=F32), 0.0)
        part = jnp.dot((a * a).astype(BF16), w2_ref[...], preferred_element_type=F32)
        if first:
            o_ref[...] = part
        else:
            o_ref[...] += part
        part_id = (pl.program_id(0) * pl.num_programs(1) + f) % CAST_EVERY
        for k, (src, dst) in enumerate(zip(cast_src, cast_dst)):
            share = src.shape[0] // CAST_EVERY
            rows = pl.ds(pl.multiple_of(part_id * share, share), share)
            if k == 0:
                _cast_w_in(src[rows, :], dst.at[rows])
            else:
                dst[rows, :] = src[rows, :].astype(BF16)

    @pl.when(f == 0)
    def _():
        step(True)

    @pl.when(f > 0)
    def _():
        step(False)

    @pl.when(f == pl.num_programs(1) - 1)
    def _():
        gate = gate_ref[mr, :]
        if final:
            gain = fng_ref[...]
        else:
            gain = ng_ref[...] * (1.0 + nscale_ref[mr, :])
            shift = nshift_ref[mr, :]

        for r in range(o_ref.shape[0] // EPILOGUE_ROWS):
            rows = slice(r * EPILOGUE_ROWS, (r + 1) * EPILOGUE_ROWS)
            x2 = x1_ref[rows, :] + gate * o_ref[rows, :]
            if final:
                o_ref[rows, :] = _rms(x2) * gain
            else:
                o_ref[rows, :] = x2
                hn_ref[rows, :] = (_rms(x2) * gain + shift).astype(BF16)


def _mlp(h2, x1, l, w1, w2, mods_all, mod_row, norm1_g, final_gain, tm, tf, cast_next=()):
    t = x1.shape[0]
    final = final_gain is not None
    n_f = D_FF // tf
    grid = (t // tm, n_f)
    rows = pl.BlockSpec((tm, D_MODEL), lambda i, f: (i, 0))
    in_specs = [rows, rows,
                pl.BlockSpec((D_MODEL, tf), lambda i, f: (0, f)),
                pl.BlockSpec((tf, D_MODEL), lambda i, f: (f, 0)),
                _mod_spec(l, 5)]
    args = [h2, x1, w1, w2, mods_all]
    x_shape = jax.ShapeDtypeStruct((t, D_MODEL), F32)
    if final:
        in_specs.append(pl.BlockSpec((1, D_MODEL), lambda i, f: (0, 0)))
        args.append(final_gain)
        out_specs, out_shape = [rows], [x_shape]
    else:
        in_specs += [_gain_spec(l + 1), _mod_spec(l + 1, 0), _mod_spec(l + 1, 1)]
        args += [norm1_g, mods_all, mods_all]
        out_specs = [rows, rows]
        out_shape = [x_shape, jax.ShapeDtypeStruct((t, D_MODEL), BF16)]
    for k, w in enumerate(cast_next):
        n_slabs = grid[0] * grid[1] // CAST_EVERY
        slab, cols = w.shape[1] // n_slabs, w.shape[2]
        assert slab * n_slabs == w.shape[1] and slab % (2 * SUBLANES) == 0
        out_cols = Z_WIDTH if k == 0 else cols
        in_specs.append(pl.BlockSpec((None, slab, cols),
                                     lambda i, f: (l + 1, (i * n_f + f) // CAST_EVERY, 0)))
        args.append(w)
        out_specs.append(pl.BlockSpec((slab, out_cols), lambda i, f: ((i * n_f + f) // CAST_EVERY, 0)))
        out_shape.append(jax.ShapeDtypeStruct((w.shape[1], out_cols), BF16))
    return pl.pallas_call(
        functools.partial(_mlp_kernel, mod_row=mod_row, final=final, n_cast=len(cast_next)),
        grid=grid,
        in_specs=in_specs,
        out_specs=out_specs,
        out_shape=out_shape,
        compiler_params=pltpu.CompilerParams(
            dimension_semantics=("arbitrary", "arbitrary"), vmem_limit_bytes=VMEM_LIMIT_BYTES),
        name="mlp",
    )(*args)


def _rope_tables(t):
    half = HEAD_DIM // 2
    nf = half // 2
    tok = np.arange(t)
    inv = ROPE_THETA ** (-np.arange(nf, dtype=np.float64) / nf)
    lane = np.arange(HEAD_DIM)
    pos = np.where(lane[None, :] < half, (tok // GRID_W)[:, None], (tok % GRID_W)[:, None])
    ang = pos * inv[lane % nf][None, :]
    lower = (lane % half) < nf
    sin = np.sin(ang)
    cos = np.cos(ang).astype(np.float32)
    sin_lo = np.where(lower[None, :], -sin, 0.0).astype(np.float32)
    sin_hi = np.where(lower[None, :], 0.0, sin).astype(np.float32)
    return jnp.asarray(cos), jnp.asarray(sin_lo), jnp.asarray(sin_hi)


def _pool_tables(tb, seg_len):
    pos = np.arange(tb) % seg_len
    base = np.arange(tb) - pos
    col = np.arange(tb)[None, :]
    masks, invs = [], []
    for w in POOL_WINDOWS:
        lo = np.maximum(pos - w // 2, 0)
        hi = np.minimum(pos + w // 2 - 1, seg_len - 1)
        masks.append((col >= (base + lo)[:, None]) & (col <= (base + hi)[:, None]))
        invs.append(np.repeat((1.0 / (hi - lo + 1))[:, None], GROUP_DIM, axis=1))
    pmask = jnp.asarray(np.stack(masks).astype(np.float32), dtype=BF16)
    pinv = jnp.asarray(np.concatenate(invs, axis=1).astype(np.float32))
    return pmask, pinv


def _decay_tables(logit):
    lg = jax.nn.log_sigmoid(logit.astype(F32))
    lgf, lgb = lg[0], lg[1]
    idx = jnp.arange(CHUNK, dtype=F32)
    dist = idx[:, None] - idx[None, :]
    fwd = jnp.exp(lgf[:, None, None] * jnp.maximum(dist, 0.0))
    bwd = jnp.exp(lgb[:, None, None] * jnp.maximum(-dist, 0.0))
    dmat = jnp.where(dist > 0, fwd, jnp.where(dist < 0, bwd, 2.0))

    def lanes(tab):
        return jnp.repeat(tab.T, HEAD_DIM, axis=1)

    return dict(
        dmat=dmat,
        qdf=lanes(jnp.exp(lgf[:, None] * (idx + 1.0)[None])),
        qdb=lanes(jnp.exp(lgb[:, None] * (CHUNK - idx)[None])),
        kdf=lanes(jnp.exp(lgf[:, None] * (CHUNK - 1.0 - idx)[None])),
        kdb=lanes(jnp.exp(lgb[:, None] * idx[None])),
        cdf=jnp.broadcast_to(jnp.exp(lgf * CHUNK)[:, None, None], (RET_HEADS, 1, HEAD_DIM)),
        cdb=jnp.broadcast_to(jnp.exp(lgb * CHUNK)[:, None, None], (RET_HEADS, 1, HEAD_DIM)),
    )


LATENT_ROWS_IN = 1024
LATENT_ROWS_MIX = 256
LATENT_ROWS_MLP = 512
LATENT_FF_TILE = 1024
LATENT_ROWS_NORM = 512
LATENT_STATE_CHUNKS = 4
FF_TILE = 1024


def kernel(x, c, ctx, c_ctx, w_ada, b_ada, norm1_g, w_in, ret_decay_logit, ret_norm_g, pool_w,
           pool_scale, sg_norm_g, sg_w, sg_b, w_br, w_bp, w_bs, w_out, norm2_g, w1, w2, final_norm_g):
    assert x.shape[0] == 1 and ctx.shape[0] == 1
    t = x.shape[1]
    tc = ctx.shape[1]
    xs = x[0]
    xc = ctx[0]

    cond = jnp.stack([c[0], c_ctx])[:, :, None]
    mods_all = _ada(cond, w_ada, b_ada)

    rope = _rope_tables(t)
    pool_lat = _pool_tables(LATENT_ROWS_MIX, GRID_W)
    pool_ctx = _pool_tables(tc, tc)
    zero_state = jnp.zeros((RET_HEADS, HEAD_DIM, HEAD_DIM), F32)

    tabs = jax.vmap(_decay_tables)(ret_decay_logit)
    tabs["btab"] = jnp.repeat(jnp.swapaxes(sg_b, 1, 2), GROUP_DIM, axis=2)
    vec = lambda v: v.reshape(DEPTH, 1, -1)
    p = dict(ret_norm_g=vec(ret_norm_g), pool_w=pool_w.astype(BF16), pool_scale=vec(pool_scale),
             sg_norm_g=vec(sg_norm_g), sg_w=sg_w.astype(BF16),
             w_br=w_br.astype(BF16), w_bp=w_bp.astype(BF16), w_bs=w_bs.astype(BF16),
             w_out=w_out.astype(BF16), norm2_g=vec(norm2_g))
    n1g = vec(norm1_g)
    fng = final_norm_g.reshape(1, -1)
    streamed = (w_in, w1, w2)
    w_in_b = _cast_first_w_in(w_in)
    w1_b, w2_b = w1[0].astype(BF16), w2[0].astype(BF16)

    hc = _norm_mod(xc, 0, n1g, mods_all, ROW_CONTEXT, tc)
    hx = _norm_mod(xs, 0, n1g, mods_all, ROW_LATENT, LATENT_ROWS_NORM)
    for l in range(DEPTH):
        last = l == DEPTH - 1

        zc = _in_proj(hc, w_in_b, None, tc)
        sfc, sbc, sf, sb = _states(zc, l, tabs, zero_state, zero_state, tc // CHUNK)
        if not last:
            xc1, hc2 = _mixer(zc, xc, sfc, sbc, l, tabs, pool_ctx, p, mods_all, ROW_CONTEXT, tc)
            xc, hc = _mlp(hc2, xc1, l, w1_b, w2_b, mods_all, ROW_CONTEXT, n1g, None, tc, FF_TILE)

        zx = _in_proj(hx, w_in_b, rope, LATENT_ROWS_IN)
        sfx, sbx, _, _ = _states(zx, l, tabs, sf, sb, LATENT_STATE_CHUNKS)
        x1, h2 = _mixer(zx, xs, sfx, sbx, l, tabs, pool_lat, p, mods_all, ROW_LATENT, LATENT_ROWS_MIX)
        if last:
            xs, = _mlp(h2, x1, l, w1_b, w2_b, mods_all, ROW_LATENT, n1g, fng,
                       LATENT_ROWS_MLP, LATENT_FF_TILE)
        else:
            xs, hx, w_in_b, w1_b, w2_b = _mlp(h2, x1, l, w1_b, w2_b, mods_all, ROW_LATENT, n1g, None,
                                              LATENT_ROWS_MLP, LATENT_FF_TILE, cast_next=streamed)

    return xs[None]
```

```python
import functools

import numpy as np
import jax
import jax.numpy as jnp
from jax import lax
from jax.experimental import pallas as pl
from jax.experimental.pallas import tpu as pltpu

F32 = jnp.float32
BF16 = jnp.bfloat16

D_MODEL = 2048
DEPTH = 4
GRID_W = 64
EPS = 1e-6
N_MOD = 6

HEAD_DIM = 128
RET_WIDTH = D_MODEL // 2
RET_HEADS = RET_WIDTH // HEAD_DIM
CHUNK = 128
K_SCALE = HEAD_DIM ** -0.5
ROPE_THETA = 10000.0

POOL_WIDTH = D_MODEL // 4
POOL_WINDOWS = (2, 4, 8, 16)
POOL_GROUPS = len(POOL_WINDOWS)
GROUP_DIM = POOL_WIDTH // POOL_GROUPS

SG_WIDTH = D_MODEL // 4
SG_GROUPS = 4

D_FF = 4 * D_MODEL
N_IN = 4 * RET_WIDTH + POOL_WIDTH + 2 * SG_WIDTH + 3 * D_MODEL

SUBLANES = 8
LANES = 128
VMEM_LIMIT_BYTES = 56 * 1024 * 1024

MIX_COLS = 4 * RET_WIDTH + POOL_WIDTH + 2 * SG_WIDTH
GATE_COLS = 3 * D_MODEL
IN_TILES = 2
QK_COLS = 2 * RET_WIDTH
SUB_TILE = 512
_HEADS_PER_SUB = SUB_TILE // HEAD_DIM
MERGE_TILE = 512
N_MERGE = D_MODEL // MERGE_TILE
Q_COL, K_COL, V_COL, G_COL = (k * RET_WIDTH for k in range(4))
POOL_COL = 4 * RET_WIDTH
U_COL = POOL_COL + POOL_WIDTH
SV_COL = U_COL + SG_WIDTH


def _cast_w_in(src, mix_ref, gate_ref):
    mix_ref[...] = src[:, :MIX_COLS].astype(BF16)
    gate_ref[...] = src[:, MIX_COLS:].astype(BF16)

ADA_TILE = 1024
ADA_ROWS = 8
ROW_LATENT, ROW_CONTEXT = 0, 1


def _layer_spec(arr, l):
    nd = arr.ndim - 1
    return pl.BlockSpec((None,) + arr.shape[1:], lambda *_: (l,) + (0,) * nd,
                        pipeline_mode=pl.Buffered(1))


def _const_spec(arr):
    nd = arr.ndim
    return pl.BlockSpec(arr.shape, lambda *_: (0,) * nd, pipeline_mode=pl.Buffered(1))


def _mod_spec(l, k):
    return pl.BlockSpec((None, ADA_ROWS, D_MODEL), lambda *_: (l, 0, k))


def _gain_spec(l):
    return pl.BlockSpec((None, 1, D_MODEL), lambda *_: (l, 0, 0))


def _rms(x):
    return x * lax.rsqrt(jnp.mean(x * x, axis=-1, keepdims=True) + EPS)


def _norm_modulate(x, gain, shift, scale):
    return _rms(x) * gain * (1.0 + scale) + shift


def _ada_kernel(cond_ref, w_ref, b_ref, o_ref, act_scr):
    tn = w_ref.shape[1]

    @pl.when(jnp.logical_and(pl.program_id(0) == 0, pl.program_id(1) == 0))
    def _():
        for r in range(2):
            a = cond_ref[r]
            act_scr[r] = jnp.broadcast_to(a * jax.nn.sigmoid(a), (D_MODEL, LANES))

    def body(g, accs):
        r0 = pl.multiple_of(g * SUBLANES, SUBLANES)
        w = w_ref[pl.ds(r0, SUBLANES), :]
        out = []
        for r in range(2):
            a = jnp.tile(act_scr[r, pl.ds(r0, SUBLANES), :], (1, tn // LANES))
            out.append(accs[r] + w * a)
        return tuple(out)

    zero = jnp.zeros((SUBLANES, tn), F32)
    accs = lax.fori_loop(0, D_MODEL // SUBLANES, body, (zero, zero), unroll=8)
    o_ref[...] = jnp.zeros((ADA_ROWS, tn), F32)
    for r in range(2):
        o_ref[r:r + 1, :] = jnp.sum(accs[r], axis=0, keepdims=True) + b_ref[...]


def _ada(cond, w_ada, b_ada):
    n = N_MOD * D_MODEL
    return pl.pallas_call(
        _ada_kernel,
        grid=(DEPTH, n // ADA_TILE),
        in_specs=[
            pl.BlockSpec((2, D_MODEL, 1), lambda l, j: (0, 0, 0)),
            pl.BlockSpec((None, D_MODEL, ADA_TILE), lambda l, j: (l, 0, j)),
            pl.BlockSpec((None, 1, ADA_TILE), lambda l, j: (l, 0, j)),
        ],
        out_specs=pl.BlockSpec((None, ADA_ROWS, ADA_TILE), lambda l, j: (l, 0, j)),
        out_shape=jax.ShapeDtypeStruct((DEPTH, ADA_ROWS, n), F32),
        scratch_shapes=[pltpu.VMEM((2, D_MODEL, LANES), F32)],
        compiler_params=pltpu.CompilerParams(
            dimension_semantics=("arbitrary", "arbitrary"), vmem_limit_bytes=VMEM_LIMIT_BYTES),
        name="ada",
    )(cond, w_ada, b_ada.reshape(DEPTH, 1, n))


def _norm_kernel(x_ref, g_ref, shift_ref, scale_ref, h_ref, *, mod_row):
    mr = slice(mod_row, mod_row + 1)
    h_ref[...] = _norm_modulate(x_ref[...], g_ref[...], shift_ref[mr, :], scale_ref[mr, :]).astype(BF16)


def _norm_mod(x, l, norm1_g, mods_all, mod_row, tm):
    t = x.shape[0]
    return pl.pallas_call(
        functools.partial(_norm_kernel, mod_row=mod_row),
        grid=(t // tm,),
        in_specs=[pl.BlockSpec((tm, D_MODEL), lambda i: (i, 0)),
                  _gain_spec(l), _mod_spec(l, 0), _mod_spec(l, 1)],
        out_specs=pl.BlockSpec((tm, D_MODEL), lambda i: (i, 0)),
        out_shape=jax.ShapeDtypeStruct((t, D_MODEL), BF16),
        compiler_params=pltpu.CompilerParams(
            dimension_semantics=("arbitrary",), vmem_limit_bytes=VMEM_LIMIT_BYTES),
        name="norm_mod",
    )(x, norm1_g, mods_all, mods_all)


CAST_ROWS = 128


def _cast_w_in_kernel(src_ref, mix_ref, gate_ref):
    _cast_w_in(src_ref[...], mix_ref, gate_ref)


def _cast_first_w_in(w_in):
    return pl.pallas_call(
        _cast_w_in_kernel,
        grid=(D_MODEL // CAST_ROWS,),
        in_specs=[pl.BlockSpec((None, CAST_ROWS, N_IN), lambda i: (0, i, 0))],
        out_specs=[pl.BlockSpec((CAST_ROWS, MIX_COLS), lambda i: (i, 0)),
                   pl.BlockSpec((CAST_ROWS, GATE_COLS), lambda i: (i, 0))],
        out_shape=[jax.ShapeDtypeStruct((D_MODEL, MIX_COLS), BF16),
                   jax.ShapeDtypeStruct((D_MODEL, GATE_COLS), BF16)],
        compiler_params=pltpu.CompilerParams(
            dimension_semantics=("arbitrary",), vmem_limit_bytes=VMEM_LIMIT_BYTES),
        name="cast_w_in",
    )(w_in)


def _rope(t, cos, sin_lo, sin_hi):
    up = pltpu.roll(t, HEAD_DIM - 32, axis=1)
    down = pltpu.roll(t, 32, axis=1)
    return t * cos + up * sin_lo + down * sin_hi


def _inproj_kernel(*refs, mix_part, use_rope):
    if use_rope:
        h_ref, w_ref, cos_ref, slo_ref, shi_ref, z_ref = refs
    else:
        h_ref, w_ref, z_ref = refs
    j = pl.program_id(1)

    if not mix_part:
        z_ref[...] = jnp.dot(h_ref[...], w_ref[...], preferred_element_type=F32).astype(BF16)
        return

    @pl.when(j == 0)
    def _():
        for blk in range(QK_COLS // SUB_TILE):
            c0 = blk * SUB_TILE
            acc = jnp.dot(h_ref[...], w_ref[:, c0:c0 + SUB_TILE], preferred_element_type=F32)
            if c0 >= RET_WIDTH:
                acc = acc * K_SCALE
            if use_rope:
                cos, slo, shi = cos_ref[...], slo_ref[...], shi_ref[...]
                for hh in range(_HEADS_PER_SUB):
                    hs = slice(hh * HEAD_DIM, (hh + 1) * HEAD_DIM)
                    z_ref[:, c0 + hh * HEAD_DIM:c0 + (hh + 1) * HEAD_DIM] = (
                        _rope(acc[:, hs], cos, slo, shi).astype(BF16))
            else:
                z_ref[:, c0:c0 + SUB_TILE] = acc.astype(BF16)
        z_ref[:, QK_COLS:] = jnp.dot(h_ref[...], w_ref[:, QK_COLS:],
                                     preferred_element_type=F32).astype(BF16)

    @pl.when(j != 0)
    def _():
        z_ref[...] = jnp.dot(h_ref[...], w_ref[...], preferred_element_type=F32).astype(BF16)


def _in_proj(h, w_part, mix_part, rope, tm):
    t = h.shape[0]
    width = w_part.shape[1]
    tile = width // IN_TILES
    assert tile * IN_TILES == width and tile % LANES == 0 and (not mix_part or tile >= QK_COLS)
    use_rope = mix_part and rope is not None
    in_specs = [
        pl.BlockSpec((tm, D_MODEL), lambda i, j: (i, 0)),
        pl.BlockSpec((D_MODEL, tile), lambda i, j: (0, j)),
    ]
    args = [h, w_part]
    if use_rope:
        in_specs += [pl.BlockSpec((tm, HEAD_DIM), lambda i, j: (i, 0))] * 3
        args += list(rope)
    return pl.pallas_call(
        functools.partial(_inproj_kernel, mix_part=mix_part, use_rope=use_rope),
        grid=(t // tm, IN_TILES),
        in_specs=in_specs,
        out_specs=pl.BlockSpec((tm, tile), lambda i, j: (i, j)),
        out_shape=jax.ShapeDtypeStruct((t, width), BF16),
        compiler_params=pltpu.CompilerParams(
            dimension_semantics=("arbitrary", "arbitrary"), vmem_limit_bytes=VMEM_LIMIT_BYTES),
        name="in_proj_mix" if mix_part else "in_proj_gate",
    )(*args)


def _state_kernel(k_ref, v_ref, kdf_ref, kdb_ref, cdf_ref, cdb_ref, sf0_ref, sb0_ref,
                  sf_all_ref, sb_all_ref, sf_fin_ref, sb_fin_ref, sf_scr, sb_scr):
    t = pl.program_id(0)
    nb = pl.num_programs(0)
    cps = sf_all_ref.shape[0]

    @pl.when(t == 0)
    def _():
        sf_scr[...] = sf0_ref[...]
        sb_scr[...] = sb0_ref[...]

    contract_rows = (((0,), (0,)), ((), ()))

    def advance(block, c, kd_ref, cd_ref, all_ref, scr):
        rows = pl.ds(pl.multiple_of((block * cps + c) * CHUNK, CHUNK), CHUNK)
        all_ref[c] = scr[...].astype(BF16)
        for h in range(RET_HEADS):
            hs = slice(h * HEAD_DIM, (h + 1) * HEAD_DIM)
            kd = (k_ref[rows, hs].astype(F32) * kd_ref[:, hs]).astype(BF16)
            kv = lax.dot_general(kd, v_ref[rows, hs], contract_rows, preferred_element_type=F32)
            scr[h] = scr[h] * cd_ref[h] + kv

    for c in range(cps):
        advance(t, c, kdf_ref, cdf_ref, sf_all_ref, sf_scr)
        advance(nb - 1 - t, cps - 1 - c, kdb_ref, cdb_ref, sb_all_ref, sb_scr)

    @pl.when(t == pl.num_programs(0) - 1)
    def _():
        sf_fin_ref[...] = sf_scr[...]
        sb_fin_ref[...] = sb_scr[...]


def _states(z, l, tabs, sf0, sb0, cps):
    t = z.shape[0]
    nb = t // (cps * CHUNK)
    whole = lambda c: pl.BlockSpec((t, RET_WIDTH), lambda i: (0, c), pipeline_mode=pl.Buffered(1))
    state_shape = (RET_HEADS, HEAD_DIM, HEAD_DIM)
    consts = [tabs["kdf"], tabs["kdb"], tabs["cdf"], tabs["cdb"]]
    kb, vb = K_COL // RET_WIDTH, V_COL // RET_WIDTH
    in_specs = [whole(kb), whole(vb)] + [_layer_spec(a, l) for a in consts]
    in_specs += [_const_spec(sf0), _const_spec(sb0)]
    all_shape = jax.ShapeDtypeStruct((t // CHUNK,) + state_shape, BF16)
    fin_shape = jax.ShapeDtypeStruct(state_shape, F32)
    return pl.pallas_call(
        _state_kernel,
        grid=(nb,),
        in_specs=in_specs,
        out_specs=[
            pl.BlockSpec((cps,) + state_shape, lambda i: (i, 0, 0, 0)),
            pl.BlockSpec((cps,) + state_shape, lambda i: (nb - 1 - i, 0, 0, 0)),
            pl.BlockSpec(state_shape, lambda i: (0, 0, 0)),
            pl.BlockSpec(state_shape, lambda i: (0, 0, 0)),
        ],
        out_shape=[all_shape, all_shape, fin_shape, fin_shape],
        scratch_shapes=[pltpu.VMEM(state_shape, F32), pltpu.VMEM(state_shape, F32)],
        compiler_params=pltpu.CompilerParams(
            dimension_semantics=("arbitrary",), vmem_limit_bytes=VMEM_LIMIT_BYTES),
        name="ret_states",
    )(z, z, *consts, sf0, sb0)


def _mixer_kernel(*refs, mod_row):
    (zmix_ref, sf_ref, sb_ref, zgate_ref, x_ref,
     dmat_ref, qdf_ref, qdb_ref, retg_ref, pmask_ref, pinv_ref, poolw_ref, pscale_ref,
     sgng_ref, sgw_ref, btab_ref, wbr_ref, wbp_ref, wbs_ref, wout_ref,
     gate_ref, n2g_ref, shift2_ref, scale2_ref,
     x1_ref, h2_ref,
     y_scr, ret_cur, pool_cur, sg_cur, ret_prev, pool_prev, sg_prev) = refs
    tb = x_ref.shape[0]
    nck = tb // CHUNK
    contract_last = (((1,), (1,)), ((), ()))
    mr = slice(mod_row, mod_row + 1)

    def merge_piece(i):
        cs = slice(i * MERGE_TILE, (i + 1) * MERGE_TILE)
        def gate(branch):
            c0 = branch * D_MODEL + i * MERGE_TILE
            return jax.nn.sigmoid(zgate_ref[:, c0:c0 + MERGE_TILE].astype(F32))

        y = gate(0) * jnp.dot(ret_prev[...], wbr_ref[:, cs], preferred_element_type=F32)
        y = y + gate(1) * jnp.dot(pool_prev[...], wbp_ref[:, cs], preferred_element_type=F32)
        y = y + gate(2) * jnp.dot(sg_prev[...], wbs_ref[:, cs], preferred_element_type=F32)
        y_scr[:, cs] = y.astype(BF16)

    def out_piece(i):
        cs = slice(i * MERGE_TILE, (i + 1) * MERGE_TILE)
        out = jnp.dot(y_scr[...], wout_ref[:, cs], preferred_element_type=F32)
        x1_ref[:, cs] = x_ref[:, cs] + gate_ref[mr, cs] * out

    def ret_unit(c, h):
        rows = slice(c * CHUNK, (c + 1) * CHUNK)
        hs = slice(h * HEAD_DIM, (h + 1) * HEAD_DIM)
        head = lambda col: zmix_ref[rows, col + h * HEAD_DIM:col + (h + 1) * HEAD_DIM]
        qb = head(Q_COL)
        scores = lax.dot_general(qb, head(K_COL), contract_last, preferred_element_type=F32)
        scores = (scores * dmat_ref[h]).astype(BF16)
        cross = qdf_ref[:, hs] * jnp.dot(qb, sf_ref[c, h], preferred_element_type=F32)
        cross = cross + qdb_ref[:, hs] * jnp.dot(qb, sb_ref[c, h], preferred_element_type=F32)

        def second():
            o = jnp.dot(scores, head(V_COL), preferred_element_type=F32) + cross
            g = head(G_COL).astype(F32)
            ret = _rms(o) * retg_ref[:, hs] * (g * jax.nn.sigmoid(g))
            ret_cur[rows, hs] = ret.astype(BF16)

        return second

    def pool_unit(gi):
        gs = slice(gi * GROUP_DIM, (gi + 1) * GROUP_DIM)
        pb = zmix_ref[:, POOL_COL + gi * GROUP_DIM:POOL_COL + (gi + 1) * GROUP_DIM]
        win = jnp.dot(pmask_ref[gi], pb, preferred_element_type=F32)
        pooled = (win * pinv_ref[:, gs] - pb.astype(F32)).astype(BF16)

        def second():
            mapped = jnp.dot(pooled, poolw_ref[gi], preferred_element_type=F32)
            pool_cur[:, gs] = (mapped * pscale_ref[:, gs]).astype(BF16)

        return second

    def sg_unit(c):
        rows = slice(c * CHUNK, (c + 1) * CHUNK)
        sv = zmix_ref[rows, SV_COL:SV_COL + SG_WIDTH].astype(F32)
        sv = (_rms(jax.nn.gelu(sv)) * sgng_ref[...]).astype(BF16)

        def second():
            for gi in range(SG_GROUPS):
                gs = slice(gi * GROUP_DIM, (gi + 1) * GROUP_DIM)
                mixed = jnp.dot(sgw_ref[gi], sv[:, gs], preferred_element_type=F32)
                mixed = mixed + btab_ref[:, gs]
                u = zmix_ref[rows, U_COL + gi * GROUP_DIM:U_COL + (gi + 1) * GROUP_DIM]
                u = jax.nn.gelu(u.astype(F32))
                sg_cur[rows, gs] = (u * mixed).astype(BF16)

        return second

    mix_units = [functools.partial(ret_unit, c, h) for c in range(nck) for h in range(RET_HEADS)]
    mix_units += [functools.partial(pool_unit, gi) for gi in range(POOL_GROUPS)]
    mix_units += [functools.partial(sg_unit, c) for c in range(nck)]
    pieces = ([functools.partial(merge_piece, i) for i in range(N_MERGE)]
              + [functools.partial(out_piece, i) for i in range(N_MERGE)])

    def run(mix, merge):
        units = mix_units if mix else []
        per_piece = -(-len(units) // len(pieces))
        pending = [unit() for unit in units[:per_piece]]
        for i, piece in enumerate(pieces):
            if merge:
                piece()
            for second in pending:
                second()
            pending = [unit() for unit in units[(i + 1) * per_piece:(i + 2) * per_piece]]
        assert not pending
        if merge:
            gain2 = n2g_ref[...] * (1.0 + scale2_ref[mr, :])
            h2_ref[...] = (_rms(x1_ref[...]) * gain2 + shift2_ref[mr, :]).astype(BF16)
        if mix:
            ret_prev[...] = ret_cur[...]
            pool_prev[...] = pool_cur[...]
            sg_prev[...] = sg_cur[...]

    s = pl.program_id(0)
    last = pl.num_programs(0) - 1
    pl.when(s == 0)(functools.partial(run, True, False))
    pl.when(jnp.logical_and(s > 0, s < last))(functools.partial(run, True, True))
    pl.when(s == last)(functools.partial(run, False, True))


def _mixer(z_mix, z_gate, x, sf_all, sb_all, l, tabs, pool_tabs, p, mods_all, mod_row, tb):
    t = x.shape[0]
    nb = t // tb
    nck = tb // CHUNK
    cur = lambda s: jnp.minimum(s, nb - 1)
    prev = lambda s: jnp.maximum(s - 1, 0)

    state_spec = pl.BlockSpec((nck, RET_HEADS, HEAD_DIM, HEAD_DIM), lambda s: (cur(s), 0, 0, 0))
    row_spec = pl.BlockSpec((tb, D_MODEL), lambda s: (prev(s), 0))
    in_specs = [pl.BlockSpec((tb, MIX_COLS), lambda s: (cur(s), 0)), state_spec, state_spec,
                pl.BlockSpec((tb, GATE_COLS), lambda s: (prev(s), 0)), row_spec]
    args = [z_mix, sf_all, sb_all, z_gate, x]
    pmask, pinv = pool_tabs
    layer_consts = [tabs["dmat"], tabs["qdf"], tabs["qdb"], p["ret_norm_g"]]
    in_specs += [_layer_spec(a, l) for a in layer_consts] + [_const_spec(pmask), _const_spec(pinv)]
    args += layer_consts + [pmask, pinv]
    layer_consts = [p["pool_w"], p["pool_scale"], p["sg_norm_g"], p["sg_w"], tabs["btab"],
                    p["w_br"], p["w_bp"], p["w_bs"], p["w_out"]]
    in_specs += [_layer_spec(a, l) for a in layer_consts]
    args += layer_consts
    in_specs += [_mod_spec(l, 2), _gain_spec(l), _mod_spec(l, 3), _mod_spec(l, 4)]
    args += [mods_all, p["norm2_g"], mods_all, mods_all]
    mix_scratch = [pltpu.VMEM((tb, RET_WIDTH), BF16), pltpu.VMEM((tb, POOL_WIDTH), BF16),
                   pltpu.VMEM((tb, SG_WIDTH), BF16)]
    return pl.pallas_call(
        functools.partial(_mixer_kernel, mod_row=mod_row),
        grid=(nb + 1,),
        in_specs=in_specs,
        out_specs=[row_spec, row_spec],
        out_shape=[jax.ShapeDtypeStruct((t, D_MODEL), F32),
                   jax.ShapeDtypeStruct((t, D_MODEL), BF16)],
        scratch_shapes=[pltpu.VMEM((tb, D_MODEL), BF16)] + mix_scratch + mix_scratch,
        compiler_params=pltpu.CompilerParams(
            dimension_semantics=("arbitrary",), vmem_limit_bytes=VMEM_LIMIT_BYTES),
        name="mixer",
    )(*args)


EPILOGUE_ROWS = 16
CAST_EVERY = 2


def _mlp_kernel(*refs, mod_row, final, n_cast):
    refs = list(refs)
    cast_src = cast_dst = ()
    if n_cast:
        n_dst = n_cast + 1
        cast_dst = refs[-n_dst:]
        refs = refs[:-n_dst]
        n_out = 1 if final else 2
        cast_src = refs[-n_out - n_cast:-n_out]
        refs = refs[:-n_out - n_cast] + refs[-n_out:]
    if final:
        h2_ref, x1_ref, w1_ref, w2_ref, gate_ref, fng_ref, o_ref = refs
    else:
        (h2_ref, x1_ref, w1_ref, w2_ref, gate_ref, ng_ref, nshift_ref, nscale_ref,
         o_ref, hn_ref) = refs
    f = pl.program_id(1)
    mr = slice(mod_row, mod_row + 1)

    def step(first):
        a = jnp.maximum(jnp.dot(h2_ref[...], w1_ref[...], preferred_element_type=F32), 0.0)
        part = jnp.dot((a * a).astype(BF16), w2_ref[...], preferred_element_type=F32)
        if first:
            o_ref[...] = part
        else:
            o_ref[...] += part
        part_id = (pl.program_id(0) * pl.num_programs(1) + f) % CAST_EVERY
        for k, src in enumerate(cast_src):
            share = src.shape[0] // CAST_EVERY
            rows = pl.ds(pl.multiple_of(part_id * share, share), share)
            if k == 0:
                _cast_w_in(src[rows, :], cast_dst[0].at[rows], cast_dst[1].at[rows])
            else:
                cast_dst[k + 1][rows, :] = src[rows, :].astype(BF16)

    @pl.when(f == 0)
    def _():
        step(True)

    @pl.when(f > 0)
    def _():
        step(False)

    @pl.when(f == pl.num_programs(1) - 1)
    def _():
        gate = gate_ref[mr, :]
        if final:
            gain = fng_ref[...]
        else:
            gain = ng_ref[...] * (1.0 + nscale_ref[mr, :])
            shift = nshift_ref[mr, :]

        for r in range(o_ref.shape[0] // EPILOGUE_ROWS):
            rows = slice(r * EPILOGUE_ROWS, (r + 1) * EPILOGUE_ROWS)
            x2 = x1_ref[rows, :] + gate * o_ref[rows, :]
            if final:
                o_ref[rows, :] = _rms(x2) * gain
            else:
                o_ref[rows, :] = x2
                hn_ref[rows, :] = (_rms(x2) * gain + shift).astype(BF16)


def _mlp(h2, x1, l, w1, w2, mods_all, mod_row, norm1_g, final_gain, tm, tf, cast_next=()):
    t = x1.shape[0]
    final = final_gain is not None
    n_f = D_FF // tf
    grid = (t // tm, n_f)
    rows = pl.BlockSpec((tm, D_MODEL), lambda i, f: (i, 0))
    in_specs = [rows, rows,
                pl.BlockSpec((D_MODEL, tf), lambda i, f: (0, f)),
                pl.BlockSpec((tf, D_MODEL), lambda i, f: (f, 0)),
                _mod_spec(l, 5)]
    args = [h2, x1, w1, w2, mods_all]
    x_shape = jax.ShapeDtypeStruct((t, D_MODEL), F32)
    if final:
        in_specs.append(pl.BlockSpec((1, D_MODEL), lambda i, f: (0, 0)))
        args.append(final_gain)
        out_specs, out_shape = [rows], [x_shape]
    else:
        in_specs += [_gain_spec(l + 1), _mod_spec(l + 1, 0), _mod_spec(l + 1, 1)]
        args += [norm1_g, mods_all, mods_all]
        out_specs = [rows, rows]
        out_shape = [x_shape, jax.ShapeDtypeStruct((t, D_MODEL), BF16)]
    for k, w in enumerate(cast_next):
        n_slabs = grid[0] * grid[1] // CAST_EVERY
        slab, cols = w.shape[1] // n_slabs, w.shape[2]
        assert slab * n_slabs == w.shape[1] and slab % (2 * SUBLANES) == 0
        in_specs.append(pl.BlockSpec((None, slab, cols),
                                     lambda i, f: (l + 1, (i * n_f + f) // CAST_EVERY, 0)))
        args.append(w)
        for out_cols in ((MIX_COLS, GATE_COLS) if k == 0 else (cols,)):
            out_specs.append(pl.BlockSpec((slab, out_cols),
                                          lambda i, f: ((i * n_f + f) // CAST_EVERY, 0)))
            out_shape.append(jax.ShapeDtypeStruct((w.shape[1], out_cols), BF16))
    return pl.pallas_call(
        functools.partial(_mlp_kernel, mod_row=mod_row, final=final, n_cast=len(cast_next)),
        grid=grid,
        in_specs=in_specs,
        out_specs=out_specs,
        out_shape=out_shape,
        compiler_params=pltpu.CompilerParams(
            dimension_semantics=("arbitrary", "arbitrary"), vmem_limit_bytes=VMEM_LIMIT_BYTES),
        name="mlp",
    )(*args)


def _rope_tables(t):
    half = HEAD_DIM // 2
    nf = half // 2
    tok = np.arange(t)
    inv = ROPE_THETA ** (-np.arange(nf, dtype=np.float64) / nf)
    lane = np.arange(HEAD_DIM)
    pos = np.where(lane[None, :] < half, (tok // GRID_W)[:, None], (tok % GRID_W)[:, None])
    ang = pos * inv[lane % nf][None, :]
    lower = (lane % half) < nf
    sin = np.sin(ang)
    cos = np.cos(ang).astype(np.float32)
    sin_lo = np.where(lower[None, :], -sin, 0.0).astype(np.float32)
    sin_hi = np.where(lower[None, :], 0.0, sin).astype(np.float32)
    return jnp.asarray(cos), jnp.asarray(sin_lo), jnp.asarray(sin_hi)


def _pool_tables(tb, seg_len):
    pos = np.arange(tb) % seg_len
    base = np.arange(tb) - pos
    col = np.arange(tb)[None, :]
    masks, invs = [], []
    for w in POOL_WINDOWS:
        lo = np.maximum(pos - w // 2, 0)
        hi = np.minimum(pos + w // 2 - 1, seg_len - 1)
        masks.append((col >= (base + lo)[:, None]) & (col <= (base + hi)[:, None]))
        invs.append(np.repeat((1.0 / (hi - lo + 1))[:, None], GROUP_DIM, axis=1))
    pmask = jnp.asarray(np.stack(masks).astype(np.float32), dtype=BF16)
    pinv = jnp.asarray(np.concatenate(invs, axis=1).astype(np.float32))
    return pmask, pinv


def _decay_tables(logit):
    lg = jax.nn.log_sigmoid(logit.astype(F32))
    lgf, lgb = lg[0], lg[1]
    idx = jnp.arange(CHUNK, dtype=F32)
    dist = idx[:, None] - idx[None, :]
    fwd = jnp.exp(lgf[:, None, None] * jnp.maximum(dist, 0.0))
    bwd = jnp.exp(lgb[:, None, None] * jnp.maximum(-dist, 0.0))
    dmat = jnp.where(dist > 0, fwd, jnp.where(dist < 0, bwd, 2.0))

    def lanes(tab):
        return jnp.repeat(tab.T, HEAD_DIM, axis=1)

    return dict(
        dmat=dmat,
        qdf=lanes(jnp.exp(lgf[:, None] * (idx + 1.0)[None])),
        qdb=lanes(jnp.exp(lgb[:, None] * (CHUNK - idx)[None])),
        kdf=lanes(jnp.exp(lgf[:, None] * (CHUNK - 1.0 - idx)[None])),
        kdb=lanes(jnp.exp(lgb[:, None] * idx[None])),
        cdf=jnp.broadcast_to(jnp.exp(lgf * CHUNK)[:, None, None], (RET_HEADS, 1, HEAD_DIM)),
        cdb=jnp.broadcast_to(jnp.exp(lgb * CHUNK)[:, None, None], (RET_HEADS, 1, HEAD_DIM)),
    )


LATENT_ROWS_IN = 1024
LATENT_ROWS_MIX = 256
LATENT_ROWS_MLP = 512
LATENT_FF_TILE = 1024
LATENT_ROWS_NORM = 512
LATENT_STATE_CHUNKS = 4
FF_TILE = 1024


def kernel(x, c, ctx, c_ctx, w_ada, b_ada, norm1_g, w_in, ret_decay_logit, ret_norm_g, pool_w,
           pool_scale, sg_norm_g, sg_w, sg_b, w_br, w_bp, w_bs, w_out, norm2_g, w1, w2, final_norm_g):
    assert x.shape[0] == 1 and ctx.shape[0] == 1
    t = x.shape[1]
    tc = ctx.shape[1]
    xs = x[0]
    xc = ctx[0]

    cond = jnp.stack([c[0], c_ctx])[:, :, None]
    mods_all = _ada(cond, w_ada, b_ada)

    rope = _rope_tables(t)
    pool_lat = _pool_tables(LATENT_ROWS_MIX, GRID_W)
    pool_ctx = _pool_tables(tc, tc)
    zero_state = jnp.zeros((RET_HEADS, HEAD_DIM, HEAD_DIM), F32)

    tabs = jax.vmap(_decay_tables)(ret_decay_logit)
    tabs["btab"] = jnp.repeat(jnp.swapaxes(sg_b, 1, 2), GROUP_DIM, axis=2)
    vec = lambda v: v.reshape(DEPTH, 1, -1)
    p = dict(ret_norm_g=vec(ret_norm_g), pool_w=pool_w.astype(BF16), pool_scale=vec(pool_scale),
             sg_norm_g=vec(sg_norm_g), sg_w=sg_w.astype(BF16),
             w_br=w_br.astype(BF16), w_bp=w_bp.astype(BF16), w_bs=w_bs.astype(BF16),
             w_out=w_out.astype(BF16), norm2_g=vec(norm2_g))
    n1g = vec(norm1_g)
    fng = final_norm_g.reshape(1, -1)
    streamed = (w_in, w1, w2)
    w_mix_b, w_gate_b = _cast_first_w_in(w_in)
    w1_b, w2_b = w1[0].astype(BF16), w2[0].astype(BF16)

    hc = _norm_mod(xc, 0, n1g, mods_all, ROW_CONTEXT, tc)
    hx = _norm_mod(xs, 0, n1g, mods_all, ROW_LATENT, LATENT_ROWS_NORM)
    for l in range(DEPTH):
        last = l == DEPTH - 1

        zc = _in_proj(hc, w_mix_b, True, None, tc)
        sfc, sbc, sf, sb = _states(zc, l, tabs, zero_state, zero_state, tc // CHUNK)
        if not last:
            gc = _in_proj(hc, w_gate_b, False, None, tc)
            xc1, hc2 = _mixer(zc, gc, xc, sfc, sbc, l, tabs, pool_ctx, p, mods_all, ROW_CONTEXT, tc)
            xc, hc = _mlp(hc2, xc1, l, w1_b, w2_b, mods_all, ROW_CONTEXT, n1g, None, tc, FF_TILE)

        zx = _in_proj(hx, w_mix_b, True, rope, LATENT_ROWS_IN)
        gx = _in_proj(hx, w_gate_b, False, None, LATENT_ROWS_IN)
        sfx, sbx, _, _ = _states(zx, l, tabs, sf, sb, LATENT_STATE_CHUNKS)
        x1, h2 = _mixer(zx, gx, xs, sfx, sbx, l, tabs, pool_lat, p, mods_all, ROW_LATENT,
                        LATENT_ROWS_MIX)
        if last:
            xs, = _mlp(h2, x1, l, w1_b, w2_b, mods_all, ROW_LATENT, n1g, fng,
                       LATENT_ROWS_MLP, LATENT_FF_TILE)
        else:
            xs, hx, w_mix_b, w_gate_b, w1_b, w2_b = _mlp(
                h2, x1, l, w1_b, w2_b, mods_all, ROW_LATENT, n1g, None,
                LATENT_ROWS_MLP, LATENT_FF_TILE, cast_next=streamed)

    return xs[None]
```

```python
import functools

import numpy as np
import jax
import jax.numpy as jnp
from jax import lax
from jax.experimental import pallas as pl
from jax.experimental.pallas import tpu as pltpu

F32 = jnp.float32
BF16 = jnp.bfloat16

D_MODEL = 2048
DEPTH = 4
GRID_W = 64
EPS = 1e-6
N_MOD = 6

HEAD_DIM = 128
RET_WIDTH = D_MODEL // 2
RET_HEADS = RET_WIDTH // HEAD_DIM
CHUNK = 128
K_SCALE = HEAD_DIM ** -0.5
ROPE_THETA = 10000.0

POOL_WIDTH = D_MODEL // 4
POOL_WINDOWS = (2, 4, 8, 16)
POOL_GROUPS = len(POOL_WINDOWS)
GROUP_DIM = POOL_WIDTH // POOL_GROUPS

SG_WIDTH = D_MODEL // 4
SG_GROUPS = 4

D_FF = 4 * D_MODEL
N_IN = 4 * RET_WIDTH + POOL_WIDTH + 2 * SG_WIDTH + 3 * D_MODEL

SUBLANES = 8
LANES = 128
VMEM_LIMIT_BYTES = 56 * 1024 * 1024

MIX_COLS = 4 * RET_WIDTH + POOL_WIDTH + 2 * SG_WIDTH
GATE_COLS = 3 * D_MODEL
IN_TILES = 2
QK_COLS = 2 * RET_WIDTH
SUB_TILE = 512
_HEADS_PER_SUB = SUB_TILE // HEAD_DIM
MERGE_TILE = 512
N_MERGE = D_MODEL // MERGE_TILE
Q_COL, K_COL, V_COL, G_COL = (k * RET_WIDTH for k in range(4))
POOL_COL = 4 * RET_WIDTH
U_COL = POOL_COL + POOL_WIDTH
SV_COL = U_COL + SG_WIDTH


def _cast_w_in(src, mix_ref, gate_ref):
    mix_ref[...] = src[:, :MIX_COLS].astype(BF16)
    gate_ref[...] = src[:, MIX_COLS:].astype(BF16)

ADA_TILE = 1024
ADA_ROWS = 8
ROW_LATENT, ROW_CONTEXT = 0, 1


def _layer_spec(arr, l):
    nd = arr.ndim - 1
    return pl.BlockSpec((None,) + arr.shape[1:], lambda *_: (l,) + (0,) * nd,
                        pipeline_mode=pl.Buffered(1))


def _const_spec(arr):
    nd = arr.ndim
    return pl.BlockSpec(arr.shape, lambda *_: (0,) * nd, pipeline_mode=pl.Buffered(1))


def _mod_spec(l, k):
    return pl.BlockSpec((None, ADA_ROWS, D_MODEL), lambda *_: (l, 0, k))


def _gain_spec(l):
    return pl.BlockSpec((None, 1, D_MODEL), lambda *_: (l, 0, 0))


def _rms(x):
    return x * lax.rsqrt(jnp.mean(x * x, axis=-1, keepdims=True) + EPS)


def _norm_modulate(x, gain, shift, scale):
    return _rms(x) * gain * (1.0 + scale) + shift


def _ada_kernel(cond_ref, w_ref, b_ref, o_ref, act_scr):
    tn = w_ref.shape[1]

    @pl.when(jnp.logical_and(pl.program_id(0) == 0, pl.program_id(1) == 0))
    def _():
        for r in range(2):
            a = cond_ref[r]
            act_scr[r] = jnp.broadcast_to(a * jax.nn.sigmoid(a), (D_MODEL, LANES))

    def body(g, accs):
        r0 = pl.multiple_of(g * SUBLANES, SUBLANES)
        w = w_ref[pl.ds(r0, SUBLANES), :]
        out = []
        for r in range(2):
            a = jnp.tile(act_scr[r, pl.ds(r0, SUBLANES), :], (1, tn // LANES))
            out.append(accs[r] + w * a)
        return tuple(out)

    zero = jnp.zeros((SUBLANES, tn), F32)
    accs = lax.fori_loop(0, D_MODEL // SUBLANES, body, (zero, zero), unroll=8)
    o_ref[...] = jnp.zeros((ADA_ROWS, tn), F32)
    for r in range(2):
        o_ref[r:r + 1, :] = jnp.sum(accs[r], axis=0, keepdims=True) + b_ref[...]


def _ada(cond, w_ada, b_ada):
    n = N_MOD * D_MODEL
    return pl.pallas_call(
        _ada_kernel,
        grid=(DEPTH, n // ADA_TILE),
        in_specs=[
            pl.BlockSpec((2, D_MODEL, 1), lambda l, j: (0, 0, 0)),
            pl.BlockSpec((None, D_MODEL, ADA_TILE), lambda l, j: (l, 0, j)),
            pl.BlockSpec((None, 1, ADA_TILE), lambda l, j: (l, 0, j)),
        ],
        out_specs=pl.BlockSpec((None, ADA_ROWS, ADA_TILE), lambda l, j: (l, 0, j)),
        out_shape=jax.ShapeDtypeStruct((DEPTH, ADA_ROWS, n), F32),
        scratch_shapes=[pltpu.VMEM((2, D_MODEL, LANES), F32)],
        compiler_params=pltpu.CompilerParams(
            dimension_semantics=("arbitrary", "arbitrary"), vmem_limit_bytes=VMEM_LIMIT_BYTES),
        name="ada",
    )(cond, w_ada, b_ada.reshape(DEPTH, 1, n))


def _norm_kernel(x_ref, g_ref, shift_ref, scale_ref, h_ref, *, mod_row):
    mr = slice(mod_row, mod_row + 1)
    h_ref[...] = _norm_modulate(x_ref[...], g_ref[...], shift_ref[mr, :], scale_ref[mr, :]).astype(BF16)


def _norm_mod(x, l, norm1_g, mods_all, mod_row, tm):
    t = x.shape[0]
    return pl.pallas_call(
        functools.partial(_norm_kernel, mod_row=mod_row),
        grid=(t // tm,),
        in_specs=[pl.BlockSpec((tm, D_MODEL), lambda i: (i, 0)),
                  _gain_spec(l), _mod_spec(l, 0), _mod_spec(l, 1)],
        out_specs=pl.BlockSpec((tm, D_MODEL), lambda i: (i, 0)),
        out_shape=jax.ShapeDtypeStruct((t, D_MODEL), BF16),
        compiler_params=pltpu.CompilerParams(
            dimension_semantics=("arbitrary",), vmem_limit_bytes=VMEM_LIMIT_BYTES),
        name="norm_mod",
    )(x, norm1_g, mods_all, mods_all)


CAST_ROWS = 128


def _cast_w_in_kernel(src_ref, mix_ref, gate_ref):
    _cast_w_in(src_ref[...], mix_ref, gate_ref)


def _cast_first_w_in(w_in):
    return pl.pallas_call(
        _cast_w_in_kernel,
        grid=(D_MODEL // CAST_ROWS,),
        in_specs=[pl.BlockSpec((None, CAST_ROWS, N_IN), lambda i: (0, i, 0))],
        out_specs=[pl.BlockSpec((CAST_ROWS, MIX_COLS), lambda i: (i, 0)),
                   pl.BlockSpec((CAST_ROWS, GATE_COLS), lambda i: (i, 0))],
        out_shape=[jax.ShapeDtypeStruct((D_MODEL, MIX_COLS), BF16),
                   jax.ShapeDtypeStruct((D_MODEL, GATE_COLS), BF16)],
        compiler_params=pltpu.CompilerParams(
            dimension_semantics=("arbitrary",), vmem_limit_bytes=VMEM_LIMIT_BYTES),
        name="cast_w_in",
    )(w_in)


def _rope(t, cos, sin_lo, sin_hi):
    up = pltpu.roll(t, HEAD_DIM - 32, axis=1)
    down = pltpu.roll(t, 32, axis=1)
    return t * cos + up * sin_lo + down * sin_hi


def _inproj_kernel(*refs, mix_part, use_rope):
    if use_rope:
        h_ref, w_ref, cos_ref, slo_ref, shi_ref, z_ref = refs
    else:
        h_ref, w_ref, z_ref = refs
    j = pl.program_id(1)

    if not mix_part:
        z_ref[...] = jnp.dot(h_ref[...], w_ref[...], preferred_element_type=F32).astype(BF16)
        return

    @pl.when(j == 0)
    def _():
        for blk in range(QK_COLS // SUB_TILE):
            c0 = blk * SUB_TILE
            acc = jnp.dot(h_ref[...], w_ref[:, c0:c0 + SUB_TILE], preferred_element_type=F32)
            if c0 >= RET_WIDTH:
                acc = acc * K_SCALE
            if use_rope:
                cos, slo, shi = cos_ref[...], slo_ref[...], shi_ref[...]
                for hh in range(_HEADS_PER_SUB):
                    hs = slice(hh * HEAD_DIM, (hh + 1) * HEAD_DIM)
                    z_ref[:, c0 + hh * HEAD_DIM:c0 + (hh + 1) * HEAD_DIM] = (
                        _rope(acc[:, hs], cos, slo, shi).astype(BF16))
            else:
                z_ref[:, c0:c0 + SUB_TILE] = acc.astype(BF16)
        z_ref[:, QK_COLS:] = jnp.dot(h_ref[...], w_ref[:, QK_COLS:],
                                     preferred_element_type=F32).astype(BF16)

    @pl.when(j != 0)
    def _():
        z_ref[...] = jnp.dot(h_ref[...], w_ref[...], preferred_element_type=F32).astype(BF16)


def _in_proj(h, w_part, mix_part, rope, tm):
    t = h.shape[0]
    width = w_part.shape[1]
    tile = width // IN_TILES
    assert tile * IN_TILES == width and tile % LANES == 0 and (not mix_part or tile >= QK_COLS)
    use_rope = mix_part and rope is not None
    in_specs = [
        pl.BlockSpec((tm, D_MODEL), lambda i, j: (i, 0)),
        pl.BlockSpec((D_MODEL, tile), lambda i, j: (0, j)),
    ]
    args = [h, w_part]
    if use_rope:
        in_specs += [pl.BlockSpec((tm, HEAD_DIM), lambda i, j: (i, 0))] * 3
        args += list(rope)
    return pl.pallas_call(
        functools.partial(_inproj_kernel, mix_part=mix_part, use_rope=use_rope),
        grid=(t // tm, IN_TILES),
        in_specs=in_specs,
        out_specs=pl.BlockSpec((tm, tile), lambda i, j: (i, j)),
        out_shape=jax.ShapeDtypeStruct((t, width), BF16),
        compiler_params=pltpu.CompilerParams(
            dimension_semantics=("arbitrary", "arbitrary"), vmem_limit_bytes=VMEM_LIMIT_BYTES),
        name="in_proj_mix" if mix_part else "in_proj_gate",
    )(*args)


def _state_kernel(k_ref, v_ref, kdf_ref, kdb_ref, cdf_ref, cdb_ref, sf0_ref, sb0_ref,
                  sf_all_ref, sb_all_ref, sf_fin_ref, sb_fin_ref, sf_scr, sb_scr):
    t = pl.program_id(0)
    nb = pl.num_programs(0)
    cps = sf_all_ref.shape[0]

    @pl.when(t == 0)
    def _():
        sf_scr[...] = sf0_ref[...]
        sb_scr[...] = sb0_ref[...]

    contract_rows = (((0,), (0,)), ((), ()))

    def advance(block, c, kd_ref, cd_ref, all_ref, scr):
        rows = pl.ds(pl.multiple_of((block * cps + c) * CHUNK, CHUNK), CHUNK)
        all_ref[c] = scr[...].astype(BF16)
        for h in range(RET_HEADS):
            hs = slice(h * HEAD_DIM, (h + 1) * HEAD_DIM)
            kd = (k_ref[rows, hs].astype(F32) * kd_ref[:, hs]).astype(BF16)
            kv = lax.dot_general(kd, v_ref[rows, hs], contract_rows, preferred_element_type=F32)
            scr[h] = scr[h] * cd_ref[h] + kv

    for c in range(cps):
        advance(t, c, kdf_ref, cdf_ref, sf_all_ref, sf_scr)
        advance(nb - 1 - t, cps - 1 - c, kdb_ref, cdb_ref, sb_all_ref, sb_scr)

    @pl.when(t == pl.num_programs(0) - 1)
    def _():
        sf_fin_ref[...] = sf_scr[...]
        sb_fin_ref[...] = sb_scr[...]


def _states(z, l, tabs, sf0, sb0, cps):
    t = z.shape[0]
    nb = t // (cps * CHUNK)
    whole = lambda c: pl.BlockSpec((t, RET_WIDTH), lambda i: (0, c), pipeline_mode=pl.Buffered(1))
    state_shape = (RET_HEADS, HEAD_DIM, HEAD_DIM)
    consts = [tabs["kdf"], tabs["kdb"], tabs["cdf"], tabs["cdb"]]
    kb, vb = K_COL // RET_WIDTH, V_COL // RET_WIDTH
    in_specs = [whole(kb), whole(vb)] + [_layer_spec(a, l) for a in consts]
    in_specs += [_const_spec(sf0), _const_spec(sb0)]
    all_shape = jax.ShapeDtypeStruct((t // CHUNK,) + state_shape, BF16)
    fin_shape = jax.ShapeDtypeStruct(state_shape, F32)
    return pl.pallas_call(
        _state_kernel,
        grid=(nb,),
        in_specs=in_specs,
        out_specs=[
            pl.BlockSpec((cps,) + state_shape, lambda i: (i, 0, 0, 0)),
            pl.BlockSpec((cps,) + state_shape, lambda i: (nb - 1 - i, 0, 0, 0)),
            pl.BlockSpec(state_shape, lambda i: (0, 0, 0)),
            pl.BlockSpec(state_shape, lambda i: (0, 0, 0)),
        ],
        out_shape=[all_shape, all_shape, fin_shape, fin_shape],
        scratch_shapes=[pltpu.VMEM(state_shape, F32), pltpu.VMEM(state_shape, F32)],
        compiler_params=pltpu.CompilerParams(
            dimension_semantics=("arbitrary",), vmem_limit_bytes=VMEM_LIMIT_BYTES),
        name="ret_states",
    )(z, z, *consts, sf0, sb0)


def _mixer_kernel(*refs, mod_row):
    (zmix_ref, sf_ref, sb_ref, zgate_ref, x_ref,
     dmat_ref, qdf_ref, qdb_ref, retg_ref, pmask_ref, pinv_ref, poolw_ref, pscale_ref,
     sgng_ref, sgw_ref, btab_ref, wbr_ref, wbp_ref, wbs_ref, wout_ref,
     gate_ref, n2g_ref, shift2_ref, scale2_ref,
     x1_ref, h2_ref,
     y_scr, ret_cur, pool_cur, sg_cur, ret_prev, pool_prev, sg_prev) = refs
    tb = x_ref.shape[0]
    nck = tb // CHUNK
    contract_last = (((1,), (1,)), ((), ()))
    mr = slice(mod_row, mod_row + 1)

    def merge_piece(i):
        cs = slice(i * MERGE_TILE, (i + 1) * MERGE_TILE)
        def gate(branch):
            c0 = branch * D_MODEL + i * MERGE_TILE
            return jax.nn.sigmoid(zgate_ref[:, c0:c0 + MERGE_TILE].astype(F32))

        y = gate(0) * jnp.dot(ret_prev[...], wbr_ref[:, cs], preferred_element_type=F32)
        y = y + gate(1) * jnp.dot(pool_prev[...], wbp_ref[:, cs], preferred_element_type=F32)
        y = y + gate(2) * jnp.dot(sg_prev[...], wbs_ref[:, cs], preferred_element_type=F32)
        y_scr[:, cs] = y.astype(BF16)

    def out_piece(i):
        cs = slice(i * MERGE_TILE, (i + 1) * MERGE_TILE)
        out = jnp.dot(y_scr[...], wout_ref[:, cs], preferred_element_type=F32)
        x1_ref[:, cs] = x_ref[:, cs] + gate_ref[mr, cs] * out

    def ret_unit(c, h):
        rows = slice(c * CHUNK, (c + 1) * CHUNK)
        hs = slice(h * HEAD_DIM, (h + 1) * HEAD_DIM)
        head = lambda col: zmix_ref[rows, col + h * HEAD_DIM:col + (h + 1) * HEAD_DIM]
        qb = head(Q_COL)
        scores = lax.dot_general(qb, head(K_COL), contract_last, preferred_element_type=F32)
        scores = (scores * dmat_ref[h]).astype(BF16)
        cross = qdf_ref[:, hs] * jnp.dot(qb, sf_ref[c, h], preferred_element_type=F32)
        cross = cross + qdb_ref[:, hs] * jnp.dot(qb, sb_ref[c, h], preferred_element_type=F32)

        def second():
            o = jnp.dot(scores, head(V_COL), preferred_element_type=F32) + cross
            g = head(G_COL).astype(F32)
            ret = _rms(o) * retg_ref[:, hs] * (g * jax.nn.sigmoid(g))
            ret_cur[rows, hs] = ret.astype(BF16)

        return second

    def pool_unit(gi):
        gs = slice(gi * GROUP_DIM, (gi + 1) * GROUP_DIM)
        pb = zmix_ref[:, POOL_COL + gi * GROUP_DIM:POOL_COL + (gi + 1) * GROUP_DIM]
        win = jnp.dot(pmask_ref[gi], pb, preferred_element_type=F32)
        pooled = (win * pinv_ref[:, gs] - pb.astype(F32)).astype(BF16)

        def second():
            mapped = jnp.dot(pooled, poolw_ref[gi], preferred_element_type=F32)
            pool_cur[:, gs] = (mapped * pscale_ref[:, gs]).astype(BF16)

        return second

    def sg_unit(c):
        rows = slice(c * CHUNK, (c + 1) * CHUNK)
        sv = zmix_ref[rows, SV_COL:SV_COL + SG_WIDTH].astype(F32)
        sv = (_rms(jax.nn.gelu(sv)) * sgng_ref[...]).astype(BF16)

        def second():
            for gi in range(SG_GROUPS):
                gs = slice(gi * GROUP_DIM, (gi + 1) * GROUP_DIM)
                mixed = jnp.dot(sgw_ref[gi], sv[:, gs], preferred_element_type=F32)
                mixed = mixed + btab_ref[:, gs]
                u = zmix_ref[rows, U_COL + gi * GROUP_DIM:U_COL + (gi + 1) * GROUP_DIM]
                u = jax.nn.gelu(u.astype(F32))
                sg_cur[rows, gs] = (u * mixed).astype(BF16)

        return second

    mix_units = [functools.partial(ret_unit, c, h) for c in range(nck) for h in range(RET_HEADS)]
    mix_units += [functools.partial(pool_unit, gi) for gi in range(POOL_GROUPS)]
    mix_units += [functools.partial(sg_unit, c) for c in range(nck)]
    pieces = ([functools.partial(merge_piece, i) for i in range(N_MERGE)]
              + [functools.partial(out_piece, i) for i in range(N_MERGE)])

    def run(mix, merge):
        units = mix_units if mix else []
        per_piece = -(-len(units) // len(pieces))
        pending = [unit() for unit in units[:per_piece]]
        for i, piece in enumerate(pieces):
            if merge:
                piece()
            for second in pending:
                second()
            pending = [unit() for unit in units[(i + 1) * per_piece:(i + 2) * per_piece]]
        assert not pending
        if merge:
            gain2 = n2g_ref[...] * (1.0 + scale2_ref[mr, :])
            h2_ref[...] = (_rms(x1_ref[...]) * gain2 + shift2_ref[mr, :]).astype(BF16)
        if mix:
            ret_prev[...] = ret_cur[...]
            pool_prev[...] = pool_cur[...]
            sg_prev[...] = sg_cur[...]

    s = pl.program_id(0)
    last = pl.num_programs(0) - 1
    pl.when(s == 0)(functools.partial(run, True, False))
    pl.when(jnp.logical_and(s > 0, s < last))(functools.partial(run, True, True))
    pl.when(s == last)(functools.partial(run, False, True))


def _mixer(z_mix, z_gate, x, sf_all, sb_all, l, tabs, pool_tabs, p, mods_all, mod_row, tb):
    t = x.shape[0]
    nb = t // tb
    nck = tb // CHUNK
    cur = lambda s: jnp.minimum(s, nb - 1)
    prev = lambda s: jnp.maximum(s - 1, 0)

    state_spec = pl.BlockSpec((nck, RET_HEADS, HEAD_DIM, HEAD_DIM), lambda s: (cur(s), 0, 0, 0))
    row_spec = pl.BlockSpec((tb, D_MODEL), lambda s: (prev(s), 0))
    in_specs = [pl.BlockSpec((tb, MIX_COLS), lambda s: (cur(s), 0)), state_spec, state_spec,
                pl.BlockSpec((tb, GATE_COLS), lambda s: (prev(s), 0)), row_spec]
    args = [z_mix, sf_all, sb_all, z_gate, x]
    pmask, pinv = pool_tabs
    layer_consts = [tabs["dmat"], tabs["qdf"], tabs["qdb"], p["ret_norm_g"]]
    in_specs += [_layer_spec(a, l) for a in layer_consts] + [_const_spec(pmask), _const_spec(pinv)]
    args += layer_consts + [pmask, pinv]
    layer_consts = [p["pool_w"], p["pool_scale"], p["sg_norm_g"], p["sg_w"], tabs["btab"],
                    p["w_br"], p["w_bp"], p["w_bs"], p["w_out"]]
    in_specs += [_layer_spec(a, l) for a in layer_consts]
    args += layer_consts
    in_specs += [_mod_spec(l, 2), _gain_spec(l), _mod_spec(l, 3), _mod_spec(l, 4)]
    args += [mods_all, p["norm2_g"], mods_all, mods_all]
    mix_scratch = [pltpu.VMEM((tb, RET_WIDTH), BF16), pltpu.VMEM((tb, POOL_WIDTH), BF16),
                   pltpu.VMEM((tb, SG_WIDTH), BF16)]
    return pl.pallas_call(
        functools.partial(_mixer_kernel, mod_row=mod_row),
        grid=(nb + 1,),
        in_specs=in_specs,
        out_specs=[row_spec, row_spec],
        out_shape=[jax.ShapeDtypeStruct((t, D_MODEL), F32),
                   jax.ShapeDtypeStruct((t, D_MODEL), BF16)],
        scratch_shapes=[pltpu.VMEM((tb, D_MODEL), BF16)] + mix_scratch + mix_scratch,
        compiler_params=pltpu.CompilerParams(
            dimension_semantics=("arbitrary",), vmem_limit_bytes=VMEM_LIMIT_BYTES),
        name="mixer",
    )(*args)


EPILOGUE_ROWS = 16
CAST_EVERY = 2


def _mlp_kernel(*refs, mod_row, final, n_cast):
    refs = list(refs)
    cast_src = cast_dst = ()
    if n_cast:
        n_dst = n_cast + 1
        cast_dst = refs[-n_dst:]
        refs = refs[:-n_dst]
        n_out = 1 if final else 2
        cast_src = refs[-n_out - n_cast:-n_out]
        refs = refs[:-n_out - n_cast] + refs[-n_out:]
    if final:
        h2_ref, x1_ref, w1_ref, w2_ref, gate_ref, fng_ref, o_ref = refs
    else:
        (h2_ref, x1_ref, w1_ref, w2_ref, gate_ref, ng_ref, nshift_ref, nscale_ref,
         o_ref, hn_ref) = refs
    f = pl.program_id(1)
    mr = slice(mod_row, mod_row + 1)

    def step(first):
        a = jnp.maximum(jnp.dot(h2_ref[...], w1_ref[...], preferred_element_type=F32), 0.0)
        part = jnp.dot((a * a).astype(BF16), w2_ref[...], preferred_element_type=F32)
        if first:
            o_ref[...] = part
        else:
            o_ref[...] += part
        part_id = (pl.program_id(0) * pl.num_programs(1) + f) % CAST_EVERY
        for k, src in enumerate(cast_src):
            share = src.shape[0] // CAST_EVERY
            rows = pl.ds(pl.multiple_of(part_id * share, share), share)
            dst = cast_dst[k + 1]
            if k == 0:
                _cast_w_in(src[rows, :], cast_dst[0].at[rows], cast_dst[1].at[rows])
            elif dst.ndim == 3:
                for tile in range(dst.shape[0]):
                    cols = slice(tile * dst.shape[2], (tile + 1) * dst.shape[2])
                    dst[tile, rows, :] = src[rows, cols].astype(BF16)
            else:
                dst[rows, :] = src[rows, :].astype(BF16)

    @pl.when(f == 0)
    def _():
        step(True)

    @pl.when(f > 0)
    def _():
        step(False)

    @pl.when(f == pl.num_programs(1) - 1)
    def _():
        gate = gate_ref[mr, :]
        if final:
            gain = fng_ref[...]
        else:
            gain = ng_ref[...] * (1.0 + nscale_ref[mr, :])
            shift = nshift_ref[mr, :]

        for r in range(o_ref.shape[0] // EPILOGUE_ROWS):
            rows = slice(r * EPILOGUE_ROWS, (r + 1) * EPILOGUE_ROWS)
            x2 = x1_ref[rows, :] + gate * o_ref[rows, :]
            if final:
                o_ref[rows, :] = _rms(x2) * gain
            else:
                o_ref[rows, :] = x2
                hn_ref[rows, :] = (_rms(x2) * gain + shift).astype(BF16)


def _mlp(h2, x1, l, w1, w2, mods_all, mod_row, norm1_g, final_gain, tm, tf, cast_next=()):
    t = x1.shape[0]
    final = final_gain is not None
    n_f = D_FF // tf
    grid = (t // tm, n_f)
    rows = pl.BlockSpec((tm, D_MODEL), lambda i, f: (i, 0))
    assert w1.shape == (n_f, D_MODEL, tf)
    in_specs = [rows, rows,
                pl.BlockSpec((None, D_MODEL, tf), lambda i, f: (f, 0, 0)),
                pl.BlockSpec((tf, D_MODEL), lambda i, f: (f, 0)),
                _mod_spec(l, 5)]
    args = [h2, x1, w1, w2, mods_all]
    x_shape = jax.ShapeDtypeStruct((t, D_MODEL), F32)
    if final:
        in_specs.append(pl.BlockSpec((1, D_MODEL), lambda i, f: (0, 0)))
        args.append(final_gain)
        out_specs, out_shape = [rows], [x_shape]
    else:
        in_specs += [_gain_spec(l + 1), _mod_spec(l + 1, 0), _mod_spec(l + 1, 1)]
        args += [norm1_g, mods_all, mods_all]
        out_specs = [rows, rows]
        out_shape = [x_shape, jax.ShapeDtypeStruct((t, D_MODEL), BF16)]
    for k, w in enumerate(cast_next):
        n_slabs = grid[0] * grid[1] // CAST_EVERY
        slab, cols = w.shape[1] // n_slabs, w.shape[2]
        assert slab * n_slabs == w.shape[1] and slab % (2 * SUBLANES) == 0
        in_specs.append(pl.BlockSpec((None, slab, cols),
                                     lambda i, f: (l + 1, (i * n_f + f) // CAST_EVERY, 0)))
        args.append(w)
        if k == 1:
            out_specs.append(pl.BlockSpec((n_f, slab, tf),
                                          lambda i, f: (0, (i * n_f + f) // CAST_EVERY, 0)))
            out_shape.append(jax.ShapeDtypeStruct((n_f, w.shape[1], tf), BF16))
            continue
        for out_cols in ((MIX_COLS, GATE_COLS) if k == 0 else (cols,)):
            out_specs.append(pl.BlockSpec((slab, out_cols),
                                          lambda i, f: ((i * n_f + f) // CAST_EVERY, 0)))
            out_shape.append(jax.ShapeDtypeStruct((w.shape[1], out_cols), BF16))
    return pl.pallas_call(
        functools.partial(_mlp_kernel, mod_row=mod_row, final=final, n_cast=len(cast_next)),
        grid=grid,
        in_specs=in_specs,
        out_specs=out_specs,
        out_shape=out_shape,
        compiler_params=pltpu.CompilerParams(
            dimension_semantics=("arbitrary", "arbitrary"), vmem_limit_bytes=VMEM_LIMIT_BYTES),
        name="mlp",
    )(*args)


def _tile_major(w, tile):
    rows, cols = w.shape
    return w.reshape(rows, cols // tile, tile).transpose(1, 0, 2)


def _rope_tables(t):
    half = HEAD_DIM // 2
    nf = half // 2
    tok = np.arange(t)
    inv = ROPE_THETA ** (-np.arange(nf, dtype=np.float64) / nf)
    lane = np.arange(HEAD_DIM)
    pos = np.where(lane[None, :] < half, (tok // GRID_W)[:, None], (tok % GRID_W)[:, None])
    ang = pos * inv[lane % nf][None, :]
    lower = (lane % half) < nf
    sin = np.sin(ang)
    cos = np.cos(ang).astype(np.float32)
    sin_lo = np.where(lower[None, :], -sin, 0.0).astype(np.float32)
    sin_hi = np.where(lower[None, :], 0.0, sin).astype(np.float32)
    return jnp.asarray(cos), jnp.asarray(sin_lo), jnp.asarray(sin_hi)


def _pool_tables(tb, seg_len):
    pos = np.arange(tb) % seg_len
    base = np.arange(tb) - pos
    col = np.arange(tb)[None, :]
    masks, invs = [], []
    for w in POOL_WINDOWS:
        lo = np.maximum(pos - w // 2, 0)
        hi = np.minimum(pos + w // 2 - 1, seg_len - 1)
        masks.append((col >= (base + lo)[:, None]) & (col <= (base + hi)[:, None]))
        invs.append(np.repeat((1.0 / (hi - lo + 1))[:, None], GROUP_DIM, axis=1))
    pmask = jnp.asarray(np.stack(masks).astype(np.float32), dtype=BF16)
    pinv = jnp.asarray(np.concatenate(invs, axis=1).astype(np.float32))
    return pmask, pinv


def _decay_tables(logit):
    lg = jax.nn.log_sigmoid(logit.astype(F32))
    lgf, lgb = lg[0], lg[1]
    idx = jnp.arange(CHUNK, dtype=F32)
    dist = idx[:, None] - idx[None, :]
    fwd = jnp.exp(lgf[:, None, None] * jnp.maximum(dist, 0.0))
    bwd = jnp.exp(lgb[:, None, None] * jnp.maximum(-dist, 0.0))
    dmat = jnp.where(dist > 0, fwd, jnp.where(dist < 0, bwd, 2.0))

    def lanes(tab):
        return jnp.repeat(tab.T, HEAD_DIM, axis=1)

    return dict(
        dmat=dmat,
        qdf=lanes(jnp.exp(lgf[:, None] * (idx + 1.0)[None])),
        qdb=lanes(jnp.exp(lgb[:, None] * (CHUNK - idx)[None])),
        kdf=lanes(jnp.exp(lgf[:, None] * (CHUNK - 1.0 - idx)[None])),
        kdb=lanes(jnp.exp(lgb[:, None] * idx[None])),
        cdf=jnp.broadcast_to(jnp.exp(lgf * CHUNK)[:, None, None], (RET_HEADS, 1, HEAD_DIM)),
        cdb=jnp.broadcast_to(jnp.exp(lgb * CHUNK)[:, None, None], (RET_HEADS, 1, HEAD_DIM)),
    )


LATENT_ROWS_IN = 1024
LATENT_ROWS_MIX = 256
LATENT_ROWS_MLP = 512
LATENT_FF_TILE = 1024
LATENT_ROWS_NORM = 512
LATENT_STATE_CHUNKS = 4
FF_TILE = 1024


def kernel(x, c, ctx, c_ctx, w_ada, b_ada, norm1_g, w_in, ret_decay_logit, ret_norm_g, pool_w,
           pool_scale, sg_norm_g, sg_w, sg_b, w_br, w_bp, w_bs, w_out, norm2_g, w1, w2, final_norm_g):
    assert x.shape[0] == 1 and ctx.shape[0] == 1
    t = x.shape[1]
    tc = ctx.shape[1]
    xs = x[0]
    xc = ctx[0]

    cond = jnp.stack([c[0], c_ctx])[:, :, None]
    mods_all = _ada(cond, w_ada, b_ada)

    rope = _rope_tables(t)
    pool_lat = _pool_tables(LATENT_ROWS_MIX, GRID_W)
    pool_ctx = _pool_tables(tc, tc)
    zero_state = jnp.zeros((RET_HEADS, HEAD_DIM, HEAD_DIM), F32)

    tabs = jax.vmap(_decay_tables)(ret_decay_logit)
    tabs["btab"] = jnp.repeat(jnp.swapaxes(sg_b, 1, 2), GROUP_DIM, axis=2)
    vec = lambda v: v.reshape(DEPTH, 1, -1)
    p = dict(ret_norm_g=vec(ret_norm_g), pool_w=pool_w.astype(BF16), pool_scale=vec(pool_scale),
             sg_norm_g=vec(sg_norm_g), sg_w=sg_w.astype(BF16),
             w_br=w_br.astype(BF16), w_bp=w_bp.astype(BF16), w_bs=w_bs.astype(BF16),
             w_out=w_out.astype(BF16), norm2_g=vec(norm2_g))
    n1g = vec(norm1_g)
    fng = final_norm_g.reshape(1, -1)
    streamed = (w_in, w1, w2)
    w_mix_b, w_gate_b = _cast_first_w_in(w_in)
    w1_b, w2_b = _tile_major(w1[0].astype(BF16), FF_TILE), w2[0].astype(BF16)

    hc = _norm_mod(xc, 0, n1g, mods_all, ROW_CONTEXT, tc)
    hx = _norm_mod(xs, 0, n1g, mods_all, ROW_LATENT, LATENT_ROWS_NORM)
    for l in range(DEPTH):
        last = l == DEPTH - 1

        zc = _in_proj(hc, w_mix_b, True, None, tc)
        sfc, sbc, sf, sb = _states(zc, l, tabs, zero_state, zero_state, tc // CHUNK)
        if not last:
            gc = _in_proj(hc, w_gate_b, False, None, tc)
            xc1, hc2 = _mixer(zc, gc, xc, sfc, sbc, l, tabs, pool_ctx, p, mods_all, ROW_CONTEXT, tc)
            xc, hc = _mlp(hc2, xc1, l, w1_b, w2_b, mods_all, ROW_CONTEXT, n1g, None, tc, FF_TILE)

        zx = _in_proj(hx, w_mix_b, True, rope, LATENT_ROWS_IN)
        gx = _in_proj(hx, w_gate_b, False, None, LATENT_ROWS_IN)
        sfx, sbx, _, _ = _states(zx, l, tabs, sf, sb, LATENT_STATE_CHUNKS)
        x1, h2 = _mixer(zx, gx, xs, sfx, sbx, l, tabs, pool_lat, p, mods_all, ROW_LATENT,
                        LATENT_ROWS_MIX)
        if last:
            xs, = _mlp(h2, x1, l, w1_b, w2_b, mods_all, ROW_LATENT, n1g, fng,
                       LATENT_ROWS_MLP, LATENT_FF_TILE)
        else:
            xs, hx, w_mix_b, w_gate_b, w1_b, w2_b = _mlp(
                h2, x1, l, w1_b, w2_b, mods_all, ROW_LATENT, n1g, None,
                LATENT_ROWS_MLP, LATENT_FF_TILE, cast_next=streamed)

    return xs[None]
```

```python
import functools

import numpy as np
import jax
import jax.numpy as jnp
from jax import lax
from jax.experimental import pallas as pl
from jax.experimental.pallas import tpu as pltpu

F32 = jnp.float32
BF16 = jnp.bfloat16

D_MODEL = 2048
DEPTH = 4
GRID_W = 64
EPS = 1e-6
N_MOD = 6

HEAD_DIM = 128
RET_WIDTH = D_MODEL // 2
RET_HEADS = RET_WIDTH // HEAD_DIM
CHUNK = 128
K_SCALE = HEAD_DIM ** -0.5
ROPE_THETA = 10000.0

POOL_WIDTH = D_MODEL // 4
POOL_WINDOWS = (2, 4, 8, 16)
POOL_GROUPS = len(POOL_WINDOWS)
GROUP_DIM = POOL_WIDTH // POOL_GROUPS

SG_WIDTH = D_MODEL // 4
SG_GROUPS = 4

D_FF = 4 * D_MODEL
N_IN = 4 * RET_WIDTH + POOL_WIDTH + 2 * SG_WIDTH + 3 * D_MODEL

SUBLANES = 8
LANES = 128
VMEM_LIMIT_BYTES = 56 * 1024 * 1024

MIX_COLS = 4 * RET_WIDTH + POOL_WIDTH + 2 * SG_WIDTH
GATE_COLS = 3 * D_MODEL
IN_TILES = 2
QK_COLS = 2 * RET_WIDTH
SUB_TILE = 512
_HEADS_PER_SUB = SUB_TILE // HEAD_DIM
MERGE_TILE = 512
N_MERGE = D_MODEL // MERGE_TILE
Q_COL, K_COL, V_COL, G_COL = (k * RET_WIDTH for k in range(4))
POOL_COL = 4 * RET_WIDTH
U_COL = POOL_COL + POOL_WIDTH
SV_COL = U_COL + SG_WIDTH


def _cast_w_in(src, mix_ref, gate_ref):
    mix_ref[...] = src[:, :MIX_COLS].astype(BF16)
    gate_ref[...] = src[:, MIX_COLS:].astype(BF16)

ADA_TILE = 1024
ADA_ROWS = 8
ROW_LATENT, ROW_CONTEXT = 0, 1


def _layer_spec(arr, l):
    nd = arr.ndim - 1
    return pl.BlockSpec((None,) + arr.shape[1:], lambda *_: (l,) + (0,) * nd,
                        pipeline_mode=pl.Buffered(1))


def _const_spec(arr):
    nd = arr.ndim
    return pl.BlockSpec(arr.shape, lambda *_: (0,) * nd, pipeline_mode=pl.Buffered(1))


def _mod_spec(l, k):
    return pl.BlockSpec((None, ADA_ROWS, D_MODEL), lambda *_: (l, 0, k))


def _gain_spec(l):
    return pl.BlockSpec((None, 1, D_MODEL), lambda *_: (l, 0, 0))


def _rms(x):
    return x * lax.rsqrt(jnp.mean(x * x, axis=-1, keepdims=True) + EPS)


def _norm_modulate(x, gain, shift, scale):
    return _rms(x) * gain * (1.0 + scale) + shift


def _ada_kernel(cond_ref, w_ref, b_ref, o_ref, act_scr):
    tn = w_ref.shape[1]

    @pl.when(jnp.logical_and(pl.program_id(0) == 0, pl.program_id(1) == 0))
    def _():
        for r in range(2):
            a = cond_ref[r]
            act_scr[r] = jnp.broadcast_to(a * jax.nn.sigmoid(a), (D_MODEL, LANES))

    def body(g, accs):
        r0 = pl.multiple_of(g * SUBLANES, SUBLANES)
        w = w_ref[pl.ds(r0, SUBLANES), :]
        out = []
        for r in range(2):
            a = jnp.tile(act_scr[r, pl.ds(r0, SUBLANES), :], (1, tn // LANES))
            out.append(accs[r] + w * a)
        return tuple(out)

    zero = jnp.zeros((SUBLANES, tn), F32)
    accs = lax.fori_loop(0, D_MODEL // SUBLANES, body, (zero, zero), unroll=8)
    o_ref[...] = jnp.zeros((ADA_ROWS, tn), F32)
    for r in range(2):
        o_ref[r:r + 1, :] = jnp.sum(accs[r], axis=0, keepdims=True) + b_ref[...]


def _ada(cond, w_ada, b_ada):
    n = N_MOD * D_MODEL
    return pl.pallas_call(
        _ada_kernel,
        grid=(DEPTH, n // ADA_TILE),
        in_specs=[
            pl.BlockSpec((2, D_MODEL, 1), lambda l, j: (0, 0, 0)),
            pl.BlockSpec((None, D_MODEL, ADA_TILE), lambda l, j: (l, 0, j)),
            pl.BlockSpec((None, 1, ADA_TILE), lambda l, j: (l, 0, j)),
        ],
        out_specs=pl.BlockSpec((None, ADA_ROWS, ADA_TILE), lambda l, j: (l, 0, j)),
        out_shape=jax.ShapeDtypeStruct((DEPTH, ADA_ROWS, n), F32),
        scratch_shapes=[pltpu.VMEM((2, D_MODEL, LANES), F32)],
        compiler_params=pltpu.CompilerParams(
            dimension_semantics=("arbitrary", "arbitrary"), vmem_limit_bytes=VMEM_LIMIT_BYTES),
        name="ada",
    )(cond, w_ada, b_ada.reshape(DEPTH, 1, n))


def _norm_kernel(x_ref, g_ref, shift_ref, scale_ref, h_ref, *, mod_row):
    mr = slice(mod_row, mod_row + 1)
    h_ref[...] = _norm_modulate(x_ref[...], g_ref[...], shift_ref[mr, :], scale_ref[mr, :]).astype(BF16)


def _norm_mod(x, l, norm1_g, mods_all, mod_row, tm):
    t = x.shape[0]
    return pl.pallas_call(
        functools.partial(_norm_kernel, mod_row=mod_row),
        grid=(t // tm,),
        in_specs=[pl.BlockSpec((tm, D_MODEL), lambda i: (i, 0)),
                  _gain_spec(l), _mod_spec(l, 0), _mod_spec(l, 1)],
        out_specs=pl.BlockSpec((tm, D_MODEL), lambda i: (i, 0)),
        out_shape=jax.ShapeDtypeStruct((t, D_MODEL), BF16),
        compiler_params=pltpu.CompilerParams(
            dimension_semantics=("arbitrary",), vmem_limit_bytes=VMEM_LIMIT_BYTES),
        name="norm_mod",
    )(x, norm1_g, mods_all, mods_all)


CAST_ROWS = 128


def _cast_w_in_kernel(src_ref, mix_ref, gate_ref):
    _cast_w_in(src_ref[...], mix_ref, gate_ref)


def _cast_first_w_in(w_in):
    return pl.pallas_call(
        _cast_w_in_kernel,
        grid=(D_MODEL // CAST_ROWS,),
        in_specs=[pl.BlockSpec((None, CAST_ROWS, N_IN), lambda i: (0, i, 0))],
        out_specs=[pl.BlockSpec((CAST_ROWS, MIX_COLS), lambda i: (i, 0)),
                   pl.BlockSpec((CAST_ROWS, GATE_COLS), lambda i: (i, 0))],
        out_shape=[jax.ShapeDtypeStruct((D_MODEL, MIX_COLS), BF16),
                   jax.ShapeDtypeStruct((D_MODEL, GATE_COLS), BF16)],
        compiler_params=pltpu.CompilerParams(
            dimension_semantics=("arbitrary",), vmem_limit_bytes=VMEM_LIMIT_BYTES),
        name="cast_w_in",
    )(w_in)


def _rope(t, cos, sin_lo, sin_hi):
    up = pltpu.roll(t, HEAD_DIM - 32, axis=1)
    down = pltpu.roll(t, 32, axis=1)
    return t * cos + up * sin_lo + down * sin_hi


def _inproj_kernel(*refs, mix_part, use_rope):
    if use_rope:
        h_ref, w_ref, cos_ref, slo_ref, shi_ref, z_ref = refs
    else:
        h_ref, w_ref, z_ref = refs
    j = pl.program_id(1)

    if not mix_part:
        z_ref[...] = jnp.dot(h_ref[...], w_ref[...], preferred_element_type=F32).astype(BF16)
        return

    @pl.when(j == 0)
    def _():
        for blk in range(QK_COLS // SUB_TILE):
            c0 = blk * SUB_TILE
            acc = jnp.dot(h_ref[...], w_ref[:, c0:c0 + SUB_TILE], preferred_element_type=F32)
            if c0 >= RET_WIDTH:
                acc = acc * K_SCALE
            if use_rope:
                cos, slo, shi = cos_ref[...], slo_ref[...], shi_ref[...]
                for hh in range(_HEADS_PER_SUB):
                    hs = slice(hh * HEAD_DIM, (hh + 1) * HEAD_DIM)
                    z_ref[:, c0 + hh * HEAD_DIM:c0 + (hh + 1) * HEAD_DIM] = (
                        _rope(acc[:, hs], cos, slo, shi).astype(BF16))
            else:
                z_ref[:, c0:c0 + SUB_TILE] = acc.astype(BF16)
        z_ref[:, QK_COLS:] = jnp.dot(h_ref[...], w_ref[:, QK_COLS:],
                                     preferred_element_type=F32).astype(BF16)

    @pl.when(j != 0)
    def _():
        z_ref[...] = jnp.dot(h_ref[...], w_ref[...], preferred_element_type=F32).astype(BF16)


def _in_proj(h, w_part, mix_part, rope, tm):
    t = h.shape[0]
    width = w_part.shape[1]
    tile = width // IN_TILES
    assert tile * IN_TILES == width and tile % LANES == 0 and (not mix_part or tile >= QK_COLS)
    use_rope = mix_part and rope is not None
    in_specs = [
        pl.BlockSpec((tm, D_MODEL), lambda i, j: (i, 0)),
        pl.BlockSpec((D_MODEL, tile), lambda i, j: (0, j)),
    ]
    args = [h, w_part]
    if use_rope:
        in_specs += [pl.BlockSpec((tm, HEAD_DIM), lambda i, j: (i, 0))] * 3
        args += list(rope)
    return pl.pallas_call(
        functools.partial(_inproj_kernel, mix_part=mix_part, use_rope=use_rope),
        grid=(t // tm, IN_TILES),
        in_specs=in_specs,
        out_specs=pl.BlockSpec((tm, tile), lambda i, j: (i, j)),
        out_shape=jax.ShapeDtypeStruct((t, width), BF16),
        compiler_params=pltpu.CompilerParams(
            dimension_semantics=("arbitrary", "arbitrary"), vmem_limit_bytes=VMEM_LIMIT_BYTES),
        name="in_proj_mix" if mix_part else "in_proj_gate",
    )(*args)


def _state_kernel(k_ref, v_ref, kdf_ref, kdb_ref, cdf_ref, cdb_ref, sf0_ref, sb0_ref,
                  sf_all_ref, sb_all_ref, sf_fin_ref, sb_fin_ref, sf_scr, sb_scr):
    t = pl.program_id(0)
    nb = pl.num_programs(0)
    cps = sf_all_ref.shape[0]

    @pl.when(t == 0)
    def _():
        sf_scr[...] = sf0_ref[...]
        sb_scr[...] = sb0_ref[...]

    contract_rows = (((0,), (0,)), ((), ()))

    def advance(block, c, kd_ref, cd_ref, all_ref, scr):
        rows = pl.ds(pl.multiple_of((block * cps + c) * CHUNK, CHUNK), CHUNK)
        all_ref[c] = scr[...].astype(BF16)
        for h in range(RET_HEADS):
            hs = slice(h * HEAD_DIM, (h + 1) * HEAD_DIM)
            kd = (k_ref[rows, hs].astype(F32) * kd_ref[:, hs]).astype(BF16)
            kv = lax.dot_general(kd, v_ref[rows, hs], contract_rows, preferred_element_type=F32)
            scr[h] = scr[h] * cd_ref[h] + kv

    for c in range(cps):
        advance(t, c, kdf_ref, cdf_ref, sf_all_ref, sf_scr)
        advance(nb - 1 - t, cps - 1 - c, kdb_ref, cdb_ref, sb_all_ref, sb_scr)

    @pl.when(t == pl.num_programs(0) - 1)
    def _():
        sf_fin_ref[...] = sf_scr[...]
        sb_fin_ref[...] = sb_scr[...]


def _states(z, l, tabs, sf0, sb0, cps):
    t = z.shape[0]
    nb = t // (cps * CHUNK)
    whole = lambda c: pl.BlockSpec((t, RET_WIDTH), lambda i: (0, c), pipeline_mode=pl.Buffered(1))
    state_shape = (RET_HEADS, HEAD_DIM, HEAD_DIM)
    consts = [tabs["kdf"], tabs["kdb"], tabs["cdf"], tabs["cdb"]]
    kb, vb = K_COL // RET_WIDTH, V_COL // RET_WIDTH
    in_specs = [whole(kb), whole(vb)] + [_layer_spec(a, l) for a in consts]
    in_specs += [_const_spec(sf0), _const_spec(sb0)]
    all_shape = jax.ShapeDtypeStruct((t // CHUNK,) + state_shape, BF16)
    fin_shape = jax.ShapeDtypeStruct(state_shape, F32)
    return pl.pallas_call(
        _state_kernel,
        grid=(nb,),
        in_specs=in_specs,
        out_specs=[
            pl.BlockSpec((cps,) + state_shape, lambda i: (i, 0, 0, 0)),
            pl.BlockSpec((cps,) + state_shape, lambda i: (nb - 1 - i, 0, 0, 0)),
            pl.BlockSpec(state_shape, lambda i: (0, 0, 0)),
            pl.BlockSpec(state_shape, lambda i: (0, 0, 0)),
        ],
        out_shape=[all_shape, all_shape, fin_shape, fin_shape],
        scratch_shapes=[pltpu.VMEM(state_shape, F32), pltpu.VMEM(state_shape, F32)],
        compiler_params=pltpu.CompilerParams(
            dimension_semantics=("arbitrary",), vmem_limit_bytes=VMEM_LIMIT_BYTES),
        name="ret_states",
    )(z, z, *consts, sf0, sb0)


def _mixer_kernel(*refs, mod_row):
    (zmix_ref, sf_ref, sb_ref, zgate_ref, x_ref,
     dmat_ref, qdf_ref, qdb_ref, retg_ref, pmask_ref, pinv_ref, poolw_ref, pscale_ref,
     sgng_ref, sgw_ref, btab_ref, wbr_ref, wbp_ref, wbs_ref, wout_ref,
     gate_ref, n2g_ref, shift2_ref, scale2_ref,
     x1_ref, h2_ref,
     y_scr, ret_cur, pool_cur, sg_cur, ret_prev, pool_prev, sg_prev) = refs
    tb = x_ref.shape[0]
    nck = tb // CHUNK
    contract_last = (((1,), (1,)), ((), ()))
    mr = slice(mod_row, mod_row + 1)

    def merge_piece(i):
        cs = slice(i * MERGE_TILE, (i + 1) * MERGE_TILE)
        def gate(branch):
            c0 = branch * D_MODEL + i * MERGE_TILE
            return jax.nn.sigmoid(zgate_ref[:, c0:c0 + MERGE_TILE].astype(F32))

        y = gate(0) * jnp.dot(ret_prev[...], wbr_ref[:, cs], preferred_element_type=F32)
        y = y + gate(1) * jnp.dot(pool_prev[...], wbp_ref[:, cs], preferred_element_type=F32)
        y = y + gate(2) * jnp.dot(sg_prev[...], wbs_ref[:, cs], preferred_element_type=F32)
        y_scr[:, cs] = y.astype(BF16)

    def out_piece(i):
        cs = slice(i * MERGE_TILE, (i + 1) * MERGE_TILE)
        out = jnp.dot(y_scr[...], wout_ref[:, cs], preferred_element_type=F32)
        x1_ref[:, cs] = x_ref[:, cs] + gate_ref[mr, cs] * out

    def ret_unit(c, h):
        rows = slice(c * CHUNK, (c + 1) * CHUNK)
        hs = slice(h * HEAD_DIM, (h + 1) * HEAD_DIM)
        head = lambda col: zmix_ref[rows, col + h * HEAD_DIM:col + (h + 1) * HEAD_DIM]
        qb = head(Q_COL)
        scores = lax.dot_general(qb, head(K_COL), contract_last, preferred_element_type=F32)
        scores = (scores * dmat_ref[h]).astype(BF16)
        cross = qdf_ref[:, hs] * jnp.dot(qb, sf_ref[c, h], preferred_element_type=F32)
        cross = cross + qdb_ref[:, hs] * jnp.dot(qb, sb_ref[c, h], preferred_element_type=F32)

        def second():
            o = jnp.dot(scores, head(V_COL), preferred_element_type=F32) + cross
            g = head(G_COL).astype(F32)
            ret = _rms(o) * retg_ref[:, hs] * (g * jax.nn.sigmoid(g))
            ret_cur[rows, hs] = ret.astype(BF16)

        return second

    def pool_unit(gi):
        gs = slice(gi * GROUP_DIM, (gi + 1) * GROUP_DIM)
        pb = zmix_ref[:, POOL_COL + gi * GROUP_DIM:POOL_COL + (gi + 1) * GROUP_DIM]
        win = jnp.dot(pmask_ref[gi], pb, preferred_element_type=F32)
        pooled = (win * pinv_ref[:, gs] - pb.astype(F32)).astype(BF16)

        def second():
            mapped = jnp.dot(pooled, poolw_ref[gi], preferred_element_type=F32)
            pool_cur[:, gs] = (mapped * pscale_ref[:, gs]).astype(BF16)

        return second

    def sg_unit(c):
        rows = slice(c * CHUNK, (c + 1) * CHUNK)
        sv = zmix_ref[rows, SV_COL:SV_COL + SG_WIDTH].astype(F32)
        sv = (_rms(jax.nn.gelu(sv)) * sgng_ref[...]).astype(BF16)

        def second():
            for gi in range(SG_GROUPS):
                gs = slice(gi * GROUP_DIM, (gi + 1) * GROUP_DIM)
                mixed = jnp.dot(sgw_ref[gi], sv[:, gs], preferred_element_type=F32)
                mixed = mixed + btab_ref[:, gs]
                u = zmix_ref[rows, U_COL + gi * GROUP_DIM:U_COL + (gi + 1) * GROUP_DIM]
                u = jax.nn.gelu(u.astype(F32))
                sg_cur[rows, gs] = (u * mixed).astype(BF16)

        return second

    mix_units = [functools.partial(ret_unit, c, h) for c in range(nck) for h in range(RET_HEADS)]
    mix_units += [functools.partial(pool_unit, gi) for gi in range(POOL_GROUPS)]
    mix_units += [functools.partial(sg_unit, c) for c in range(nck)]
    pieces = ([functools.partial(merge_piece, i) for i in range(N_MERGE)]
              + [functools.partial(out_piece, i) for i in range(N_MERGE)])

    def run(mix, merge):
        units = mix_units if mix else []
        per_piece = -(-len(units) // len(pieces))
        pending = [unit() for unit in units[:per_piece]]
        for i, piece in enumerate(pieces):
            if merge:
                piece()
            for second in pending:
                second()
            pending = [unit() for unit in units[(i + 1) * per_piece:(i + 2) * per_piece]]
        assert not pending
        if merge:
            gain2 = n2g_ref[...] * (1.0 + scale2_ref[mr, :])
            h2_ref[...] = (_rms(x1_ref[...]) * gain2 + shift2_ref[mr, :]).astype(BF16)
        if mix:
            ret_prev[...] = ret_cur[...]
            pool_prev[...] = pool_cur[...]
            sg_prev[...] = sg_cur[...]

    s = pl.program_id(0)
    last = pl.num_programs(0) - 1
    pl.when(s == 0)(functools.partial(run, True, False))
    pl.when(jnp.logical_and(s > 0, s < last))(functools.partial(run, True, True))
    pl.when(s == last)(functools.partial(run, False, True))


def _mixer(z_mix, z_gate, x, sf_all, sb_all, l, tabs, pool_tabs, p, mods_all, mod_row, tb):
    t = x.shape[0]
    nb = t // tb
    nck = tb // CHUNK
    cur = lambda s: jnp.minimum(s, nb - 1)
    prev = lambda s: jnp.maximum(s - 1, 0)

    state_spec = pl.BlockSpec((nck, RET_HEADS, HEAD_DIM, HEAD_DIM), lambda s: (cur(s), 0, 0, 0))
    row_spec = pl.BlockSpec((tb, D_MODEL), lambda s: (prev(s), 0))
    in_specs = [pl.BlockSpec((tb, MIX_COLS), lambda s: (cur(s), 0)), state_spec, state_spec,
                pl.BlockSpec((tb, GATE_COLS), lambda s: (prev(s), 0)), row_spec]
    args = [z_mix, sf_all, sb_all, z_gate, x]
    pmask, pinv = pool_tabs
    layer_consts = [tabs["dmat"], tabs["qdf"], tabs["qdb"], p["ret_norm_g"]]
    in_specs += [_layer_spec(a, l) for a in layer_consts] + [_const_spec(pmask), _const_spec(pinv)]
    args += layer_consts + [pmask, pinv]
    layer_consts = [p["pool_w"], p["pool_scale"], p["sg_norm_g"], p["sg_w"], tabs["btab"],
                    p["w_br"], p["w_bp"], p["w_bs"], p["w_out"]]
    in_specs += [_layer_spec(a, l) for a in layer_consts]
    args += layer_consts
    in_specs += [_mod_spec(l, 2), _gain_spec(l), _mod_spec(l, 3), _mod_spec(l, 4)]
    args += [mods_all, p["norm2_g"], mods_all, mods_all]
    mix_scratch = [pltpu.VMEM((tb, RET_WIDTH), BF16), pltpu.VMEM((tb, POOL_WIDTH), BF16),
                   pltpu.VMEM((tb, SG_WIDTH), BF16)]
    return pl.pallas_call(
        functools.partial(_mixer_kernel, mod_row=mod_row),
        grid=(nb + 1,),
        in_specs=in_specs,
        out_specs=[row_spec, row_spec],
        out_shape=[jax.ShapeDtypeStruct((t, D_MODEL), F32),
                   jax.ShapeDtypeStruct((t, D_MODEL), BF16)],
        scratch_shapes=[pltpu.VMEM((tb, D_MODEL), BF16)] + mix_scratch + mix_scratch,
        compiler_params=pltpu.CompilerParams(
            dimension_semantics=("arbitrary",), vmem_limit_bytes=VMEM_LIMIT_BYTES),
        name="mixer",
    )(*args)


EPILOGUE_ROWS = 16
CAST_EVERY = 2


def _mlp_kernel(*refs, mod_row, final, n_cast):
    refs = list(refs)
    cast_src = cast_dst = ()
    if n_cast:
        n_dst = n_cast + 1
        cast_dst = refs[-n_dst:]
        refs = refs[:-n_dst]
        n_out = 1 if final else 2
        cast_src = refs[-n_out - n_cast:-n_out]
        refs = refs[:-n_out - n_cast] + refs[-n_out:]
    if final:
        h2_ref, x1_ref, w1_ref, w2_ref, gate_ref, fng_ref, o_ref = refs
    else:
        (h2_ref, x1_ref, w1_ref, w2_ref, gate_ref, ng_ref, nshift_ref, nscale_ref,
         o_ref, hn_ref) = refs
    f = pl.program_id(1)
    mr = slice(mod_row, mod_row + 1)

    def step(first):
        a = jnp.maximum(jnp.dot(h2_ref[...], w1_ref[...], preferred_element_type=F32), 0.0)
        part = jnp.dot((a * a).astype(BF16), w2_ref[...], preferred_element_type=F32)
        if first:
            o_ref[...] = part
        else:
            o_ref[...] += part
        part_id = (pl.program_id(0) * pl.num_programs(1) + f) % CAST_EVERY
        for k, src in enumerate(cast_src):
            share = src.shape[0] // CAST_EVERY
            rows = pl.ds(pl.multiple_of(part_id * share, share), share)
            if k == 0:
                _cast_w_in(src[rows, :], cast_dst[0].at[rows], cast_dst[1].at[rows])
            else:
                cast_dst[k + 1][rows, :] = src[rows, :].astype(BF16)

    @pl.when(f == 0)
    def _():
        step(True)

    @pl.when(f > 0)
    def _():
        step(False)

    @pl.when(f == pl.num_programs(1) - 1)
    def _():
        gate = gate_ref[mr, :]
        if final:
            gain = fng_ref[...]
        else:
            gain = ng_ref[...] * (1.0 + nscale_ref[mr, :])
            shift = nshift_ref[mr, :]

        for r in range(o_ref.shape[0] // EPILOGUE_ROWS):
            rows = slice(r * EPILOGUE_ROWS, (r + 1) * EPILOGUE_ROWS)
            x2 = x1_ref[rows, :] + gate * o_ref[rows, :]
            if final:
                o_ref[rows, :] = _rms(x2) * gain
            else:
                o_ref[rows, :] = x2
                hn_ref[rows, :] = (_rms(x2) * gain + shift).astype(BF16)


def _mlp(h2, x1, l, w1, w2, mods_all, mod_row, norm1_g, final_gain, tm, tf, cast_next=()):
    t = x1.shape[0]
    final = final_gain is not None
    n_f = D_FF // tf
    grid = (t // tm, n_f)
    rows = pl.BlockSpec((tm, D_MODEL), lambda i, f: (i, 0))
    in_specs = [rows, rows,
                pl.BlockSpec((D_MODEL, tf), lambda i, f: (0, f)),
                pl.BlockSpec((tf, D_MODEL), lambda i, f: (f, 0)),
                _mod_spec(l, 5)]
    args = [h2, x1, w1, w2, mods_all]
    x_shape = jax.ShapeDtypeStruct((t, D_MODEL), F32)
    if final:
        in_specs.append(pl.BlockSpec((1, D_MODEL), lambda i, f: (0, 0)))
        args.append(final_gain)
        out_specs, out_shape = [rows], [x_shape]
    else:
        in_specs += [_gain_spec(l + 1), _mod_spec(l + 1, 0), _mod_spec(l + 1, 1)]
        args += [norm1_g, mods_all, mods_all]
        out_specs = [rows, rows]
        out_shape = [x_shape, jax.ShapeDtypeStruct((t, D_MODEL), BF16)]
    for k, w in enumerate(cast_next):
        n_slabs = grid[0] * grid[1] // CAST_EVERY
        slab, cols = w.shape[1] // n_slabs, w.shape[2]
        assert slab * n_slabs == w.shape[1] and slab % (2 * SUBLANES) == 0
        in_specs.append(pl.BlockSpec((None, slab, cols),
                                     lambda i, f: (l + 1, (i * n_f + f) // CAST_EVERY, 0)))
        args.append(w)
        for out_cols in ((MIX_COLS, GATE_COLS) if k == 0 else (cols,)):
            out_specs.append(pl.BlockSpec((slab, out_cols),
                                          lambda i, f: ((i * n_f + f) // CAST_EVERY, 0)))
            out_shape.append(jax.ShapeDtypeStruct((w.shape[1], out_cols), BF16))
    return pl.pallas_call(
        functools.partial(_mlp_kernel, mod_row=mod_row, final=final, n_cast=len(cast_next)),
        grid=grid,
        in_specs=in_specs,
        out_specs=out_specs,
        out_shape=out_shape,
        compiler_params=pltpu.CompilerParams(
            dimension_semantics=("arbitrary", "arbitrary"), vmem_limit_bytes=VMEM_LIMIT_BYTES),
        name="mlp",
    )(*args)


def _rope_tables(t):
    half = HEAD_DIM // 2
    nf = half // 2
    tok = np.arange(t)
    inv = ROPE_THETA ** (-np.arange(nf, dtype=np.float64) / nf)
    lane = np.arange(HEAD_DIM)
    pos = np.where(lane[None, :] < half, (tok // GRID_W)[:, None], (tok % GRID_W)[:, None])
    ang = pos * inv[lane % nf][None, :]
    lower = (lane % half) < nf
    sin = np.sin(ang)
    cos = np.cos(ang).astype(np.float32)
    sin_lo = np.where(lower[None, :], -sin, 0.0).astype(np.float32)
    sin_hi = np.where(lower[None, :], 0.0, sin).astype(np.float32)
    return jnp.asarray(cos), jnp.asarray(sin_lo), jnp.asarray(sin_hi)


def _pool_tables(tb, seg_len):
    pos = np.arange(tb) % seg_len
    base = np.arange(tb) - pos
    col = np.arange(tb)[None, :]
    masks, invs = [], []
    for w in POOL_WINDOWS:
        lo = np.maximum(pos - w // 2, 0)
        hi = np.minimum(pos + w // 2 - 1, seg_len - 1)
        masks.append((col >= (base + lo)[:, None]) & (col <= (base + hi)[:, None]))
        invs.append(np.repeat((1.0 / (hi - lo + 1))[:, None], GROUP_DIM, axis=1))
    pmask = jnp.asarray(np.stack(masks).astype(np.float32), dtype=BF16)
    pinv = jnp.asarray(np.concatenate(invs, axis=1).astype(np.float32))
    return pmask, pinv


def _decay_tables(logit):
    lg = jax.nn.log_sigmoid(logit.astype(F32))
    lgf, lgb = lg[0], lg[1]
    idx = jnp.arange(CHUNK, dtype=F32)
    dist = idx[:, None] - idx[None, :]
    fwd = jnp.exp(lgf[:, None, None] * jnp.maximum(dist, 0.0))
    bwd = jnp.exp(lgb[:, None, None] * jnp.maximum(-dist, 0.0))
    dmat = jnp.where(dist > 0, fwd, jnp.where(dist < 0, bwd, 2.0))

    def lanes(tab):
        return jnp.repeat(tab.T, HEAD_DIM, axis=1)

    return dict(
        dmat=dmat,
        qdf=lanes(jnp.exp(lgf[:, None] * (idx + 1.0)[None])),
        qdb=lanes(jnp.exp(lgb[:, None] * (CHUNK - idx)[None])),
        kdf=lanes(jnp.exp(lgf[:, None] * (CHUNK - 1.0 - idx)[None])),
        kdb=lanes(jnp.exp(lgb[:, None] * idx[None])),
        cdf=jnp.broadcast_to(jnp.exp(lgf * CHUNK)[:, None, None], (RET_HEADS, 1, HEAD_DIM)),
        cdb=jnp.broadcast_to(jnp.exp(lgb * CHUNK)[:, None, None], (RET_HEADS, 1, HEAD_DIM)),
    )


LATENT_ROWS_IN = 1024
LATENT_ROWS_MIX = 256
LATENT_ROWS_MLP = 512
LATENT_FF_TILE = 1024
LATENT_ROWS_NORM = 512
LATENT_STATE_CHUNKS = 4
FF_TILE = 1024


def kernel(x, c, ctx, c_ctx, w_ada, b_ada, norm1_g, w_in, ret_decay_logit, ret_norm_g, pool_w,
           pool_scale, sg_norm_g, sg_w, sg_b, w_br, w_bp, w_bs, w_out, norm2_g, w1, w2, final_norm_g):
    assert x.shape[0] == 1 and ctx.shape[0] == 1
    t = x.shape[1]
    tc = ctx.shape[1]
    xs = x[0]
    xc = ctx[0]

    cond = jnp.stack([c[0], c_ctx])[:, :, None]
    mods_all = _ada(cond, w_ada, b_ada)

    rope = _rope_tables(t)
    pool_lat = _pool_tables(LATENT_ROWS_MIX, GRID_W)
    pool_ctx = _pool_tables(tc, tc)
    zero_state = jnp.zeros((RET_HEADS, HEAD_DIM, HEAD_DIM), F32)

    tabs = jax.vmap(_decay_tables)(ret_decay_logit)
    tabs["btab"] = jnp.repeat(jnp.swapaxes(sg_b, 1, 2), GROUP_DIM, axis=2)
    vec = lambda v: v.reshape(DEPTH, 1, -1)
    p = dict(ret_norm_g=vec(ret_norm_g), pool_w=pool_w.astype(BF16), pool_scale=vec(pool_scale),
             sg_norm_g=vec(sg_norm_g), sg_w=sg_w.astype(BF16),
             w_br=w_br.astype(BF16), w_bp=w_bp.astype(BF16), w_bs=w_bs.astype(BF16),
             w_out=w_out.astype(BF16), norm2_g=vec(norm2_g))
    n1g = vec(norm1_g)
    fng = final_norm_g.reshape(1, -1)
    streamed = (w_in, w1, w2)
    w_mix_b, w_gate_b = _cast_first_w_in(w_in)
    w1_b, w2_b = w1[0].astype(BF16), w2[0].astype(BF16)

    hc = _norm_mod(xc, 0, n1g, mods_all, ROW_CONTEXT, tc)
    hx = _norm_mod(xs, 0, n1g, mods_all, ROW_LATENT, LATENT_ROWS_NORM)
    for l in range(DEPTH):
        last = l == DEPTH - 1

        zc = _in_proj(hc, w_mix_b, True, None, tc)
        sfc, sbc, sf, sb = _states(zc, l, tabs, zero_state, zero_state, tc // CHUNK)
        if not last:
            gc = _in_proj(hc, w_gate_b, False, None, tc)
            xc1, hc2 = _mixer(zc, gc, xc, sfc, sbc, l, tabs, pool_ctx, p, mods_all, ROW_CONTEXT, tc)
            xc, hc = _mlp(hc2, xc1, l, w1_b, w2_b, mods_all, ROW_CONTEXT, n1g, None, tc, FF_TILE)

        zx = _in_proj(hx, w_mix_b, True, rope, LATENT_ROWS_IN)
        gx = _in_proj(hx, w_gate_b, False, None, LATENT_ROWS_IN)
        sfx, sbx, _, _ = _states(zx, l, tabs, sf, sb, LATENT_STATE_CHUNKS)
        x1, h2 = _mixer(zx, gx, xs, sfx, sbx, l, tabs, pool_lat, p, mods_all, ROW_LATENT,
                        LATENT_ROWS_MIX)
        if last:
            xs, = _mlp(h2, x1, l, w1_b, w2_b, mods_all, ROW_LATENT, n1g, fng,
                       LATENT_ROWS_MLP, LATENT_FF_TILE)
        else:
            xs, hx, w_mix_b, w_gate_b, w1_b, w2_b = _mlp(
                h2, x1, l, w1_b, w2_b, mods_all, ROW_LATENT, n1g, None,
                LATENT_ROWS_MLP, LATENT_FF_TILE, cast_next=streamed)

    return xs[None]
```

```python
import functools

import numpy as np
import jax
import jax.numpy as jnp
from jax import lax
from jax.experimental import pallas as pl
from jax.experimental.pallas import tpu as pltpu

F32 = jnp.float32
BF16 = jnp.bfloat16

D_MODEL = 2048
DEPTH = 4
GRID_W = 64
EPS = 1e-6
N_MOD = 6

HEAD_DIM = 128
RET_WIDTH = D_MODEL // 2
RET_HEADS = RET_WIDTH // HEAD_DIM
CHUNK = 128
K_SCALE = HEAD_DIM ** -0.5
ROPE_THETA = 10000.0

POOL_WIDTH = D_MODEL // 4
POOL_WINDOWS = (2, 4, 8, 16)
POOL_GROUPS = len(POOL_WINDOWS)
GROUP_DIM = POOL_WIDTH // POOL_GROUPS

SG_WIDTH = D_MODEL // 4
SG_GROUPS = 4

D_FF = 4 * D_MODEL
N_IN = 4 * RET_WIDTH + POOL_WIDTH + 2 * SG_WIDTH + 3 * D_MODEL

SUBLANES = 8
LANES = 128
VMEM_LIMIT_BYTES = 56 * 1024 * 1024

MIX_COLS = 4 * RET_WIDTH + POOL_WIDTH + 2 * SG_WIDTH
GATE_COLS = 3 * D_MODEL
IN_TILES = 2
QK_COLS = 2 * RET_WIDTH
SUB_TILE = 512
_HEADS_PER_SUB = SUB_TILE // HEAD_DIM
MERGE_TILE = 512
N_MERGE = D_MODEL // MERGE_TILE
Q_COL, K_COL, V_COL, G_COL = (k * RET_WIDTH for k in range(4))
POOL_COL = 4 * RET_WIDTH
U_COL = POOL_COL + POOL_WIDTH
SV_COL = U_COL + SG_WIDTH


def _cast_w_in(src, mix_ref, gate_ref):
    mix_ref[...] = src[:, :MIX_COLS].astype(BF16)
    gate_ref[...] = src[:, MIX_COLS:].astype(BF16)

ADA_TILE = 1024
ADA_ROWS = 8
ROW_LATENT, ROW_CONTEXT = 0, 1


def _layer_spec(arr, l):
    nd = arr.ndim - 1
    return pl.BlockSpec((None,) + arr.shape[1:], lambda *_: (l,) + (0,) * nd,
                        pipeline_mode=pl.Buffered(1))


def _const_spec(arr):
    nd = arr.ndim
    return pl.BlockSpec(arr.shape, lambda *_: (0,) * nd, pipeline_mode=pl.Buffered(1))


def _mod_spec(l, k):
    return pl.BlockSpec((None, ADA_ROWS, D_MODEL), lambda *_: (l, 0, k))


def _gain_spec(l):
    return pl.BlockSpec((None, 1, D_MODEL), lambda *_: (l, 0, 0))


def _rms(x):
    return x * lax.rsqrt(jnp.mean(x * x, axis=-1, keepdims=True) + EPS)


def _norm_modulate(x, gain, shift, scale):
    return _rms(x) * gain * (1.0 + scale) + shift


def _ada_kernel(cond_ref, w_ref, b_ref, o_ref, act_scr):
    tn = w_ref.shape[1]

    @pl.when(jnp.logical_and(pl.program_id(0) == 0, pl.program_id(1) == 0))
    def _():
        for r in range(2):
            a = cond_ref[r]
            act_scr[r] = jnp.broadcast_to(a * jax.nn.sigmoid(a), (D_MODEL, LANES))

    def body(g, accs):
        r0 = pl.multiple_of(g * SUBLANES, SUBLANES)
        w = w_ref[pl.ds(r0, SUBLANES), :]
        out = []
        for r in range(2):
            a = jnp.tile(act_scr[r, pl.ds(r0, SUBLANES), :], (1, tn // LANES))
            out.append(accs[r] + w * a)
        return tuple(out)

    zero = jnp.zeros((SUBLANES, tn), F32)
    accs = lax.fori_loop(0, D_MODEL // SUBLANES, body, (zero, zero), unroll=8)
    o_ref[...] = jnp.zeros((ADA_ROWS, tn), F32)
    for r in range(2):
        o_ref[r:r + 1, :] = jnp.sum(accs[r], axis=0, keepdims=True) + b_ref[...]


def _ada(cond, w_ada, b_ada):
    n = N_MOD * D_MODEL
    return pl.pallas_call(
        _ada_kernel,
        grid=(DEPTH, n // ADA_TILE),
        in_specs=[
            pl.BlockSpec((2, D_MODEL, 1), lambda l, j: (0, 0, 0)),
            pl.BlockSpec((None, D_MODEL, ADA_TILE), lambda l, j: (l, 0, j)),
            pl.BlockSpec((None, 1, ADA_TILE), lambda l, j: (l, 0, j)),
        ],
        out_specs=pl.BlockSpec((None, ADA_ROWS, ADA_TILE), lambda l, j: (l, 0, j)),
        out_shape=jax.ShapeDtypeStruct((DEPTH, ADA_ROWS, n), F32),
        scratch_shapes=[pltpu.VMEM((2, D_MODEL, LANES), F32)],
        compiler_params=pltpu.CompilerParams(
            dimension_semantics=("arbitrary", "arbitrary"), vmem_limit_bytes=VMEM_LIMIT_BYTES),
        name="ada",
    )(cond, w_ada, b_ada.reshape(DEPTH, 1, n))


def _norm_kernel(x_ref, g_ref, shift_ref, scale_ref, h_ref, *, mod_row):
    mr = slice(mod_row, mod_row + 1)
    h_ref[...] = _norm_modulate(x_ref[...], g_ref[...], shift_ref[mr, :], scale_ref[mr, :]).astype(BF16)


def _norm_mod(x, l, norm1_g, mods_all, mod_row, tm):
    t = x.shape[0]
    return pl.pallas_call(
        functools.partial(_norm_kernel, mod_row=mod_row),
        grid=(t // tm,),
        in_specs=[pl.BlockSpec((tm, D_MODEL), lambda i: (i, 0)),
                  _gain_spec(l), _mod_spec(l, 0), _mod_spec(l, 1)],
        out_specs=pl.BlockSpec((tm, D_MODEL), lambda i: (i, 0)),
        out_shape=jax.ShapeDtypeStruct((t, D_MODEL), BF16),
        compiler_params=pltpu.CompilerParams(
            dimension_semantics=("arbitrary",), vmem_limit_bytes=VMEM_LIMIT_BYTES),
        name="norm_mod",
    )(x, norm1_g, mods_all, mods_all)


def _rope(t, cos, sin_lo, sin_hi):
    up = pltpu.roll(t, HEAD_DIM - 32, axis=1)
    down = pltpu.roll(t, 32, axis=1)
    return t * cos + up * sin_lo + down * sin_hi


def _inproj_kernel(*refs, mix_part, use_rope):
    if use_rope:
        h_ref, w_ref, cos_ref, slo_ref, shi_ref, z_ref = refs
    else:
        h_ref, w_ref, z_ref = refs
    j = pl.program_id(1)

    if not mix_part:
        z_ref[...] = jnp.dot(h_ref[...], w_ref[...], preferred_element_type=F32).astype(BF16)
        return

    @pl.when(j == 0)
    def _():
        for blk in range(QK_COLS // SUB_TILE):
            c0 = blk * SUB_TILE
            acc = jnp.dot(h_ref[...], w_ref[:, c0:c0 + SUB_TILE], preferred_element_type=F32)
            if c0 >= RET_WIDTH:
                acc = acc * K_SCALE
            if use_rope:
                cos, slo, shi = cos_ref[...], slo_ref[...], shi_ref[...]
                for hh in range(_HEADS_PER_SUB):
                    hs = slice(hh * HEAD_DIM, (hh + 1) * HEAD_DIM)
                    z_ref[:, c0 + hh * HEAD_DIM:c0 + (hh + 1) * HEAD_DIM] = (
                        _rope(acc[:, hs], cos, slo, shi).astype(BF16))
            else:
                z_ref[:, c0:c0 + SUB_TILE] = acc.astype(BF16)
        z_ref[:, QK_COLS:] = jnp.dot(h_ref[...], w_ref[:, QK_COLS:],
                                     preferred_element_type=F32).astype(BF16)

    @pl.when(j != 0)
    def _():
        z_ref[...] = jnp.dot(h_ref[...], w_ref[...], preferred_element_type=F32).astype(BF16)


def _in_proj(h, w_part, mix_part, rope, tm):
    t = h.shape[0]
    width = w_part.shape[1]
    tile = width // IN_TILES
    assert tile * IN_TILES == width and tile % LANES == 0 and (not mix_part or tile >= QK_COLS)
    use_rope = mix_part and rope is not None
    in_specs = [
        pl.BlockSpec((tm, D_MODEL), lambda i, j: (i, 0)),
        pl.BlockSpec((D_MODEL, tile), lambda i, j: (0, j)),
    ]
    args = [h, w_part]
    if use_rope:
        in_specs += [pl.BlockSpec((tm, HEAD_DIM), lambda i, j: (i, 0))] * 3
        args += list(rope)
    return pl.pallas_call(
        functools.partial(_inproj_kernel, mix_part=mix_part, use_rope=use_rope),
        grid=(t // tm, IN_TILES),
        in_specs=in_specs,
        out_specs=pl.BlockSpec((tm, tile), lambda i, j: (i, j)),
        out_shape=jax.ShapeDtypeStruct((t, width), BF16),
        compiler_params=pltpu.CompilerParams(
            dimension_semantics=("arbitrary", "arbitrary"), vmem_limit_bytes=VMEM_LIMIT_BYTES),
        name="in_proj_mix" if mix_part else "in_proj_gate",
    )(*args)


def _inproj_cast_kernel(h_ref, w_ref, z_ref, wb_ref, *, col0):
    tile = w_ref.shape[1]
    wb = w_ref[...].astype(BF16)
    wb_ref[...] = wb
    acc = jnp.dot(h_ref[...], wb, preferred_element_type=F32)
    col = col0 + pl.program_id(0) * tile + lax.broadcasted_iota(jnp.int32, (1, tile), 1)
    is_k = jnp.logical_and(col >= K_COL, col < K_COL + RET_WIDTH)
    z_ref[...] = (acc * jnp.where(is_k, K_SCALE, 1.0)).astype(BF16)


def _in_proj_cast(h, w_in, col0, width, tile):
    t = h.shape[0]
    assert width % tile == 0 and col0 % tile == 0 and tile % LANES == 0
    return pl.pallas_call(
        functools.partial(_inproj_cast_kernel, col0=col0),
        grid=(width // tile,),
        in_specs=[pl.BlockSpec((t, D_MODEL), lambda j: (0, 0)),
                  pl.BlockSpec((None, D_MODEL, tile), lambda j: (0, 0, col0 // tile + j))],
        out_specs=[pl.BlockSpec((t, tile), lambda j: (0, j)),
                   pl.BlockSpec((D_MODEL, tile), lambda j: (0, j))],
        out_shape=[jax.ShapeDtypeStruct((t, width), BF16),
                   jax.ShapeDtypeStruct((D_MODEL, width), BF16)],
        compiler_params=pltpu.CompilerParams(
            dimension_semantics=("arbitrary",), vmem_limit_bytes=VMEM_LIMIT_BYTES),
        name="in_proj_cast",
    )(h, w_in)


def _state_kernel(k_ref, v_ref, kdf_ref, kdb_ref, cdf_ref, cdb_ref, sf0_ref, sb0_ref,
                  sf_all_ref, sb_all_ref, sf_fin_ref, sb_fin_ref, sf_scr, sb_scr):
    t = pl.program_id(0)
    nb = pl.num_programs(0)
    cps = sf_all_ref.shape[0]

    @pl.when(t == 0)
    def _():
        sf_scr[...] = sf0_ref[...]
        sb_scr[...] = sb0_ref[...]

    contract_rows = (((0,), (0,)), ((), ()))

    def advance(block, c, kd_ref, cd_ref, all_ref, scr):
        rows = pl.ds(pl.multiple_of((block * cps + c) * CHUNK, CHUNK), CHUNK)
        all_ref[c] = scr[...].astype(BF16)
        for h in range(RET_HEADS):
            hs = slice(h * HEAD_DIM, (h + 1) * HEAD_DIM)
            kd = (k_ref[rows, hs].astype(F32) * kd_ref[:, hs]).astype(BF16)
            kv = lax.dot_general(kd, v_ref[rows, hs], contract_rows, preferred_element_type=F32)
            scr[h] = scr[h] * cd_ref[h] + kv

    for c in range(cps):
        advance(t, c, kdf_ref, cdf_ref, sf_all_ref, sf_scr)
        advance(nb - 1 - t, cps - 1 - c, kdb_ref, cdb_ref, sb_all_ref, sb_scr)

    @pl.when(t == pl.num_programs(0) - 1)
    def _():
        sf_fin_ref[...] = sf_scr[...]
        sb_fin_ref[...] = sb_scr[...]


def _states(z, l, tabs, sf0, sb0, cps):
    t = z.shape[0]
    nb = t // (cps * CHUNK)
    whole = lambda c: pl.BlockSpec((t, RET_WIDTH), lambda i: (0, c), pipeline_mode=pl.Buffered(1))
    state_shape = (RET_HEADS, HEAD_DIM, HEAD_DIM)
    consts = [tabs["kdf"], tabs["kdb"], tabs["cdf"], tabs["cdb"]]
    kb, vb = K_COL // RET_WIDTH, V_COL // RET_WIDTH
    in_specs = [whole(kb), whole(vb)] + [_layer_spec(a, l) for a in consts]
    in_specs += [_const_spec(sf0), _const_spec(sb0)]
    all_shape = jax.ShapeDtypeStruct((t // CHUNK,) + state_shape, BF16)
    fin_shape = jax.ShapeDtypeStruct(state_shape, F32)
    return pl.pallas_call(
        _state_kernel,
        grid=(nb,),
        in_specs=in_specs,
        out_specs=[
            pl.BlockSpec((cps,) + state_shape, lambda i: (i, 0, 0, 0)),
            pl.BlockSpec((cps,) + state_shape, lambda i: (nb - 1 - i, 0, 0, 0)),
            pl.BlockSpec(state_shape, lambda i: (0, 0, 0)),
            pl.BlockSpec(state_shape, lambda i: (0, 0, 0)),
        ],
        out_shape=[all_shape, all_shape, fin_shape, fin_shape],
        scratch_shapes=[pltpu.VMEM(state_shape, F32), pltpu.VMEM(state_shape, F32)],
        compiler_params=pltpu.CompilerParams(
            dimension_semantics=("arbitrary",), vmem_limit_bytes=VMEM_LIMIT_BYTES),
        name="ret_states",
    )(z, z, *consts, sf0, sb0)


def _mixer_kernel(*refs, mod_row):
    (zmix_ref, sf_ref, sb_ref, zgate_ref, x_ref,
     dmat_ref, qdf_ref, qdb_ref, retg_ref, pmask_ref, pinv_ref, poolw_ref, pscale_ref,
     sgng_ref, sgw_ref, btab_ref, wbr_ref, wbp_ref, wbs_ref, wout_ref,
     gate_ref, n2g_ref, shift2_ref, scale2_ref,
     x1_ref, h2_ref,
     y_scr, ret_cur, pool_cur, sg_cur, ret_prev, pool_prev, sg_prev) = refs
    tb = x_ref.shape[0]
    nck = tb // CHUNK
    contract_last = (((1,), (1,)), ((), ()))
    mr = slice(mod_row, mod_row + 1)

    def merge_piece(i):
        cs = slice(i * MERGE_TILE, (i + 1) * MERGE_TILE)
        def gate(branch):
            c0 = branch * D_MODEL + i * MERGE_TILE
            return jax.nn.sigmoid(zgate_ref[:, c0:c0 + MERGE_TILE].astype(F32))

        y = gate(0) * jnp.dot(ret_prev[...], wbr_ref[:, cs], preferred_element_type=F32)
        y = y + gate(1) * jnp.dot(pool_prev[...], wbp_ref[:, cs], preferred_element_type=F32)
        y = y + gate(2) * jnp.dot(sg_prev[...], wbs_ref[:, cs], preferred_element_type=F32)
        y_scr[:, cs] = y.astype(BF16)

    def out_piece(i):
        cs = slice(i * MERGE_TILE, (i + 1) * MERGE_TILE)
        out = jnp.dot(y_scr[...], wout_ref[:, cs], preferred_element_type=F32)
        x1_ref[:, cs] = x_ref[:, cs] + gate_ref[mr, cs] * out

    def ret_unit(c, h):
        rows = slice(c * CHUNK, (c + 1) * CHUNK)
        hs = slice(h * HEAD_DIM, (h + 1) * HEAD_DIM)
        head = lambda col: zmix_ref[rows, col + h * HEAD_DIM:col + (h + 1) * HEAD_DIM]
        qb = head(Q_COL)
        scores = lax.dot_general(qb, head(K_COL), contract_last, preferred_element_type=F32)
        scores = (scores * dmat_ref[h]).astype(BF16)
        cross = qdf_ref[:, hs] * jnp.dot(qb, sf_ref[c, h], preferred_element_type=F32)
        cross = cross + qdb_ref[:, hs] * jnp.dot(qb, sb_ref[c, h], preferred_element_type=F32)

        def second():
            o = jnp.dot(scores, head(V_COL), preferred_element_type=F32) + cross
            g = head(G_COL).astype(F32)
            ret = _rms(o) * retg_ref[:, hs] * (g * jax.nn.sigmoid(g))
            ret_cur[rows, hs] = ret.astype(BF16)

        return second

    def pool_unit(gi):
        gs = slice(gi * GROUP_DIM, (gi + 1) * GROUP_DIM)
        pb = zmix_ref[:, POOL_COL + gi * GROUP_DIM:POOL_COL + (gi + 1) * GROUP_DIM]
        win = jnp.dot(pmask_ref[gi], pb, preferred_element_type=F32)
        pooled = (win * pinv_ref[:, gs] - pb.astype(F32)).astype(BF16)

        def second():
            mapped = jnp.dot(pooled, poolw_ref[gi], preferred_element_type=F32)
            pool_cur[:, gs] = (mapped * pscale_ref[:, gs]).astype(BF16)

        return second

    def sg_unit(c):
        rows = slice(c * CHUNK, (c + 1) * CHUNK)
        sv = zmix_ref[rows, SV_COL:SV_COL + SG_WIDTH].astype(F32)
        sv = (_rms(jax.nn.gelu(sv)) * sgng_ref[...]).astype(BF16)

        def second():
            for gi in range(SG_GROUPS):
                gs = slice(gi * GROUP_DIM, (gi + 1) * GROUP_DIM)
                mixed = jnp.dot(sgw_ref[gi], sv[:, gs], preferred_element_type=F32)
                mixed = mixed + btab_ref[:, gs]
                u = zmix_ref[rows, U_COL + gi * GROUP_DIM:U_COL + (gi + 1) * GROUP_DIM]
                u = jax.nn.gelu(u.astype(F32))
                sg_cur[rows, gs] = (u * mixed).astype(BF16)

        return second

    mix_units = [functools.partial(ret_unit, c, h) for c in range(nck) for h in range(RET_HEADS)]
    mix_units += [functools.partial(pool_unit, gi) for gi in range(POOL_GROUPS)]
    mix_units += [functools.partial(sg_unit, c) for c in range(nck)]
    pieces = ([functools.partial(merge_piece, i) for i in range(N_MERGE)]
              + [functools.partial(out_piece, i) for i in range(N_MERGE)])

    def run(mix, merge):
        units = mix_units if mix else []
        per_piece = -(-len(units) // len(pieces))
        pending = [unit() for unit in units[:per_piece]]
        for i, piece in enumerate(pieces):
            if merge:
                piece()
            for second in pending:
                second()
            pending = [unit() for unit in units[(i + 1) * per_piece:(i + 2) * per_piece]]
        assert not pending
        if merge:
            gain2 = n2g_ref[...] * (1.0 + scale2_ref[mr, :])
            h2_ref[...] = (_rms(x1_ref[...]) * gain2 + shift2_ref[mr, :]).astype(BF16)
        if mix:
            ret_prev[...] = ret_cur[...]
            pool_prev[...] = pool_cur[...]
            sg_prev[...] = sg_cur[...]

    s = pl.program_id(0)
    last = pl.num_programs(0) - 1
    pl.when(s == 0)(functools.partial(run, True, False))
    pl.when(jnp.logical_and(s > 0, s < last))(functools.partial(run, True, True))
    pl.when(s == last)(functools.partial(run, False, True))


def _mixer(z_mix, z_gate, x, sf_all, sb_all, l, tabs, pool_tabs, p, mods_all, mod_row, tb):
    t = x.shape[0]
    nb = t // tb
    nck = tb // CHUNK
    cur = lambda s: jnp.minimum(s, nb - 1)
    prev = lambda s: jnp.maximum(s - 1, 0)

    state_spec = pl.BlockSpec((nck, RET_HEADS, HEAD_DIM, HEAD_DIM), lambda s: (cur(s), 0, 0, 0))
    row_spec = pl.BlockSpec((tb, D_MODEL), lambda s: (prev(s), 0))
    in_specs = [pl.BlockSpec((tb, MIX_COLS), lambda s: (cur(s), 0)), state_spec, state_spec,
                pl.BlockSpec((tb, GATE_COLS), lambda s: (prev(s), 0)), row_spec]
    args = [z_mix, sf_all, sb_all, z_gate, x]
    pmask, pinv = pool_tabs
    layer_consts = [tabs["dmat"], tabs["qdf"], tabs["qdb"], p["ret_norm_g"]]
    in_specs += [_layer_spec(a, l) for a in layer_consts] + [_const_spec(pmask), _const_spec(pinv)]
    args += layer_consts + [pmask, pinv]
    layer_consts = [p["pool_w"], p["pool_scale"], p["sg_norm_g"], p["sg_w"], tabs["btab"],
                    p["w_br"], p["w_bp"], p["w_bs"], p["w_out"]]
    in_specs += [_layer_spec(a, l) for a in layer_consts]
    args += layer_consts
    in_specs += [_mod_spec(l, 2), _gain_spec(l), _mod_spec(l, 3), _mod_spec(l, 4)]
    args += [mods_all, p["norm2_g"], mods_all, mods_all]
    mix_scratch = [pltpu.VMEM((tb, RET_WIDTH), BF16), pltpu.VMEM((tb, POOL_WIDTH), BF16),
                   pltpu.VMEM((tb, SG_WIDTH), BF16)]
    return pl.pallas_call(
        functools.partial(_mixer_kernel, mod_row=mod_row),
        grid=(nb + 1,),
        in_specs=in_specs,
        out_specs=[row_spec, row_spec],
        out_shape=[jax.ShapeDtypeStruct((t, D_MODEL), F32),
                   jax.ShapeDtypeStruct((t, D_MODEL), BF16)],
        scratch_shapes=[pltpu.VMEM((tb, D_MODEL), BF16)] + mix_scratch + mix_scratch,
        compiler_params=pltpu.CompilerParams(
            dimension_semantics=("arbitrary",), vmem_limit_bytes=VMEM_LIMIT_BYTES),
        name="mixer",
    )(*args)


EPILOGUE_ROWS = 16
CAST_EVERY = 2


def _mlp_kernel(*refs, mod_row, final, n_cast, emit_weights):
    refs = list(refs)
    w1b_ref = w2b_ref = None
    if emit_weights:
        w1b_ref, w2b_ref = refs[-2:]
        refs = refs[:-2]
    cast_src = cast_dst = ()
    if n_cast:
        n_dst = n_cast + 1
        cast_dst = refs[-n_dst:]
        refs = refs[:-n_dst]
        n_out = 1 if final else 2
        cast_src = refs[-n_out - n_cast:-n_out]
        refs = refs[:-n_out - n_cast] + refs[-n_out:]
    if final:
        h2_ref, x1_ref, w1_ref, w2_ref, gate_ref, fng_ref, o_ref = refs
    else:
        (h2_ref, x1_ref, w1_ref, w2_ref, gate_ref, ng_ref, nshift_ref, nscale_ref,
         o_ref, hn_ref) = refs
    f = pl.program_id(1)
    mr = slice(mod_row, mod_row + 1)

    def step(first):
        w1, w2 = w1_ref[...], w2_ref[...]
        if emit_weights:
            w1, w2 = w1.astype(BF16), w2.astype(BF16)
            w1b_ref[...] = w1
            w2b_ref[...] = w2
        a = jnp.maximum(jnp.dot(h2_ref[...], w1, preferred_element_type=F32), 0.0)
        part = jnp.dot((a * a).astype(BF16), w2, preferred_element_type=F32)
        if first:
            o_ref[...] = part
        else:
            o_ref[...] += part
        part_id = (pl.program_id(0) * pl.num_programs(1) + f) % CAST_EVERY
        for k, src in enumerate(cast_src):
            share = src.shape[0] // CAST_EVERY
            rows = pl.ds(pl.multiple_of(part_id * share, share), share)
            if k == 0:
                _cast_w_in(src[rows, :], cast_dst[0].at[rows], cast_dst[1].at[rows])
            else:
                cast_dst[k + 1][rows, :] = src[rows, :].astype(BF16)

    @pl.when(f == 0)
    def _():
        step(True)

    @pl.when(f > 0)
    def _():
        step(False)

    @pl.when(f == pl.num_programs(1) - 1)
    def _():
        gate = gate_ref[mr, :]
        if final:
            gain = fng_ref[...]
        else:
            gain = ng_ref[...] * (1.0 + nscale_ref[mr, :])
            shift = nshift_ref[mr, :]

        for r in range(o_ref.shape[0] // EPILOGUE_ROWS):
            rows = slice(r * EPILOGUE_ROWS, (r + 1) * EPILOGUE_ROWS)
            x2 = x1_ref[rows, :] + gate * o_ref[rows, :]
            if final:
                o_ref[rows, :] = _rms(x2) * gain
            else:
                o_ref[rows, :] = x2
                hn_ref[rows, :] = (_rms(x2) * gain + shift).astype(BF16)


def _mlp(h2, x1, l, w1, w2, mods_all, mod_row, norm1_g, final_gain, tm, tf, cast_next=()):
    t = x1.shape[0]
    final = final_gain is not None
    n_f = D_FF // tf
    grid = (t // tm, n_f)
    rows = pl.BlockSpec((tm, D_MODEL), lambda i, f: (i, 0))
    emit_weights = w1.ndim == 3
    assert not emit_weights or (grid[0] == 1 and not cast_next)
    lead = (None,) if emit_weights else ()
    at = (lambda *idx: (l,) + idx) if emit_weights else (lambda *idx: idx)
    in_specs = [rows, rows,
                pl.BlockSpec(lead + (D_MODEL, tf), lambda i, f: at(0, f)),
                pl.BlockSpec(lead + (tf, D_MODEL), lambda i, f: at(f, 0)),
                _mod_spec(l, 5)]
    args = [h2, x1, w1, w2, mods_all]
    x_shape = jax.ShapeDtypeStruct((t, D_MODEL), F32)
    if final:
        in_specs.append(pl.BlockSpec((1, D_MODEL), lambda i, f: (0, 0)))
        args.append(final_gain)
        out_specs, out_shape = [rows], [x_shape]
    else:
        in_specs += [_gain_spec(l + 1), _mod_spec(l + 1, 0), _mod_spec(l + 1, 1)]
        args += [norm1_g, mods_all, mods_all]
        out_specs = [rows, rows]
        out_shape = [x_shape, jax.ShapeDtypeStruct((t, D_MODEL), BF16)]
    for k, w in enumerate(cast_next):
        n_slabs = grid[0] * grid[1] // CAST_EVERY
        slab, cols = w.shape[1] // n_slabs, w.shape[2]
        assert slab * n_slabs == w.shape[1] and slab % (2 * SUBLANES) == 0
        in_specs.append(pl.BlockSpec((None, slab, cols),
                                     lambda i, f: (l + 1, (i * n_f + f) // CAST_EVERY, 0)))
        args.append(w)
        for out_cols in ((MIX_COLS, GATE_COLS) if k == 0 else (cols,)):
            out_specs.append(pl.BlockSpec((slab, out_cols),
                                          lambda i, f: ((i * n_f + f) // CAST_EVERY, 0)))
            out_shape.append(jax.ShapeDtypeStruct((w.shape[1], out_cols), BF16))
    if emit_weights:
        out_specs += [pl.BlockSpec((D_MODEL, tf), lambda i, f: (0, f)),
                      pl.BlockSpec((tf, D_MODEL), lambda i, f: (f, 0))]
        out_shape += [jax.ShapeDtypeStruct(w1.shape[1:], BF16), jax.ShapeDtypeStruct(w2.shape[1:], BF16)]
    return pl.pallas_call(
        functools.partial(_mlp_kernel, mod_row=mod_row, final=final, n_cast=len(cast_next),
                          emit_weights=emit_weights),
        grid=grid,
        in_specs=in_specs,
        out_specs=out_specs,
        out_shape=out_shape,
        compiler_params=pltpu.CompilerParams(
            dimension_semantics=("arbitrary", "arbitrary"), vmem_limit_bytes=VMEM_LIMIT_BYTES),
        name="mlp",
    )(*args)


def _rope_tables(t):
    half = HEAD_DIM // 2
    nf = half // 2
    tok = np.arange(t)
    inv = ROPE_THETA ** (-np.arange(nf, dtype=np.float64) / nf)
    lane = np.arange(HEAD_DIM)
    pos = np.where(lane[None, :] < half, (tok // GRID_W)[:, None], (tok % GRID_W)[:, None])
    ang = pos * inv[lane % nf][None, :]
    lower = (lane % half) < nf
    sin = np.sin(ang)
    cos = np.cos(ang).astype(np.float32)
    sin_lo = np.where(lower[None, :], -sin, 0.0).astype(np.float32)
    sin_hi = np.where(lower[None, :], 0.0, sin).astype(np.float32)
    return jnp.asarray(cos), jnp.asarray(sin_lo), jnp.asarray(sin_hi)


def _pool_tables(tb, seg_len):
    pos = np.arange(tb) % seg_len
    base = np.arange(tb) - pos
    col = np.arange(tb)[None, :]
    masks, invs = [], []
    for w in POOL_WINDOWS:
        lo = np.maximum(pos - w // 2, 0)
        hi = np.minimum(pos + w // 2 - 1, seg_len - 1)
        masks.append((col >= (base + lo)[:, None]) & (col <= (base + hi)[:, None]))
        invs.append(np.repeat((1.0 / (hi - lo + 1))[:, None], GROUP_DIM, axis=1))
    pmask = jnp.asarray(np.stack(masks).astype(np.float32), dtype=BF16)
    pinv = jnp.asarray(np.concatenate(invs, axis=1).astype(np.float32))
    return pmask, pinv


def _decay_tables(logit):
    lg = jax.nn.log_sigmoid(logit.astype(F32))
    lgf, lgb = lg[0], lg[1]
    idx = jnp.arange(CHUNK, dtype=F32)
    dist = idx[:, None] - idx[None, :]
    fwd = jnp.exp(lgf[:, None, None] * jnp.maximum(dist, 0.0))
    bwd = jnp.exp(lgb[:, None, None] * jnp.maximum(-dist, 0.0))
    dmat = jnp.where(dist > 0, fwd, jnp.where(dist < 0, bwd, 2.0))

    def lanes(tab):
        return jnp.repeat(tab.T, HEAD_DIM, axis=1)

    return dict(
        dmat=dmat,
        qdf=lanes(jnp.exp(lgf[:, None] * (idx + 1.0)[None])),
        qdb=lanes(jnp.exp(lgb[:, None] * (CHUNK - idx)[None])),
        kdf=lanes(jnp.exp(lgf[:, None] * (CHUNK - 1.0 - idx)[None])),
        kdb=lanes(jnp.exp(lgb[:, None] * idx[None])),
        cdf=jnp.broadcast_to(jnp.exp(lgf * CHUNK)[:, None, None], (RET_HEADS, 1, HEAD_DIM)),
        cdb=jnp.broadcast_to(jnp.exp(lgb * CHUNK)[:, None, None], (RET_HEADS, 1, HEAD_DIM)),
    )


LATENT_ROWS_IN = 1024
LATENT_ROWS_MIX = 256
LATENT_ROWS_MLP = 512
LATENT_FF_TILE = 1024
LATENT_ROWS_NORM = 512
LATENT_STATE_CHUNKS = 4
FF_TILE = 1024
CTX_MIX_TILE = MIX_COLS // 4
CTX_GATE_TILE = 512


def kernel(x, c, ctx, c_ctx, w_ada, b_ada, norm1_g, w_in, ret_decay_logit, ret_norm_g, pool_w,
           pool_scale, sg_norm_g, sg_w, sg_b, w_br, w_bp, w_bs, w_out, norm2_g, w1, w2, final_norm_g):
    assert x.shape[0] == 1 and ctx.shape[0] == 1
    t = x.shape[1]
    tc = ctx.shape[1]
    xs = x[0]
    xc = ctx[0]

    cond = jnp.stack([c[0], c_ctx])[:, :, None]
    mods_all = _ada(cond, w_ada, b_ada)

    rope = _rope_tables(t)
    pool_lat = _pool_tables(LATENT_ROWS_MIX, GRID_W)
    pool_ctx = _pool_tables(tc, tc)
    zero_state = jnp.zeros((RET_HEADS, HEAD_DIM, HEAD_DIM), F32)

    tabs = jax.vmap(_decay_tables)(ret_decay_logit)
    tabs["btab"] = jnp.repeat(jnp.swapaxes(sg_b, 1, 2), GROUP_DIM, axis=2)
    vec = lambda v: v.reshape(DEPTH, 1, -1)
    p = dict(ret_norm_g=vec(ret_norm_g), pool_w=pool_w.astype(BF16), pool_scale=vec(pool_scale),
             sg_norm_g=vec(sg_norm_g), sg_w=sg_w.astype(BF16),
             w_br=w_br.astype(BF16), w_bp=w_bp.astype(BF16), w_bs=w_bs.astype(BF16),
             w_out=w_out.astype(BF16), norm2_g=vec(norm2_g))
    n1g = vec(norm1_g)
    fng = final_norm_g.reshape(1, -1)
    streamed = (w_in, w1, w2)
    assert DEPTH > 1

    hc = _norm_mod(xc, 0, n1g, mods_all, ROW_CONTEXT, tc)
    hx = _norm_mod(xs, 0, n1g, mods_all, ROW_LATENT, LATENT_ROWS_NORM)
    for l in range(DEPTH):
        last = l == DEPTH - 1

        if l == 0:
            zc, w_mix_b = _in_proj_cast(hc, w_in, 0, MIX_COLS, CTX_MIX_TILE)
            gc, w_gate_b = _in_proj_cast(hc, w_in, MIX_COLS, GATE_COLS, CTX_GATE_TILE)
        else:
            zc = _in_proj(hc, w_mix_b, True, None, tc)
        sfc, sbc, sf, sb = _states(zc, l, tabs, zero_state, zero_state, tc // CHUNK)
        if not last:
            if l > 0:
                gc = _in_proj(hc, w_gate_b, False, None, tc)
            xc1, hc2 = _mixer(zc, gc, xc, sfc, sbc, l, tabs, pool_ctx, p, mods_all, ROW_CONTEXT, tc)
            if l == 0:
                xc, hc, w1_b, w2_b = _mlp(hc2, xc1, l, w1, w2, mods_all, ROW_CONTEXT, n1g, None,
                                          tc, FF_TILE)
            else:
                xc, hc = _mlp(hc2, xc1, l, w1_b, w2_b, mods_all, ROW_CONTEXT, n1g, None, tc, FF_TILE)

        zx = _in_proj(hx, w_mix_b, True, rope, LATENT_ROWS_IN)
        gx = _in_proj(hx, w_gate_b, False, None, LATENT_ROWS_IN)
        sfx, sbx, _, _ = _states(zx, l, tabs, sf, sb, LATENT_STATE_CHUNKS)
        x1, h2 = _mixer(zx, gx, xs, sfx, sbx, l, tabs, pool_lat, p, mods_all, ROW_LATENT,
                        LATENT_ROWS_MIX)
        if last:
            xs, = _mlp(h2, x1, l, w1_b, w2_b, mods_all, ROW_LATENT, n1g, fng,
                       LATENT_ROWS_MLP, LATENT_FF_TILE)
        else:
            xs, hx, w_mix_b, w_gate_b, w1_b, w2_b = _mlp(
                h2, x1, l, w1_b, w2_b, mods_all, ROW_LATENT, n1g, None,
                LATENT_ROWS_MLP, LATENT_FF_TILE, cast_next=streamed)

    return xs[None]
```

```python
import functools

import numpy as np
import jax
import jax.numpy as jnp
from jax import lax
from jax.experimental import pallas as pl
from jax.experimental.pallas import tpu as pltpu

F32 = jnp.float32
BF16 = jnp.bfloat16

D_MODEL = 2048
DEPTH = 4
GRID_W = 64
EPS = 1e-6
N_MOD = 6

HEAD_DIM = 128
RET_WIDTH = D_MODEL // 2
RET_HEADS = RET_WIDTH // HEAD_DIM
CHUNK = 128
K_SCALE = HEAD_DIM ** -0.5
ROPE_THETA = 10000.0

POOL_WIDTH = D_MODEL // 4
POOL_WINDOWS = (2, 4, 8, 16)
POOL_GROUPS = len(POOL_WINDOWS)
GROUP_DIM = POOL_WIDTH // POOL_GROUPS

SG_WIDTH = D_MODEL // 4
SG_GROUPS = 4

D_FF = 4 * D_MODEL
N_IN = 4 * RET_WIDTH + POOL_WIDTH + 2 * SG_WIDTH + 3 * D_MODEL

SUBLANES = 8
LANES = 128
VMEM_LIMIT_BYTES = 56 * 1024 * 1024

MIX_COLS = 4 * RET_WIDTH + POOL_WIDTH + 2 * SG_WIDTH
GATE_COLS = 3 * D_MODEL
IN_TILES = 2
QK_COLS = 2 * RET_WIDTH
SUB_TILE = 512
_HEADS_PER_SUB = SUB_TILE // HEAD_DIM
MERGE_TILE = 512
N_MERGE = D_MODEL // MERGE_TILE
Q_COL, K_COL, V_COL, G_COL = (k * RET_WIDTH for k in range(4))
POOL_COL = 4 * RET_WIDTH
U_COL = POOL_COL + POOL_WIDTH
SV_COL = U_COL + SG_WIDTH


def _cast_w_in(src, mix_ref, gate_ref):
    mix_ref[...] = src[:, :MIX_COLS].astype(BF16)
    gate_ref[...] = src[:, MIX_COLS:].astype(BF16)

ADA_TILE = 2048
ADA_ROWS = 8
ROW_LATENT, ROW_CONTEXT = 0, 1


def _layer_spec(arr, l):
    nd = arr.ndim - 1
    return pl.BlockSpec((None,) + arr.shape[1:], lambda *_: (l,) + (0,) * nd,
                        pipeline_mode=pl.Buffered(1))


def _const_spec(arr):
    nd = arr.ndim
    return pl.BlockSpec(arr.shape, lambda *_: (0,) * nd, pipeline_mode=pl.Buffered(1))


def _mod_spec(l, k):
    return pl.BlockSpec((None, ADA_ROWS, D_MODEL), lambda *_: (l, 0, k))


def _gain_spec(l):
    return pl.BlockSpec((None, 1, D_MODEL), lambda *_: (l, 0, 0))


def _rms(x):
    return x * lax.rsqrt(jnp.mean(x * x, axis=-1, keepdims=True) + EPS)


def _norm_modulate(x, gain, shift, scale):
    return _rms(x) * gain * (1.0 + scale) + shift


def _ada_kernel(cond_ref, w_ref, b_ref, o_ref, act_scr):
    tn = w_ref.shape[1]

    @pl.when(jnp.logical_and(pl.program_id(0) == 0, pl.program_id(1) == 0))
    def _():
        for r in range(2):
            a = cond_ref[r]
            act_scr[r] = jnp.broadcast_to(a * jax.nn.sigmoid(a), (D_MODEL, LANES))

    def body(g, accs):
        r0 = pl.multiple_of(g * SUBLANES, SUBLANES)
        w = w_ref[pl.ds(r0, SUBLANES), :]
        out = []
        for r in range(2):
            a = jnp.tile(act_scr[r, pl.ds(r0, SUBLANES), :], (1, tn // LANES))
            out.append(accs[r] + w * a)
        return tuple(out)

    zero = jnp.zeros((SUBLANES, tn), F32)
    accs = lax.fori_loop(0, D_MODEL // SUBLANES, body, (zero, zero), unroll=8)
    o_ref[...] = jnp.zeros((ADA_ROWS, tn), F32)
    for r in range(2):
        o_ref[r:r + 1, :] = jnp.sum(accs[r], axis=0, keepdims=True) + b_ref[...]


def _ada(cond, w_ada, b_ada):
    n = N_MOD * D_MODEL
    return pl.pallas_call(
        _ada_kernel,
        grid=(DEPTH, n // ADA_TILE),
        in_specs=[
            pl.BlockSpec((2, D_MODEL, 1), lambda l, j: (0, 0, 0)),
            pl.BlockSpec((None, D_MODEL, ADA_TILE), lambda l, j: (l, 0, j)),
            pl.BlockSpec((None, 1, ADA_TILE), lambda l, j: (l, 0, j)),
        ],
        out_specs=pl.BlockSpec((None, ADA_ROWS, ADA_TILE), lambda l, j: (l, 0, j)),
        out_shape=jax.ShapeDtypeStruct((DEPTH, ADA_ROWS, n), F32),
        scratch_shapes=[pltpu.VMEM((2, D_MODEL, LANES), F32)],
        compiler_params=pltpu.CompilerParams(
            dimension_semantics=("arbitrary", "arbitrary"), vmem_limit_bytes=VMEM_LIMIT_BYTES),
        name="ada",
    )(cond, w_ada, b_ada.reshape(DEPTH, 1, n))


def _norm_kernel(x_ref, g_ref, shift_ref, scale_ref, h_ref, *, mod_row):
    mr = slice(mod_row, mod_row + 1)
    h_ref[...] = _norm_modulate(x_ref[...], g_ref[...], shift_ref[mr, :], scale_ref[mr, :]).astype(BF16)


def _norm_mod(x, l, norm1_g, mods_all, mod_row, tm):
    t = x.shape[0]
    return pl.pallas_call(
        functools.partial(_norm_kernel, mod_row=mod_row),
        grid=(t // tm,),
        in_specs=[pl.BlockSpec((tm, D_MODEL), lambda i: (i, 0)),
                  _gain_spec(l), _mod_spec(l, 0), _mod_spec(l, 1)],
        out_specs=pl.BlockSpec((tm, D_MODEL), lambda i: (i, 0)),
        out_shape=jax.ShapeDtypeStruct((t, D_MODEL), BF16),
        compiler_params=pltpu.CompilerParams(
            dimension_semantics=("arbitrary",), vmem_limit_bytes=VMEM_LIMIT_BYTES),
        name="norm_mod",
    )(x, norm1_g, mods_all, mods_all)


def _rope(t, cos, sin_lo, sin_hi):
    up = pltpu.roll(t, HEAD_DIM - 32, axis=1)
    down = pltpu.roll(t, 32, axis=1)
    return t * cos + up * sin_lo + down * sin_hi


def _inproj_kernel(*refs, mix_part, use_rope):
    if use_rope:
        h_ref, w_ref, cos_ref, slo_ref, shi_ref, z_ref = refs
    else:
        h_ref, w_ref, z_ref = refs
    j = pl.program_id(1)

    if not mix_part:
        z_ref[...] = jnp.dot(h_ref[...], w_ref[...], preferred_element_type=F32).astype(BF16)
        return

    @pl.when(j == 0)
    def _():
        for blk in range(QK_COLS // SUB_TILE):
            c0 = blk * SUB_TILE
            acc = jnp.dot(h_ref[...], w_ref[:, c0:c0 + SUB_TILE], preferred_element_type=F32)
            if c0 >= RET_WIDTH:
                acc = acc * K_SCALE
            if use_rope:
                cos, slo, shi = cos_ref[...], slo_ref[...], shi_ref[...]
                for hh in range(_HEADS_PER_SUB):
                    hs = slice(hh * HEAD_DIM, (hh + 1) * HEAD_DIM)
                    z_ref[:, c0 + hh * HEAD_DIM:c0 + (hh + 1) * HEAD_DIM] = (
                        _rope(acc[:, hs], cos, slo, shi).astype(BF16))
            else:
                z_ref[:, c0:c0 + SUB_TILE] = acc.astype(BF16)
        z_ref[:, QK_COLS:] = jnp.dot(h_ref[...], w_ref[:, QK_COLS:],
                                     preferred_element_type=F32).astype(BF16)

    @pl.when(j != 0)
    def _():
        z_ref[...] = jnp.dot(h_ref[...], w_ref[...], preferred_element_type=F32).astype(BF16)


def _in_proj(h, w_part, mix_part, rope, tm):
    t = h.shape[0]
    width = w_part.shape[1]
    tile = width // IN_TILES
    assert tile * IN_TILES == width and tile % LANES == 0 and (not mix_part or tile >= QK_COLS)
    use_rope = mix_part and rope is not None
    in_specs = [
        pl.BlockSpec((tm, D_MODEL), lambda i, j: (i, 0)),
        pl.BlockSpec((D_MODEL, tile), lambda i, j: (0, j)),
    ]
    args = [h, w_part]
    if use_rope:
        in_specs += [pl.BlockSpec((tm, HEAD_DIM), lambda i, j: (i, 0))] * 3
        args += list(rope)
    return pl.pallas_call(
        functools.partial(_inproj_kernel, mix_part=mix_part, use_rope=use_rope),
        grid=(t // tm, IN_TILES),
        in_specs=in_specs,
        out_specs=pl.BlockSpec((tm, tile), lambda i, j: (i, j)),
        out_shape=jax.ShapeDtypeStruct((t, width), BF16),
        compiler_params=pltpu.CompilerParams(
            dimension_semantics=("arbitrary", "arbitrary"), vmem_limit_bytes=VMEM_LIMIT_BYTES),
        name="in_proj_mix" if mix_part else "in_proj_gate",
    )(*args)


def _inproj_cast_kernel(h_ref, w_ref, z_ref, wb_ref, *, col0):
    tile = w_ref.shape[1]
    wb = w_ref[...].astype(BF16)
    wb_ref[...] = wb
    acc = jnp.dot(h_ref[...], wb, preferred_element_type=F32)
    col = col0 + pl.program_id(0) * tile + lax.broadcasted_iota(jnp.int32, (1, tile), 1)
    is_k = jnp.logical_and(col >= K_COL, col < K_COL + RET_WIDTH)
    z_ref[...] = (acc * jnp.where(is_k, K_SCALE, 1.0)).astype(BF16)


def _in_proj_cast(h, w_in, col0, width, tile):
    t = h.shape[0]
    assert width % tile == 0 and col0 % tile == 0 and tile % LANES == 0
    return pl.pallas_call(
        functools.partial(_inproj_cast_kernel, col0=col0),
        grid=(width // tile,),
        in_specs=[pl.BlockSpec((t, D_MODEL), lambda j: (0, 0)),
                  pl.BlockSpec((None, D_MODEL, tile), lambda j: (0, 0, col0 // tile + j))],
        out_specs=[pl.BlockSpec((t, tile), lambda j: (0, j)),
                   pl.BlockSpec((D_MODEL, tile), lambda j: (0, j))],
        out_shape=[jax.ShapeDtypeStruct((t, width), BF16),
                   jax.ShapeDtypeStruct((D_MODEL, width), BF16)],
        compiler_params=pltpu.CompilerParams(
            dimension_semantics=("arbitrary",), vmem_limit_bytes=VMEM_LIMIT_BYTES),
        name="in_proj_cast",
    )(h, w_in)


def _state_kernel(k_ref, v_ref, kdf_ref, kdb_ref, cdf_ref, cdb_ref, sf0_ref, sb0_ref,
                  sf_all_ref, sb_all_ref, sf_fin_ref, sb_fin_ref, sf_scr, sb_scr):
    t = pl.program_id(0)
    nb = pl.num_programs(0)
    cps = sf_all_ref.shape[0]

    @pl.when(t == 0)
    def _():
        sf_scr[...] = sf0_ref[...]
        sb_scr[...] = sb0_ref[...]

    contract_rows = (((0,), (0,)), ((), ()))

    def advance(block, c, kd_ref, cd_ref, all_ref, scr):
        rows = pl.ds(pl.multiple_of((block * cps + c) * CHUNK, CHUNK), CHUNK)
        all_ref[c] = scr[...].astype(BF16)
        for h in range(RET_HEADS):
            hs = slice(h * HEAD_DIM, (h + 1) * HEAD_DIM)
            kd = (k_ref[rows, hs].astype(F32) * kd_ref[:, hs]).astype(BF16)
            kv = lax.dot_general(kd, v_ref[rows, hs], contract_rows, preferred_element_type=F32)
            scr[h] = scr[h] * cd_ref[h] + kv

    for c in range(cps):
        advance(t, c, kdf_ref, cdf_ref, sf_all_ref, sf_scr)
        advance(nb - 1 - t, cps - 1 - c, kdb_ref, cdb_ref, sb_all_ref, sb_scr)

    @pl.when(t == pl.num_programs(0) - 1)
    def _():
        sf_fin_ref[...] = sf_scr[...]
        sb_fin_ref[...] = sb_scr[...]


def _states(z, l, tabs, sf0, sb0, cps):
    t = z.shape[0]
    nb = t // (cps * CHUNK)
    whole = lambda c: pl.BlockSpec((t, RET_WIDTH), lambda i: (0, c), pipeline_mode=pl.Buffered(1))
    state_shape = (RET_HEADS, HEAD_DIM, HEAD_DIM)
    consts = [tabs["kdf"], tabs["kdb"], tabs["cdf"], tabs["cdb"]]
    kb, vb = K_COL // RET_WIDTH, V_COL // RET_WIDTH
    in_specs = [whole(kb), whole(vb)] + [_layer_spec(a, l) for a in consts]
    in_specs += [_const_spec(sf0), _const_spec(sb0)]
    all_shape = jax.ShapeDtypeStruct((t // CHUNK,) + state_shape, BF16)
    fin_shape = jax.ShapeDtypeStruct(state_shape, F32)
    return pl.pallas_call(
        _state_kernel,
        grid=(nb,),
        in_specs=in_specs,
        out_specs=[
            pl.BlockSpec((cps,) + state_shape, lambda i: (i, 0, 0, 0)),
            pl.BlockSpec((cps,) + state_shape, lambda i: (nb - 1 - i, 0, 0, 0)),
            pl.BlockSpec(state_shape, lambda i: (0, 0, 0)),
            pl.BlockSpec(state_shape, lambda i: (0, 0, 0)),
        ],
        out_shape=[all_shape, all_shape, fin_shape, fin_shape],
        scratch_shapes=[pltpu.VMEM(state_shape, F32), pltpu.VMEM(state_shape, F32)],
        compiler_params=pltpu.CompilerParams(
            dimension_semantics=("arbitrary",), vmem_limit_bytes=VMEM_LIMIT_BYTES),
        name="ret_states",
    )(z, z, *consts, sf0, sb0)


def _mixer_kernel(*refs, mod_row):
    (zmix_ref, sf_ref, sb_ref, zgate_ref, x_ref,
     dmat_ref, qdf_ref, qdb_ref, retg_ref, pmask_ref, pinv_ref, poolw_ref, pscale_ref,
     sgng_ref, sgw_ref, btab_ref, wbr_ref, wbp_ref, wbs_ref, wout_ref,
     gate_ref, n2g_ref, shift2_ref, scale2_ref,
     x1_ref, h2_ref,
     y_scr, ret_cur, pool_cur, sg_cur, ret_prev, pool_prev, sg_prev) = refs
    tb = x_ref.shape[0]
    nck = tb // CHUNK
    contract_last = (((1,), (1,)), ((), ()))
    mr = slice(mod_row, mod_row + 1)

    def merge_piece(i):
        cs = slice(i * MERGE_TILE, (i + 1) * MERGE_TILE)
        def gate(branch):
            c0 = branch * D_MODEL + i * MERGE_TILE
            return jax.nn.sigmoid(zgate_ref[:, c0:c0 + MERGE_TILE].astype(F32))

        y = gate(0) * jnp.dot(ret_prev[...], wbr_ref[:, cs], preferred_element_type=F32)
        y = y + gate(1) * jnp.dot(pool_prev[...], wbp_ref[:, cs], preferred_element_type=F32)
        y = y + gate(2) * jnp.dot(sg_prev[...], wbs_ref[:, cs], preferred_element_type=F32)
        y_scr[:, cs] = y.astype(BF16)

    def out_piece(i):
        cs = slice(i * MERGE_TILE, (i + 1) * MERGE_TILE)
        out = jnp.dot(y_scr[...], wout_ref[:, cs], preferred_element_type=F32)
        x1_ref[:, cs] = x_ref[:, cs] + gate_ref[mr, cs] * out

    def ret_unit(c, h):
        rows = slice(c * CHUNK, (c + 1) * CHUNK)
        hs = slice(h * HEAD_DIM, (h + 1) * HEAD_DIM)
        head = lambda col: zmix_ref[rows, col + h * HEAD_DIM:col + (h + 1) * HEAD_DIM]
        qb = head(Q_COL)
        scores = lax.dot_general(qb, head(K_COL), contract_last, preferred_element_type=F32)
        scores = (scores * dmat_ref[h]).astype(BF16)
        cross = qdf_ref[:, hs] * jnp.dot(qb, sf_ref[c, h], preferred_element_type=F32)
        cross = cross + qdb_ref[:, hs] * jnp.dot(qb, sb_ref[c, h], preferred_element_type=F32)

        def second():
            o = jnp.dot(scores, head(V_COL), preferred_element_type=F32) + cross
            g = head(G_COL).astype(F32)
            ret = _rms(o) * retg_ref[:, hs] * (g * jax.nn.sigmoid(g))
            ret_cur[rows, hs] = ret.astype(BF16)

        return second

    def pool_unit(gi):
        gs = slice(gi * GROUP_DIM, (gi + 1) * GROUP_DIM)
        pb = zmix_ref[:, POOL_COL + gi * GROUP_DIM:POOL_COL + (gi + 1) * GROUP_DIM]
        win = jnp.dot(pmask_ref[gi], pb, preferred_element_type=F32)
        pooled = (win * pinv_ref[:, gs] - pb.astype(F32)).astype(BF16)

        def second():
            mapped = jnp.dot(pooled, poolw_ref[gi], preferred_element_type=F32)
            pool_cur[:, gs] = (mapped * pscale_ref[:, gs]).astype(BF16)

        return second

    def sg_unit(c):
        rows = slice(c * CHUNK, (c + 1) * CHUNK)
        sv = zmix_ref[rows, SV_COL:SV_COL + SG_WIDTH].astype(F32)
        sv = (_rms(jax.nn.gelu(sv)) * sgng_ref[...]).astype(BF16)

        def second():
            for gi in range(SG_GROUPS):
                gs = slice(gi * GROUP_DIM, (gi + 1) * GROUP_DIM)
                mixed = jnp.dot(sgw_ref[gi], sv[:, gs], preferred_element_type=F32)
                mixed = mixed + btab_ref[:, gs]
                u = zmix_ref[rows, U_COL + gi * GROUP_DIM:U_COL + (gi + 1) * GROUP_DIM]
                u = jax.nn.gelu(u.astype(F32))
                sg_cur[rows, gs] = (u * mixed).astype(BF16)

        return second

    mix_units = [functools.partial(ret_unit, c, h) for c in range(nck) for h in range(RET_HEADS)]
    mix_units += [functools.partial(pool_unit, gi) for gi in range(POOL_GROUPS)]
    mix_units += [functools.partial(sg_unit, c) for c in range(nck)]
    pieces = ([functools.partial(merge_piece, i) for i in range(N_MERGE)]
              + [functools.partial(out_piece, i) for i in range(N_MERGE)])

    def run(mix, merge):
        units = mix_units if mix else []
        per_piece = -(-len(units) // len(pieces))
        pending = [unit() for unit in units[:per_piece]]
        for i, piece in enumerate(pieces):
            if merge:
                piece()
            for second in pending:
                second()
            pending = [unit() for unit in units[(i + 1) * per_piece:(i + 2) * per_piece]]
        assert not pending
        if merge:
            gain2 = n2g_ref[...] * (1.0 + scale2_ref[mr, :])
            h2_ref[...] = (_rms(x1_ref[...]) * gain2 + shift2_ref[mr, :]).astype(BF16)
        if mix:
            ret_prev[...] = ret_cur[...]
            pool_prev[...] = pool_cur[...]
            sg_prev[...] = sg_cur[...]

    s = pl.program_id(0)
    last = pl.num_programs(0) - 1
    pl.when(s == 0)(functools.partial(run, True, False))
    pl.when(jnp.logical_and(s > 0, s < last))(functools.partial(run, True, True))
    pl.when(s == last)(functools.partial(run, False, True))


def _mixer(z_mix, z_gate, x, sf_all, sb_all, l, tabs, pool_tabs, p, mods_all, mod_row, tb):
    t = x.shape[0]
    nb = t // tb
    nck = tb // CHUNK
    cur = lambda s: jnp.minimum(s, nb - 1)
    prev = lambda s: jnp.maximum(s - 1, 0)

    state_spec = pl.BlockSpec((nck, RET_HEADS, HEAD_DIM, HEAD_DIM), lambda s: (cur(s), 0, 0, 0))
    row_spec = pl.BlockSpec((tb, D_MODEL), lambda s: (prev(s), 0))
    in_specs = [pl.BlockSpec((tb, MIX_COLS), lambda s: (cur(s), 0)), state_spec, state_spec,
                pl.BlockSpec((tb, GATE_COLS), lambda s: (prev(s), 0)), row_spec]
    args = [z_mix, sf_all, sb_all, z_gate, x]
    pmask, pinv = pool_tabs
    layer_consts = [tabs["dmat"], tabs["qdf"], tabs["qdb"], p["ret_norm_g"]]
    in_specs += [_layer_spec(a, l) for a in layer_consts] + [_const_spec(pmask), _const_spec(pinv)]
    args += layer_consts + [pmask, pinv]
    layer_consts = [p["pool_w"], p["pool_scale"], p["sg_norm_g"], p["sg_w"], tabs["btab"],
                    p["w_br"], p["w_bp"], p["w_bs"], p["w_out"]]
    in_specs += [_layer_spec(a, l) for a in layer_consts]
    args += layer_consts
    in_specs += [_mod_spec(l, 2), _gain_spec(l), _mod_spec(l, 3), _mod_spec(l, 4)]
    args += [mods_all, p["norm2_g"], mods_all, mods_all]
    mix_scratch = [pltpu.VMEM((tb, RET_WIDTH), BF16), pltpu.VMEM((tb, POOL_WIDTH), BF16),
                   pltpu.VMEM((tb, SG_WIDTH), BF16)]
    return pl.pallas_call(
        functools.partial(_mixer_kernel, mod_row=mod_row),
        grid=(nb + 1,),
        in_specs=in_specs,
        out_specs=[row_spec, row_spec],
        out_shape=[jax.ShapeDtypeStruct((t, D_MODEL), F32),
                   jax.ShapeDtypeStruct((t, D_MODEL), BF16)],
        scratch_shapes=[pltpu.VMEM((tb, D_MODEL), BF16)] + mix_scratch + mix_scratch,
        compiler_params=pltpu.CompilerParams(
            dimension_semantics=("arbitrary",), vmem_limit_bytes=VMEM_LIMIT_BYTES),
        name="mixer",
    )(*args)


EPILOGUE_ROWS = 16
CAST_EVERY = 2


def _mlp_kernel(*refs, mod_row, final, n_cast, emit_weights):
    refs = list(refs)
    w1b_ref = w2b_ref = None
    if emit_weights:
        w1b_ref, w2b_ref = refs[-2:]
        refs = refs[:-2]
    cast_src = cast_dst = ()
    if n_cast:
        n_dst = n_cast + 1
        cast_dst = refs[-n_dst:]
        refs = refs[:-n_dst]
        n_out = 1 if final else 2
        cast_src = refs[-n_out - n_cast:-n_out]
        refs = refs[:-n_out - n_cast] + refs[-n_out:]
    if final:
        h2_ref, x1_ref, w1_ref, w2_ref, gate_ref, fng_ref, o_ref = refs
    else:
        (h2_ref, x1_ref, w1_ref, w2_ref, gate_ref, ng_ref, nshift_ref, nscale_ref,
         o_ref, hn_ref) = refs
    f = pl.program_id(1)
    mr = slice(mod_row, mod_row + 1)

    def step(first):
        w1, w2 = w1_ref[...], w2_ref[...]
        if emit_weights:
            w1, w2 = w1.astype(BF16), w2.astype(BF16)
            w1b_ref[...] = w1
            w2b_ref[...] = w2
        a = jnp.maximum(jnp.dot(h2_ref[...], w1, preferred_element_type=F32), 0.0)
        part = jnp.dot((a * a).astype(BF16), w2, preferred_element_type=F32)
        if first:
            o_ref[...] = part
        else:
            o_ref[...] += part
        part_id = (pl.program_id(0) * pl.num_programs(1) + f) % CAST_EVERY
        for k, src in enumerate(cast_src):
            share = src.shape[0] // CAST_EVERY
            rows = pl.ds(pl.multiple_of(part_id * share, share), share)
            if k == 0:
                _cast_w_in(src[rows, :], cast_dst[0].at[rows], cast_dst[1].at[rows])
            else:
                cast_dst[k + 1][rows, :] = src[rows, :].astype(BF16)

    @pl.when(f == 0)
    def _():
        step(True)

    @pl.when(f > 0)
    def _():
        step(False)

    @pl.when(f == pl.num_programs(1) - 1)
    def _():
        gate = gate_ref[mr, :]
        if final:
            gain = fng_ref[...]
        else:
            gain = ng_ref[...] * (1.0 + nscale_ref[mr, :])
            shift = nshift_ref[mr, :]

        for r in range(o_ref.shape[0] // EPILOGUE_ROWS):
            rows = slice(r * EPILOGUE_ROWS, (r + 1) * EPILOGUE_ROWS)
            x2 = x1_ref[rows, :] + gate * o_ref[rows, :]
            if final:
                o_ref[rows, :] = _rms(x2) * gain
            else:
                o_ref[rows, :] = x2
                hn_ref[rows, :] = (_rms(x2) * gain + shift).astype(BF16)


def _mlp(h2, x1, l, w1, w2, mods_all, mod_row, norm1_g, final_gain, tm, tf, cast_next=()):
    t = x1.shape[0]
    final = final_gain is not None
    n_f = D_FF // tf
    grid = (t // tm, n_f)
    rows = pl.BlockSpec((tm, D_MODEL), lambda i, f: (i, 0))
    emit_weights = w1.ndim == 3
    assert not emit_weights or (grid[0] == 1 and not cast_next)
    lead = (None,) if emit_weights else ()
    at = (lambda *idx: (l,) + idx) if emit_weights else (lambda *idx: idx)
    in_specs = [rows, rows,
                pl.BlockSpec(lead + (D_MODEL, tf), lambda i, f: at(0, f)),
                pl.BlockSpec(lead + (tf, D_MODEL), lambda i, f: at(f, 0)),
                _mod_spec(l, 5)]
    args = [h2, x1, w1, w2, mods_all]
    x_shape = jax.ShapeDtypeStruct((t, D_MODEL), F32)
    if final:
        in_specs.append(pl.BlockSpec((1, D_MODEL), lambda i, f: (0, 0)))
        args.append(final_gain)
        out_specs, out_shape = [rows], [x_shape]
    else:
        in_specs += [_gain_spec(l + 1), _mod_spec(l + 1, 0), _mod_spec(l + 1, 1)]
        args += [norm1_g, mods_all, mods_all]
        out_specs = [rows, rows]
        out_shape = [x_shape, jax.ShapeDtypeStruct((t, D_MODEL), BF16)]
    for k, w in enumerate(cast_next):
        n_slabs = grid[0] * grid[1] // CAST_EVERY
        slab, cols = w.shape[1] // n_slabs, w.shape[2]
        assert slab * n_slabs == w.shape[1] and slab % (2 * SUBLANES) == 0
        in_specs.append(pl.BlockSpec((None, slab, cols),
                                     lambda i, f: (l + 1, (i * n_f + f) // CAST_EVERY, 0)))
        args.append(w)
        for out_cols in ((MIX_COLS, GATE_COLS) if k == 0 else (cols,)):
            out_specs.append(pl.BlockSpec((slab, out_cols),
                                          lambda i, f: ((i * n_f + f) // CAST_EVERY, 0)))
            out_shape.append(jax.ShapeDtypeStruct((w.shape[1], out_cols), BF16))
    if emit_weights:
        out_specs += [pl.BlockSpec((D_MODEL, tf), lambda i, f: (0, f)),
                      pl.BlockSpec((tf, D_MODEL), lambda i, f: (f, 0))]
        out_shape += [jax.ShapeDtypeStruct(w1.shape[1:], BF16), jax.ShapeDtypeStruct(w2.shape[1:], BF16)]
    return pl.pallas_call(
        functools.partial(_mlp_kernel, mod_row=mod_row, final=final, n_cast=len(cast_next),
                          emit_weights=emit_weights),
        grid=grid,
        in_specs=in_specs,
        out_specs=out_specs,
        out_shape=out_shape,
        compiler_params=pltpu.CompilerParams(
            dimension_semantics=("arbitrary", "arbitrary"), vmem_limit_bytes=VMEM_LIMIT_BYTES),
        name="mlp",
    )(*args)


def _rope_tables(t):
    half = HEAD_DIM // 2
    nf = half // 2
    tok = np.arange(t)
    inv = ROPE_THETA ** (-np.arange(nf, dtype=np.float64) / nf)
    lane = np.arange(HEAD_DIM)
    pos = np.where(lane[None, :] < half, (tok // GRID_W)[:, None], (tok % GRID_W)[:, None])
    ang = pos * inv[lane % nf][None, :]
    lower = (lane % half) < nf
    sin = np.sin(ang)
    cos = np.cos(ang).astype(np.float32)
    sin_lo = np.where(lower[None, :], -sin, 0.0).astype(np.float32)
    sin_hi = np.where(lower[None, :], 0.0, sin).astype(np.float32)
    return jnp.asarray(cos), jnp.asarray(sin_lo), jnp.asarray(sin_hi)


def _pool_tables(tb, seg_len):
    pos = np.arange(tb) % seg_len
    base = np.arange(tb) - pos
    col = np.arange(tb)[None, :]
    masks, invs = [], []
    for w in POOL_WINDOWS:
        lo = np.maximum(pos - w // 2, 0)
        hi = np.minimum(pos + w // 2 - 1, seg_len - 1)
        masks.append((col >= (base + lo)[:, None]) & (col <= (base + hi)[:, None]))
        invs.append(np.repeat((1.0 / (hi - lo + 1))[:, None], GROUP_DIM, axis=1))
    pmask = jnp.asarray(np.stack(masks).astype(np.float32), dtype=BF16)
    pinv = jnp.asarray(np.concatenate(invs, axis=1).astype(np.float32))
    return pmask, pinv


def _decay_tables(logit):
    lg = jax.nn.log_sigmoid(logit.astype(F32))
    lgf, lgb = lg[0], lg[1]
    idx = jnp.arange(CHUNK, dtype=F32)
    dist = idx[:, None] - idx[None, :]
    fwd = jnp.exp(lgf[:, None, None] * jnp.maximum(dist, 0.0))
    bwd = jnp.exp(lgb[:, None, None] * jnp.maximum(-dist, 0.0))
    dmat = jnp.where(dist > 0, fwd, jnp.where(dist < 0, bwd, 2.0))

    def lanes(tab):
        return jnp.repeat(tab.T, HEAD_DIM, axis=1)

    return dict(
        dmat=dmat,
        qdf=lanes(jnp.exp(lgf[:, None] * (idx + 1.0)[None])),
        qdb=lanes(jnp.exp(lgb[:, None] * (CHUNK - idx)[None])),
        kdf=lanes(jnp.exp(lgf[:, None] * (CHUNK - 1.0 - idx)[None])),
        kdb=lanes(jnp.exp(lgb[:, None] * idx[None])),
        cdf=jnp.broadcast_to(jnp.exp(lgf * CHUNK)[:, None, None], (RET_HEADS, 1, HEAD_DIM)),
        cdb=jnp.broadcast_to(jnp.exp(lgb * CHUNK)[:, None, None], (RET_HEADS, 1, HEAD_DIM)),
    )


LATENT_ROWS_IN = 1024
LATENT_ROWS_MIX = 256
LATENT_ROWS_MLP = 512
LATENT_FF_TILE = 1024
LATENT_ROWS_NORM = 512
LATENT_STATE_CHUNKS = 8
FF_TILE = 1024
CTX_MIX_TILE = MIX_COLS // 4
CTX_GATE_TILE = 512


def kernel(x, c, ctx, c_ctx, w_ada, b_ada, norm1_g, w_in, ret_decay_logit, ret_norm_g, pool_w,
           pool_scale, sg_norm_g, sg_w, sg_b, w_br, w_bp, w_bs, w_out, norm2_g, w1, w2, final_norm_g):
    assert x.shape[0] == 1 and ctx.shape[0] == 1
    t = x.shape[1]
    tc = ctx.shape[1]
    xs = x[0]
    xc = ctx[0]

    cond = jnp.stack([c[0], c_ctx])[:, :, None]
    mods_all = _ada(cond, w_ada, b_ada)

    rope = _rope_tables(t)
    pool_lat = _pool_tables(LATENT_ROWS_MIX, GRID_W)
    pool_ctx = _pool_tables(tc, tc)
    zero_state = jnp.zeros((RET_HEADS, HEAD_DIM, HEAD_DIM), F32)

    tabs = jax.vmap(_decay_tables)(ret_decay_logit)
    tabs["btab"] = jnp.repeat(jnp.swapaxes(sg_b, 1, 2), GROUP_DIM, axis=2)
    vec = lambda v: v.reshape(DEPTH, 1, -1)
    p = dict(ret_norm_g=vec(ret_norm_g), pool_w=pool_w.astype(BF16), pool_scale=vec(pool_scale),
             sg_norm_g=vec(sg_norm_g), sg_w=sg_w.astype(BF16),
             w_br=w_br.astype(BF16), w_bp=w_bp.astype(BF16), w_bs=w_bs.astype(BF16),
             w_out=w_out.astype(BF16), norm2_g=vec(norm2_g))
    n1g = vec(norm1_g)
    fng = final_norm_g.reshape(1, -1)
    streamed = (w_in, w1, w2)
    assert DEPTH > 1

    hc = _norm_mod(xc, 0, n1g, mods_all, ROW_CONTEXT, tc)
    hx = _norm_mod(xs, 0, n1g, mods_all, ROW_LATENT, LATENT_ROWS_NORM)
    for l in range(DEPTH):
        last = l == DEPTH - 1

        if l == 0:
            zc, w_mix_b = _in_proj_cast(hc, w_in, 0, MIX_COLS, CTX_MIX_TILE)
            gc, w_gate_b = _in_proj_cast(hc, w_in, MIX_COLS, GATE_COLS, CTX_GATE_TILE)
        else:
            zc = _in_proj(hc, w_mix_b, True, None, tc)
        sfc, sbc, sf, sb = _states(zc, l, tabs, zero_state, zero_state, tc // CHUNK)
        if not last:
            if l > 0:
                gc = _in_proj(hc, w_gate_b, False, None, tc)
            xc1, hc2 = _mixer(zc, gc, xc, sfc, sbc, l, tabs, pool_ctx, p, mods_all, ROW_CONTEXT, tc)
            if l == 0:
                xc, hc, w1_b, w2_b = _mlp(hc2, xc1, l, w1, w2, mods_all, ROW_CONTEXT, n1g, None,
                                          tc, FF_TILE)
            else:
                xc, hc = _mlp(hc2, xc1, l, w1_b, w2_b, mods_all, ROW_CONTEXT, n1g, None, tc, FF_TILE)

        zx = _in_proj(hx, w_mix_b, True, rope, LATENT_ROWS_IN)
        gx = _in_proj(hx, w_gate_b, False, None, LATENT_ROWS_IN)
        sfx, sbx, _, _ = _states(zx, l, tabs, sf, sb, LATENT_STATE_CHUNKS)
        x1, h2 = _mixer(zx, gx, xs, sfx, sbx, l, tabs, pool_lat, p, mods_all, ROW_LATENT,
                        LATENT_ROWS_MIX)
        if last:
            xs, = _mlp(h2, x1, l, w1_b, w2_b, mods_all, ROW_LATENT, n1g, fng,
                       LATENT_ROWS_MLP, LATENT_FF_TILE)
        else:
            xs, hx, w_mix_b, w_gate_b, w1_b, w2_b = _mlp(
                h2, x1, l, w1_b, w2_b, mods_all, ROW_LATENT, n1g, None,
                LATENT_ROWS_MLP, LATENT_FF_TILE, cast_next=streamed)

    return xs[None]
```

```python
import functools

import numpy as np
import jax
import jax.numpy as jnp
from jax import lax
from jax.experimental import pallas as pl
from jax.experimental.pallas import tpu as pltpu

F32 = jnp.float32
BF16 = jnp.bfloat16

D_MODEL = 2048
DEPTH = 4
GRID_W = 64
EPS = 1e-6
N_MOD = 6

HEAD_DIM = 128
RET_WIDTH = D_MODEL // 2
RET_HEADS = RET_WIDTH // HEAD_DIM
CHUNK = 128
K_SCALE = HEAD_DIM ** -0.5
ROPE_THETA = 10000.0

POOL_WIDTH = D_MODEL // 4
POOL_WINDOWS = (2, 4, 8, 16)
POOL_GROUPS = len(POOL_WINDOWS)
GROUP_DIM = POOL_WIDTH // POOL_GROUPS

SG_WIDTH = D_MODEL // 4
SG_GROUPS = 4

D_FF = 4 * D_MODEL
N_IN = 4 * RET_WIDTH + POOL_WIDTH + 2 * SG_WIDTH + 3 * D_MODEL

SUBLANES = 8
LANES = 128
VMEM_LIMIT_BYTES = 56 * 1024 * 1024

MIX_COLS = 4 * RET_WIDTH + POOL_WIDTH + 2 * SG_WIDTH
GATE_COLS = 3 * D_MODEL
IN_TILES = 2
QK_COLS = 2 * RET_WIDTH
SUB_TILE = 512
_HEADS_PER_SUB = SUB_TILE // HEAD_DIM
MERGE_TILE = 512
N_MERGE = D_MODEL // MERGE_TILE
Q_COL, K_COL, V_COL, G_COL = (k * RET_WIDTH for k in range(4))
POOL_COL = 4 * RET_WIDTH
U_COL = POOL_COL + POOL_WIDTH
SV_COL = U_COL + SG_WIDTH


def _cast_w_in(src, mix_ref, gate_ref):
    mix_ref[...] = src[:, :MIX_COLS].astype(BF16)
    gate_ref[...] = src[:, MIX_COLS:].astype(BF16)

ADA_TILE = 2048
ADA_ROWS = 8
ROW_LATENT, ROW_CONTEXT = 0, 1


def _layer_spec(arr, l):
    nd = arr.ndim - 1
    return pl.BlockSpec((None,) + arr.shape[1:], lambda *_: (l,) + (0,) * nd,
                        pipeline_mode=pl.Buffered(1))


def _const_spec(arr):
    nd = arr.ndim
    return pl.BlockSpec(arr.shape, lambda *_: (0,) * nd, pipeline_mode=pl.Buffered(1))


def _mod_spec(l, k):
    return pl.BlockSpec((None, ADA_ROWS, D_MODEL), lambda *_: (l, 0, k))


def _gain_spec(l):
    return pl.BlockSpec((None, 1, D_MODEL), lambda *_: (l, 0, 0))


def _rms(x):
    return x * lax.rsqrt(jnp.mean(x * x, axis=-1, keepdims=True) + EPS)


def _norm_modulate(x, gain, shift, scale):
    return _rms(x) * gain * (1.0 + scale) + shift


def _ada_kernel(cond_ref, w_ref, b_ref, o_ref, act_scr):
    tn = w_ref.shape[1]

    @pl.when(jnp.logical_and(pl.program_id(0) == 0, pl.program_id(1) == 0))
    def _():
        for r in range(2):
            a = cond_ref[r]
            act_scr[r] = jnp.broadcast_to(a * jax.nn.sigmoid(a), (D_MODEL, LANES))

    def body(g, accs):
        r0 = pl.multiple_of(g * SUBLANES, SUBLANES)
        w = w_ref[pl.ds(r0, SUBLANES), :]
        out = []
        for r in range(2):
            a = jnp.tile(act_scr[r, pl.ds(r0, SUBLANES), :], (1, tn // LANES))
            out.append(accs[r] + w * a)
        return tuple(out)

    zero = jnp.zeros((SUBLANES, tn), F32)
    accs = lax.fori_loop(0, D_MODEL // SUBLANES, body, (zero, zero), unroll=8)
    o_ref[...] = jnp.zeros((ADA_ROWS, tn), F32)
    for r in range(2):
        o_ref[r:r + 1, :] = jnp.sum(accs[r], axis=0, keepdims=True) + b_ref[...]


def _ada(cond, w_ada, b_ada):
    n = N_MOD * D_MODEL
    return pl.pallas_call(
        _ada_kernel,
        grid=(DEPTH, n // ADA_TILE),
        in_specs=[
            pl.BlockSpec((2, D_MODEL, 1), lambda l, j: (0, 0, 0)),
            pl.BlockSpec((None, D_MODEL, ADA_TILE), lambda l, j: (l, 0, j)),
            pl.BlockSpec((None, 1, ADA_TILE), lambda l, j: (l, 0, j)),
        ],
        out_specs=pl.BlockSpec((None, ADA_ROWS, ADA_TILE), lambda l, j: (l, 0, j)),
        out_shape=jax.ShapeDtypeStruct((DEPTH, ADA_ROWS, n), F32),
        scratch_shapes=[pltpu.VMEM((2, D_MODEL, LANES), F32)],
        compiler_params=pltpu.CompilerParams(
            dimension_semantics=("arbitrary", "arbitrary"), vmem_limit_bytes=VMEM_LIMIT_BYTES),
        name="ada",
    )(cond, w_ada, b_ada.reshape(DEPTH, 1, n))


def _norm_kernel(x_ref, g_ref, shift_ref, scale_ref, h_ref, *, mod_row):
    mr = slice(mod_row, mod_row + 1)
    h_ref[...] = _norm_modulate(x_ref[...], g_ref[...], shift_ref[mr, :], scale_ref[mr, :]).astype(BF16)


def _norm_mod(x, l, norm1_g, mods_all, mod_row, tm):
    t = x.shape[0]
    return pl.pallas_call(
        functools.partial(_norm_kernel, mod_row=mod_row),
        grid=(t // tm,),
        in_specs=[pl.BlockSpec((tm, D_MODEL), lambda i: (i, 0)),
                  _gain_spec(l), _mod_spec(l, 0), _mod_spec(l, 1)],
        out_specs=pl.BlockSpec((tm, D_MODEL), lambda i: (i, 0)),
        out_shape=jax.ShapeDtypeStruct((t, D_MODEL), BF16),
        compiler_params=pltpu.CompilerParams(
            dimension_semantics=("arbitrary",), vmem_limit_bytes=VMEM_LIMIT_BYTES),
        name="norm_mod",
    )(x, norm1_g, mods_all, mods_all)


def _rope(t, cos, sin_lo, sin_hi):
    up = pltpu.roll(t, HEAD_DIM - 32, axis=1)
    down = pltpu.roll(t, 32, axis=1)
    return t * cos + up * sin_lo + down * sin_hi


def _inproj_kernel(*refs, mix_part, use_rope):
    if use_rope:
        h_ref, w_ref, cos_ref, slo_ref, shi_ref, z_ref = refs
    else:
        h_ref, w_ref, z_ref = refs
    j = pl.program_id(1)

    if not mix_part:
        z_ref[...] = jnp.dot(h_ref[...], w_ref[...], preferred_element_type=F32).astype(BF16)
        return

    @pl.when(j == 0)
    def _():
        for blk in range(QK_COLS // SUB_TILE):
            c0 = blk * SUB_TILE
            acc = jnp.dot(h_ref[...], w_ref[:, c0:c0 + SUB_TILE], preferred_element_type=F32)
            if c0 >= RET_WIDTH:
                acc = acc * K_SCALE
            if use_rope:
                cos, slo, shi = cos_ref[...], slo_ref[...], shi_ref[...]
                for hh in range(_HEADS_PER_SUB):
                    hs = slice(hh * HEAD_DIM, (hh + 1) * HEAD_DIM)
                    z_ref[:, c0 + hh * HEAD_DIM:c0 + (hh + 1) * HEAD_DIM] = (
                        _rope(acc[:, hs], cos, slo, shi).astype(BF16))
            else:
                z_ref[:, c0:c0 + SUB_TILE] = acc.astype(BF16)
        z_ref[:, QK_COLS:] = jnp.dot(h_ref[...], w_ref[:, QK_COLS:],
                                     preferred_element_type=F32).astype(BF16)

    @pl.when(j != 0)
    def _():
        z_ref[...] = jnp.dot(h_ref[...], w_ref[...], preferred_element_type=F32).astype(BF16)


def _in_proj(h, w_part, mix_part, rope, tm):
    t = h.shape[0]
    width = w_part.shape[1]
    tile = width // IN_TILES
    assert tile * IN_TILES == width and tile % LANES == 0 and (not mix_part or tile >= QK_COLS)
    use_rope = mix_part and rope is not None
    in_specs = [
        pl.BlockSpec((tm, D_MODEL), lambda i, j: (i, 0)),
        pl.BlockSpec((D_MODEL, tile), lambda i, j: (0, j)),
    ]
    args = [h, w_part]
    if use_rope:
        in_specs += [pl.BlockSpec((tm, HEAD_DIM), lambda i, j: (i, 0))] * 3
        args += list(rope)
    return pl.pallas_call(
        functools.partial(_inproj_kernel, mix_part=mix_part, use_rope=use_rope),
        grid=(t // tm, IN_TILES),
        in_specs=in_specs,
        out_specs=pl.BlockSpec((tm, tile), lambda i, j: (i, j)),
        out_shape=jax.ShapeDtypeStruct((t, width), BF16),
        compiler_params=pltpu.CompilerParams(
            dimension_semantics=("arbitrary", "arbitrary"), vmem_limit_bytes=VMEM_LIMIT_BYTES),
        name="in_proj_mix" if mix_part else "in_proj_gate",
    )(*args)


def _inproj_cast_kernel(h_ref, w_ref, z_ref, wb_ref, *, col0):
    tile = w_ref.shape[1]
    wb = w_ref[...].astype(BF16)
    wb_ref[...] = wb
    acc = jnp.dot(h_ref[...], wb, preferred_element_type=F32)
    col = col0 + pl.program_id(0) * tile + lax.broadcasted_iota(jnp.int32, (1, tile), 1)
    is_k = jnp.logical_and(col >= K_COL, col < K_COL + RET_WIDTH)
    z_ref[...] = (acc * jnp.where(is_k, K_SCALE, 1.0)).astype(BF16)


def _in_proj_cast(h, w_in, col0, width, tile):
    t = h.shape[0]
    assert width % tile == 0 and col0 % tile == 0 and tile % LANES == 0
    return pl.pallas_call(
        functools.partial(_inproj_cast_kernel, col0=col0),
        grid=(width // tile,),
        in_specs=[pl.BlockSpec((t, D_MODEL), lambda j: (0, 0)),
                  pl.BlockSpec((None, D_MODEL, tile), lambda j: (0, 0, col0 // tile + j))],
        out_specs=[pl.BlockSpec((t, tile), lambda j: (0, j)),
                   pl.BlockSpec((D_MODEL, tile), lambda j: (0, j))],
        out_shape=[jax.ShapeDtypeStruct((t, width), BF16),
                   jax.ShapeDtypeStruct((D_MODEL, width), BF16)],
        compiler_params=pltpu.CompilerParams(
            dimension_semantics=("arbitrary",), vmem_limit_bytes=VMEM_LIMIT_BYTES),
        name="in_proj_cast",
    )(h, w_in)


def _state_kernel(k_ref, v_ref, kdf_ref, kdb_ref, cdf_ref, cdb_ref, sf0_ref, sb0_ref,
                  sf_all_ref, sb_all_ref, sf_fin_ref, sb_fin_ref, sf_scr, sb_scr):
    t = pl.program_id(0)
    nb = pl.num_programs(0)
    cps = sf_all_ref.shape[0]

    @pl.when(t == 0)
    def _():
        sf_scr[...] = sf0_ref[...]
        sb_scr[...] = sb0_ref[...]

    contract_rows = (((0,), (0,)), ((), ()))

    def advance(block, c, kd_ref, cd_ref, all_ref, scr):
        rows = pl.ds(pl.multiple_of((block * cps + c) * CHUNK, CHUNK), CHUNK)
        all_ref[c] = scr[...].astype(BF16)
        for h in range(RET_HEADS):
            hs = slice(h * HEAD_DIM, (h + 1) * HEAD_DIM)
            kd = (k_ref[rows, hs].astype(F32) * kd_ref[:, hs]).astype(BF16)
            kv = lax.dot_general(kd, v_ref[rows, hs], contract_rows, preferred_element_type=F32)
            scr[h] = scr[h] * cd_ref[h] + kv

    for c in range(cps):
        advance(t, c, kdf_ref, cdf_ref, sf_all_ref, sf_scr)
        advance(nb - 1 - t, cps - 1 - c, kdb_ref, cdb_ref, sb_all_ref, sb_scr)

    @pl.when(t == pl.num_programs(0) - 1)
    def _():
        sf_fin_ref[...] = sf_scr[...]
        sb_fin_ref[...] = sb_scr[...]


def _states(z, l, tabs, sf0, sb0, cps):
    t = z.shape[0]
    nb = t // (cps * CHUNK)
    whole = lambda c: pl.BlockSpec((t, RET_WIDTH), lambda i: (0, c), pipeline_mode=pl.Buffered(1))
    state_shape = (RET_HEADS, HEAD_DIM, HEAD_DIM)
    consts = [tabs["kdf"], tabs["kdb"], tabs["cdf"], tabs["cdb"]]
    kb, vb = K_COL // RET_WIDTH, V_COL // RET_WIDTH
    in_specs = [whole(kb), whole(vb)] + [_layer_spec(a, l) for a in consts]
    in_specs += [_const_spec(sf0), _const_spec(sb0)]
    all_shape = jax.ShapeDtypeStruct((t // CHUNK,) + state_shape, BF16)
    fin_shape = jax.ShapeDtypeStruct(state_shape, F32)
    return pl.pallas_call(
        _state_kernel,
        grid=(nb,),
        in_specs=in_specs,
        out_specs=[
            pl.BlockSpec((cps,) + state_shape, lambda i: (i, 0, 0, 0)),
            pl.BlockSpec((cps,) + state_shape, lambda i: (nb - 1 - i, 0, 0, 0)),
            pl.BlockSpec(state_shape, lambda i: (0, 0, 0)),
            pl.BlockSpec(state_shape, lambda i: (0, 0, 0)),
        ],
        out_shape=[all_shape, all_shape, fin_shape, fin_shape],
        scratch_shapes=[pltpu.VMEM(state_shape, F32), pltpu.VMEM(state_shape, F32)],
        compiler_params=pltpu.CompilerParams(
            dimension_semantics=("arbitrary",), vmem_limit_bytes=VMEM_LIMIT_BYTES),
        name="ret_states",
    )(z, z, *consts, sf0, sb0)


def _mixer_kernel(*refs, mod_row):
    (zmix_ref, sf_ref, sb_ref, zgate_ref, x_ref,
     dmat_ref, qdf_ref, qdb_ref, retg_ref, pmask_ref, pinv_ref, poolw_ref, pscale_ref,
     sgng_ref, sgw_ref, btab_ref, wbr_ref, wbp_ref, wbs_ref, wout_ref,
     gate_ref, n2g_ref, shift2_ref, scale2_ref,
     x1_ref, h2_ref,
     y_scr, ret_cur, pool_cur, sg_cur, ret_prev, pool_prev, sg_prev) = refs
    tb = x_ref.shape[0]
    nck = tb // CHUNK
    contract_last = (((1,), (1,)), ((), ()))
    mr = slice(mod_row, mod_row + 1)

    def merge_piece(i):
        cs = slice(i * MERGE_TILE, (i + 1) * MERGE_TILE)
        def gate(branch):
            c0 = branch * D_MODEL + i * MERGE_TILE
            return jax.nn.sigmoid(zgate_ref[:, c0:c0 + MERGE_TILE].astype(F32))

        y = gate(0) * jnp.dot(ret_prev[...], wbr_ref[:, cs], preferred_element_type=F32)
        y = y + gate(1) * jnp.dot(pool_prev[...], wbp_ref[:, cs], preferred_element_type=F32)
        y = y + gate(2) * jnp.dot(sg_prev[...], wbs_ref[:, cs], preferred_element_type=F32)
        y_scr[:, cs] = y.astype(BF16)

    def out_piece(i):
        cs = slice(i * MERGE_TILE, (i + 1) * MERGE_TILE)
        out = jnp.dot(y_scr[...], wout_ref[:, cs], preferred_element_type=F32)
        x1_ref[:, cs] = x_ref[:, cs] + gate_ref[mr, cs] * out

    def ret_unit(c, h):
        rows = slice(c * CHUNK, (c + 1) * CHUNK)
        hs = slice(h * HEAD_DIM, (h + 1) * HEAD_DIM)
        head = lambda col: zmix_ref[rows, col + h * HEAD_DIM:col + (h + 1) * HEAD_DIM]
        qb = head(Q_COL)
        scores = lax.dot_general(qb, head(K_COL), contract_last, preferred_element_type=F32)
        scores = (scores * dmat_ref[h]).astype(BF16)
        cross = qdf_ref[:, hs] * jnp.dot(qb, sf_ref[c, h], preferred_element_type=F32)
        cross = cross + qdb_ref[:, hs] * jnp.dot(qb, sb_ref[c, h], preferred_element_type=F32)

        def second():
            o = jnp.dot(scores, head(V_COL), preferred_element_type=F32) + cross
            g = head(G_COL).astype(F32)
            ret = _rms(o) * retg_ref[:, hs] * (g * jax.nn.sigmoid(g))
            ret_cur[rows, hs] = ret.astype(BF16)

        return second

    def pool_unit(gi):
        gs = slice(gi * GROUP_DIM, (gi + 1) * GROUP_DIM)
        pb = zmix_ref[:, POOL_COL + gi * GROUP_DIM:POOL_COL + (gi + 1) * GROUP_DIM]
        win = jnp.dot(pmask_ref[gi], pb, preferred_element_type=F32)
        pooled = (win * pinv_ref[:, gs] - pb.astype(F32)).astype(BF16)

        def second():
            mapped = jnp.dot(pooled, poolw_ref[gi], preferred_element_type=F32)
            pool_cur[:, gs] = (mapped * pscale_ref[:, gs]).astype(BF16)

        return second

    def sg_unit(c):
        rows = slice(c * CHUNK, (c + 1) * CHUNK)
        sv = zmix_ref[rows, SV_COL:SV_COL + SG_WIDTH].astype(F32)
        sv = (_rms(jax.nn.gelu(sv)) * sgng_ref[...]).astype(BF16)

        def second():
            for gi in range(SG_GROUPS):
                gs = slice(gi * GROUP_DIM, (gi + 1) * GROUP_DIM)
                mixed = jnp.dot(sgw_ref[gi], sv[:, gs], preferred_element_type=F32)
                mixed = mixed + btab_ref[:, gs]
                u = zmix_ref[rows, U_COL + gi * GROUP_DIM:U_COL + (gi + 1) * GROUP_DIM]
                u = jax.nn.gelu(u.astype(F32))
                sg_cur[rows, gs] = (u * mixed).astype(BF16)

        return second

    mix_units = [functools.partial(ret_unit, c, h) for c in range(nck) for h in range(RET_HEADS)]
    mix_units += [functools.partial(pool_unit, gi) for gi in range(POOL_GROUPS)]
    mix_units += [functools.partial(sg_unit, c) for c in range(nck)]
    pieces = ([functools.partial(merge_piece, i) for i in range(N_MERGE)]
              + [functools.partial(out_piece, i) for i in range(N_MERGE)])

    def run(mix, merge):
        units = mix_units if mix else []
        per_piece = -(-len(units) // len(pieces))
        pending = [unit() for unit in units[:per_piece]]
        for i, piece in enumerate(pieces):
            if merge:
                piece()
            for second in pending:
                second()
            pending = [unit() for unit in units[(i + 1) * per_piece:(i + 2) * per_piece]]
        assert not pending
        if merge:
            gain2 = n2g_ref[...] * (1.0 + scale2_ref[mr, :])
            h2_ref[...] = (_rms(x1_ref[...]) * gain2 + shift2_ref[mr, :]).astype(BF16)
        if mix:
            ret_prev[...] = ret_cur[...]
            pool_prev[...] = pool_cur[...]
            sg_prev[...] = sg_cur[...]

    s = pl.program_id(0)
    last = pl.num_programs(0) - 1
    pl.when(s == 0)(functools.partial(run, True, False))
    pl.when(jnp.logical_and(s > 0, s < last))(functools.partial(run, True, True))
    pl.when(s == last)(functools.partial(run, False, True))


def _mixer(z_mix, z_gate, x, sf_all, sb_all, l, tabs, pool_tabs, p, mods_all, mod_row, tb):
    t = x.shape[0]
    nb = t // tb
    nck = tb // CHUNK
    cur = lambda s: jnp.minimum(s, nb - 1)
    prev = lambda s: jnp.maximum(s - 1, 0)

    state_spec = pl.BlockSpec((nck, RET_HEADS, HEAD_DIM, HEAD_DIM), lambda s: (cur(s), 0, 0, 0))
    row_spec = pl.BlockSpec((tb, D_MODEL), lambda s: (prev(s), 0))
    in_specs = [pl.BlockSpec((tb, MIX_COLS), lambda s: (cur(s), 0)), state_spec, state_spec,
                pl.BlockSpec((tb, GATE_COLS), lambda s: (prev(s), 0)), row_spec]
    args = [z_mix, sf_all, sb_all, z_gate, x]
    pmask, pinv = pool_tabs
    layer_consts = [tabs["dmat"], tabs["qdf"], tabs["qdb"], p["ret_norm_g"]]
    in_specs += [_layer_spec(a, l) for a in layer_consts] + [_const_spec(pmask), _const_spec(pinv)]
    args += layer_consts + [pmask, pinv]
    layer_consts = [p["pool_w"], p["pool_scale"], p["sg_norm_g"], p["sg_w"], tabs["btab"],
                    p["w_br"], p["w_bp"], p["w_bs"], p["w_out"]]
    in_specs += [_layer_spec(a, l) for a in layer_consts]
    args += layer_consts
    in_specs += [_mod_spec(l, 2), _gain_spec(l), _mod_spec(l, 3), _mod_spec(l, 4)]
    args += [mods_all, p["norm2_g"], mods_all, mods_all]
    mix_scratch = [pltpu.VMEM((tb, RET_WIDTH), BF16), pltpu.VMEM((tb, POOL_WIDTH), BF16),
                   pltpu.VMEM((tb, SG_WIDTH), BF16)]
    return pl.pallas_call(
        functools.partial(_mixer_kernel, mod_row=mod_row),
        grid=(nb + 1,),
        in_specs=in_specs,
        out_specs=[row_spec, row_spec],
        out_shape=[jax.ShapeDtypeStruct((t, D_MODEL), F32),
                   jax.ShapeDtypeStruct((t, D_MODEL), BF16)],
        scratch_shapes=[pltpu.VMEM((tb, D_MODEL), BF16)] + mix_scratch + mix_scratch,
        compiler_params=pltpu.CompilerParams(
            dimension_semantics=("arbitrary",), vmem_limit_bytes=VMEM_LIMIT_BYTES),
        name="mixer",
    )(*args)


EPILOGUE_ROWS = 16
CAST_EVERY = 2


def _mlp_kernel(*refs, mod_row, final, n_cast, emit_weights):
    refs = list(refs)
    w1b_ref = w2b_ref = None
    if emit_weights:
        w1b_ref, w2b_ref = refs[-2:]
        refs = refs[:-2]
    cast_src = cast_dst = ()
    if n_cast:
        n_dst = n_cast + 1
        cast_dst = refs[-n_dst:]
        refs = refs[:-n_dst]
        n_out = 1 if final else 2
        cast_src = refs[-n_out - n_cast:-n_out]
        refs = refs[:-n_out - n_cast] + refs[-n_out:]
    if final:
        h2_ref, x1_ref, w1_ref, w2_ref, gate_ref, fng_ref, o_ref = refs
    else:
        (h2_ref, x1_ref, w1_ref, w2_ref, gate_ref, ng_ref, nshift_ref, nscale_ref,
         o_ref, hn_ref) = refs
    f = pl.program_id(1)
    mr = slice(mod_row, mod_row + 1)

    def step(first):
        w1, w2 = w1_ref[...], w2_ref[...]
        if emit_weights:
            w1, w2 = w1.astype(BF16), w2.astype(BF16)
            w1b_ref[...] = w1
            w2b_ref[...] = w2
        a = jnp.maximum(jnp.dot(h2_ref[...], w1, preferred_element_type=F32), 0.0)
        part = jnp.dot((a * a).astype(BF16), w2, preferred_element_type=F32)
        if first:
            o_ref[...] = part
        else:
            o_ref[...] += part
        part_id = (pl.program_id(0) * pl.num_programs(1) + f) % CAST_EVERY
        for k, src in enumerate(cast_src):
            share = src.shape[0] // CAST_EVERY
            rows = pl.ds(pl.multiple_of(part_id * share, share), share)
            if k == 0:
                _cast_w_in(src[rows, :], cast_dst[0].at[rows], cast_dst[1].at[rows])
            else:
                cast_dst[k + 1][rows, :] = src[rows, :].astype(BF16)

    @pl.when(f == 0)
    def _():
        step(True)

    @pl.when(f > 0)
    def _():
        step(False)

    @pl.when(f == pl.num_programs(1) - 1)
    def _():
        gate = gate_ref[mr, :]
        if final:
            gain = fng_ref[...]
        else:
            gain = ng_ref[...] * (1.0 + nscale_ref[mr, :])
            shift = nshift_ref[mr, :]

        for r in range(o_ref.shape[0] // EPILOGUE_ROWS):
            rows = slice(r * EPILOGUE_ROWS, (r + 1) * EPILOGUE_ROWS)
            x2 = x1_ref[rows, :] + gate * o_ref[rows, :]
            if final:
                o_ref[rows, :] = _rms(x2) * gain
            else:
                o_ref[rows, :] = x2
                hn_ref[rows, :] = (_rms(x2) * gain + shift).astype(BF16)


def _mlp(h2, x1, l, w1, w2, mods_all, mod_row, norm1_g, final_gain, tm, tf, cast_next=()):
    t = x1.shape[0]
    final = final_gain is not None
    n_f = D_FF // tf
    grid = (t // tm, n_f)
    rows = pl.BlockSpec((tm, D_MODEL), lambda i, f: (i, 0))
    emit_weights = w1.ndim == 3
    assert not emit_weights or (grid[0] == 1 and not cast_next)
    lead = (None,) if emit_weights else ()
    at = (lambda *idx: (l,) + idx) if emit_weights else (lambda *idx: idx)
    in_specs = [rows, rows,
                pl.BlockSpec(lead + (D_MODEL, tf), lambda i, f: at(0, f)),
                pl.BlockSpec(lead + (tf, D_MODEL), lambda i, f: at(f, 0)),
                _mod_spec(l, 5)]
    args = [h2, x1, w1, w2, mods_all]
    x_shape = jax.ShapeDtypeStruct((t, D_MODEL), F32)
    if final:
        in_specs.append(pl.BlockSpec((1, D_MODEL), lambda i, f: (0, 0)))
        args.append(final_gain)
        out_specs, out_shape = [rows], [x_shape]
    else:
        in_specs += [_gain_spec(l + 1), _mod_spec(l + 1, 0), _mod_spec(l + 1, 1)]
        args += [norm1_g, mods_all, mods_all]
        out_specs = [rows, rows]
        out_shape = [x_shape, jax.ShapeDtypeStruct((t, D_MODEL), BF16)]
    for k, w in enumerate(cast_next):
        n_slabs = grid[0] * grid[1] // CAST_EVERY
        slab, cols = w.shape[1] // n_slabs, w.shape[2]
        assert slab * n_slabs == w.shape[1] and slab % (2 * SUBLANES) == 0
        in_specs.append(pl.BlockSpec((None, slab, cols),
                                     lambda i, f: (l + 1, (i * n_f + f) // CAST_EVERY, 0)))
        args.append(w)
        for out_cols in ((MIX_COLS, GATE_COLS) if k == 0 else (cols,)):
            out_specs.append(pl.BlockSpec((slab, out_cols),
                                          lambda i, f: ((i * n_f + f) // CAST_EVERY, 0)))
            out_shape.append(jax.ShapeDtypeStruct((w.shape[1], out_cols), BF16))
    if emit_weights:
        out_specs += [pl.BlockSpec((D_MODEL, tf), lambda i, f: (0, f)),
                      pl.BlockSpec((tf, D_MODEL), lambda i, f: (f, 0))]
        out_shape += [jax.ShapeDtypeStruct(w1.shape[1:], BF16), jax.ShapeDtypeStruct(w2.shape[1:], BF16)]
    return pl.pallas_call(
        functools.partial(_mlp_kernel, mod_row=mod_row, final=final, n_cast=len(cast_next),
                          emit_weights=emit_weights),
        grid=grid,
        in_specs=in_specs,
        out_specs=out_specs,
        out_shape=out_shape,
        compiler_params=pltpu.CompilerParams(
            dimension_semantics=("arbitrary", "arbitrary"), vmem_limit_bytes=VMEM_LIMIT_BYTES),
        name="mlp",
    )(*args)


def _rope_tables(t):
    half = HEAD_DIM // 2
    nf = half // 2
    tok = np.arange(t)
    inv = ROPE_THETA ** (-np.arange(nf, dtype=np.float64) / nf)
    lane = np.arange(HEAD_DIM)
    pos = np.where(lane[None, :] < half, (tok // GRID_W)[:, None], (tok % GRID_W)[:, None])
    ang = pos * inv[lane % nf][None, :]
    lower = (lane % half) < nf
    sin = np.sin(ang)
    cos = np.cos(ang).astype(np.float32)
    sin_lo = np.where(lower[None, :], -sin, 0.0).astype(np.float32)
    sin_hi = np.where(lower[None, :], 0.0, sin).astype(np.float32)
    return jnp.asarray(cos), jnp.asarray(sin_lo), jnp.asarray(sin_hi)


def _pool_tables(tb, seg_len):
    pos = np.arange(tb) % seg_len
    base = np.arange(tb) - pos
    col = np.arange(tb)[None, :]
    masks, invs = [], []
    for w in POOL_WINDOWS:
        lo = np.maximum(pos - w // 2, 0)
        hi = np.minimum(pos + w // 2 - 1, seg_len - 1)
        masks.append((col >= (base + lo)[:, None]) & (col <= (base + hi)[:, None]))
        invs.append(np.repeat((1.0 / (hi - lo + 1))[:, None], GROUP_DIM, axis=1))
    pmask = jnp.asarray(np.stack(masks).astype(np.float32), dtype=BF16)
    pinv = jnp.asarray(np.concatenate(invs, axis=1).astype(np.float32))
    return pmask, pinv


def _decay_tables(logit):
    lg = jax.nn.log_sigmoid(logit.astype(F32))
    lgf, lgb = lg[0], lg[1]
    idx = jnp.arange(CHUNK, dtype=F32)
    dist = idx[:, None] - idx[None, :]
    fwd = jnp.exp(lgf[:, None, None] * jnp.maximum(dist, 0.0))
    bwd = jnp.exp(lgb[:, None, None] * jnp.maximum(-dist, 0.0))
    dmat = jnp.where(dist > 0, fwd, jnp.where(dist < 0, bwd, 2.0))

    def lanes(tab):
        return jnp.repeat(tab.T, HEAD_DIM, axis=1)

    return dict(
        dmat=dmat,
        qdf=lanes(jnp.exp(lgf[:, None] * (idx + 1.0)[None])),
        qdb=lanes(jnp.exp(lgb[:, None] * (CHUNK - idx)[None])),
        kdf=lanes(jnp.exp(lgf[:, None] * (CHUNK - 1.0 - idx)[None])),
        kdb=lanes(jnp.exp(lgb[:, None] * idx[None])),
        cdf=jnp.broadcast_to(jnp.exp(lgf * CHUNK)[:, None, None], (RET_HEADS, 1, HEAD_DIM)),
        cdb=jnp.broadcast_to(jnp.exp(lgb * CHUNK)[:, None, None], (RET_HEADS, 1, HEAD_DIM)),
    )


LATENT_ROWS_IN = 1024
LATENT_ROWS_MIX = 256
LATENT_ROWS_MLP = 512
LATENT_FF_TILE = 1024
LATENT_ROWS_NORM = 1024
LATENT_STATE_CHUNKS = 8
FF_TILE = 2048
FIRST_FF_TILE = 1024
CTX_MIX_TILE = MIX_COLS // 4
CTX_GATE_TILE = 512


def kernel(x, c, ctx, c_ctx, w_ada, b_ada, norm1_g, w_in, ret_decay_logit, ret_norm_g, pool_w,
           pool_scale, sg_norm_g, sg_w, sg_b, w_br, w_bp, w_bs, w_out, norm2_g, w1, w2, final_norm_g):
    assert x.shape[0] == 1 and ctx.shape[0] == 1
    t = x.shape[1]
    tc = ctx.shape[1]
    xs = x[0]
    xc = ctx[0]

    cond = jnp.stack([c[0], c_ctx])[:, :, None]
    mods_all = _ada(cond, w_ada, b_ada)

    rope = _rope_tables(t)
    pool_lat = _pool_tables(LATENT_ROWS_MIX, GRID_W)
    pool_ctx = _pool_tables(tc, tc)
    zero_state = jnp.zeros((RET_HEADS, HEAD_DIM, HEAD_DIM), F32)

    tabs = jax.vmap(_decay_tables)(ret_decay_logit)
    tabs["btab"] = jnp.repeat(jnp.swapaxes(sg_b, 1, 2), GROUP_DIM, axis=2)
    vec = lambda v: v.reshape(DEPTH, 1, -1)
    p = dict(ret_norm_g=vec(ret_norm_g), pool_w=pool_w.astype(BF16), pool_scale=vec(pool_scale),
             sg_norm_g=vec(sg_norm_g), sg_w=sg_w.astype(BF16),
             w_br=w_br.astype(BF16), w_bp=w_bp.astype(BF16), w_bs=w_bs.astype(BF16),
             w_out=w_out.astype(BF16), norm2_g=vec(norm2_g))
    n1g = vec(norm1_g)
    fng = final_norm_g.reshape(1, -1)
    streamed = (w_in, w1, w2)
    assert DEPTH > 1

    hc = _norm_mod(xc, 0, n1g, mods_all, ROW_CONTEXT, tc)
    hx = _norm_mod(xs, 0, n1g, mods_all, ROW_LATENT, LATENT_ROWS_NORM)
    for l in range(DEPTH):
        last = l == DEPTH - 1

        if l == 0:
            zc, w_mix_b = _in_proj_cast(hc, w_in, 0, MIX_COLS, CTX_MIX_TILE)
            gc, w_gate_b = _in_proj_cast(hc, w_in, MIX_COLS, GATE_COLS, CTX_GATE_TILE)
        else:
            zc = _in_proj(hc, w_mix_b, True, None, tc)
        sfc, sbc, sf, sb = _states(zc, l, tabs, zero_state, zero_state, tc // CHUNK)
        if not last:
            if l > 0:
                gc = _in_proj(hc, w_gate_b, False, None, tc)
            xc1, hc2 = _mixer(zc, gc, xc, sfc, sbc, l, tabs, pool_ctx, p, mods_all, ROW_CONTEXT, tc)
            if l == 0:
                xc, hc, w1_b, w2_b = _mlp(hc2, xc1, l, w1, w2, mods_all, ROW_CONTEXT, n1g, None,
                                          tc, FIRST_FF_TILE)
            else:
                xc, hc = _mlp(hc2, xc1, l, w1_b, w2_b, mods_all, ROW_CONTEXT, n1g, None, tc, FF_TILE)

        zx = _in_proj(hx, w_mix_b, True, rope, LATENT_ROWS_IN)
        gx = _in_proj(hx, w_gate_b, False, None, LATENT_ROWS_IN)
        sfx, sbx, _, _ = _states(zx, l, tabs, sf, sb, LATENT_STATE_CHUNKS)
        x1, h2 = _mixer(zx, gx, xs, sfx, sbx, l, tabs, pool_lat, p, mods_all, ROW_LATENT,
                        LATENT_ROWS_MIX)
        if last:
            xs, = _mlp(h2, x1, l, w1_b, w2_b, mods_all, ROW_LATENT, n1g, fng,
                       LATENT_ROWS_MLP, LATENT_FF_TILE)
        else:
            xs, hx, w_mix_b, w_gate_b, w1_b, w2_b = _mlp(
                h2, x1, l, w1_b, w2_b, mods_all, ROW_LATENT, n1g, None,
                LATENT_ROWS_MLP, LATENT_FF_TILE, cast_next=streamed)

    return xs[None]
```
